```python
import jax, jax.numpy as jnp
from jax import lax
import numpy as np

D_MODEL = 2048
BATCH = 2
SEQ = 8192
DEPTH = 1
DEC_BATCH = 16
DEC_SEQ = 32
PAST_LEN = 4096

CHUNK = 64
RNN_WIDTH = 1024
RNN_BLOCKS = 16
RNN_BW = RNN_WIDTH // RNN_BLOCKS
RNN_CONV_W = 4
LRU_C = 8.0
HG_HEADS = 8
HG_DK = 128
HG_DV = 128
HG_WIDTH = HG_HEADS * HG_DK
MEM_LEN = 256
XA_HEADS = 4
XA_HD = 256
XA_WIDTH = XA_HEADS * XA_HD
N_BRANCH = 3
BRANCH_WIDTH = 1024
FFN_DIM = 5632
FFN_CONV_W = 3
EPS = 1e-6

OFF_RNN = 0
OFF_HQ = OFF_RNN + RNN_WIDTH
OFF_HF = OFF_HQ + HG_WIDTH
OFF_HI = OFF_HF + HG_WIDTH
OFF_HO = OFF_HI + HG_HEADS * HG_DV
OFF_XQ = OFF_HO + HG_HEADS * HG_DV
OFF_GATE = OFF_XQ + XA_WIDTH
IN_COLS = OFF_GATE + N_BRANCH * D_MODEL

kernel_name = "hawk_hgrn2_memxattn_convffn_stream_step"

F32 = jnp.float32


def rmsnorm(x, g):
    xf = x.astype(F32)
    y = xf * lax.rsqrt(jnp.mean(xf * xf, axis=-1, keepdims=True) + EPS)
    return (y * g.astype(F32)).astype(x.dtype)


def causal_dwconv(x, prev, w, b):
    width = w.shape[0]
    L = x.shape[1]
    xp = jnp.concatenate([prev.astype(x.dtype), x], axis=1)
    y = b
    for j in range(width):
        y = y + xp[:, j:j + L] * w[j]
    return y.astype(x.dtype), xp[:, L:]


def _lin_combine(left, right):
    a1, b1 = left
    a2, b2 = right
    return a1 * a2, a2 * b1 + b2


def rg_lru(x, h0, wa, ba, wx, bx, lam):
    B, L, C = x.shape
    xf = x.astype(F32)
    xb = xf.reshape(B, L, RNN_BLOCKS, RNN_BW)
    r = jax.nn.sigmoid(jnp.einsum('blhi,hij->blhj', xb, wa.astype(F32)) + ba.astype(F32)).reshape(B, L, C)
    ig = jax.nn.sigmoid(jnp.einsum('blhi,hij->blhj', xb, wx.astype(F32)) + bx.astype(F32)).reshape(B, L, C)
    log_a = -LRU_C * r * jax.nn.softplus(-lam.astype(F32))
    a = jnp.exp(log_a)
    b = jnp.sqrt(-jnp.expm1(2.0 * log_a)) * (ig * xf)
    b = b.at[:, 0].add(a[:, 0] * h0.astype(F32))
    _, h = lax.associative_scan(_lin_combine, (a, b), axis=1)
    return h.astype(x.dtype), h[:, -1].astype(x.dtype)


def hgrn2(q, f_raw, i, S0, lb):
    B, L, _ = q.shape
    c = min(CHUNK, L)
    n = L // c
    f = lb + (1.0 - lb) * jax.nn.sigmoid(f_raw.astype(F32))
    g = jnp.log(f)
    k = 1.0 - f

    def blocks(t, d):
        return t.astype(F32).reshape(B, n, c, HG_HEADS, d).transpose(1, 0, 3, 2, 4)

    qs, ks, gs, vs = blocks(q, HG_DK), blocks(k, HG_DK), blocks(g, HG_DK), blocks(i, HG_DV)
    tri = jnp.tril(jnp.ones((c, c), dtype=bool))[:, :, None]

    def step(S, inp):
        qc, kc, vc, gc = inp
        bcum = jnp.cumsum(gc, axis=2)
        o_inter = jnp.einsum('bhtk,bhkv->bhtv', qc * jnp.exp(bcum), S)
        diff = bcum[:, :, :, None, :] - bcum[:, :, None, :, :]
        decay = jnp.where(tri, jnp.exp(jnp.minimum(diff, 0.0)), 0.0)
        A = jnp.einsum('bhtk,bhsk,bhtsk->bhts', qc, kc, decay)
        o = o_inter + jnp.einsum('bhts,bhsv->bhtv', A, vc)
        blast = bcum[:, :, -1:, :]
        S_new = jnp.exp(blast[:, :, 0])[..., None] * S + jnp.einsum(
            'bhsk,bhsv->bhkv', kc * jnp.exp(blast - bcum), vc)
        return S_new, o

    S, o = lax.scan(step, S0.astype(F32), (qs, ks, vs, gs))
    o = o.transpose(1, 0, 3, 2, 4).reshape(B, L, HG_HEADS, HG_DV)
    return o, S


def memory_kv(mem, g, w_kv):
    B, M, _ = mem.shape
    kv = rmsnorm(mem, g) @ w_kv
    k = kv[..., :XA_WIDTH].reshape(B, M, XA_HEADS, XA_HD)
    v = kv[..., XA_WIDTH:].reshape(B, M, XA_HEADS, XA_HD)
    return k, v


def mem_cross_attn(q, mk, mv):
    B, L, _ = q.shape
    qh = q.reshape(B, L, XA_HEADS, XA_HD)
    s = jnp.einsum('blhd,bmhd->bhlm', qh, mk.astype(q.dtype)).astype(F32) * (XA_HD ** -0.5)
    p = jax.nn.softmax(s, axis=-1).astype(q.dtype)
    return jnp.einsum('bhlm,bmhd->blhd', p, mv.astype(q.dtype)).reshape(B, L, XA_WIDTH)


def trunk_layer(x, mk, mv, h0, rconv0, S0, fconv0, p, lb):
    B, L, _ = x.shape
    xn = rmsnorm(x, p['pre_mix_norm'])
    z = xn @ p['w_in']
    xr, rconv1 = causal_dwconv(z[..., OFF_RNN:OFF_HQ], rconv0, p['rnn_conv_w'], p['rnn_conv_b'])
    y_a, h1 = rg_lru(xr, h0, p['lru_wa'], p['lru_ba'], p['lru_wx'], p['lru_bx'], p['lru_lambda'])
    o, S1 = hgrn2(z[..., OFF_HQ:OFF_HF], z[..., OFF_HF:OFF_HI], z[..., OFF_HI:OFF_HO], S0, lb)
    og = jax.nn.sigmoid(z[..., OFF_HO:OFF_XQ].astype(F32)).reshape(B, L, HG_HEADS, HG_DV)
    y_b = (rmsnorm(o, p['hg_norm']) * og).reshape(B, L, HG_HEADS * HG_DV).astype(x.dtype)
    y_c = mem_cross_attn(z[..., OFF_XQ:OFF_GATE], mk, mv)
    branches = (y_a, y_b, y_c)
    m = jnp.zeros((B, L, D_MODEL), F32)
    for nb in range(N_BRANCH):
        gl = z[..., OFF_GATE + nb * D_MODEL:OFF_GATE + (nb + 1) * D_MODEL] + p['b_gate'][nb]
        m = m + jax.nn.sigmoid(gl.astype(F32)) * (branches[nb] @ p['w_branch'][nb]).astype(F32)
    y = m.astype(x.dtype) @ p['w_out']
    x = x + rmsnorm(y, p['post_mix_norm'])
    hf = rmsnorm(x, p['pre_ffn_norm']) @ p['w_ffn_up']
    u, fconv1 = causal_dwconv(hf[..., :FFN_DIM], fconv0, p['ffn_conv_w'], p['ffn_conv_b'])
    act = jax.nn.gelu(u) * hf[..., FFN_DIM:]
    x = x + rmsnorm(act @ p['w_ffn_down'], p['post_ffn_norm'])
    return x, h1, rconv1, S1.astype(x.dtype), fconv1


def setup_inputs(seed: int = 0) -> dict:
    key = jax.random.key(seed)
    kit = iter(jax.random.split(key, 48))

    def nrm(shape, scale):
        return jax.random.normal(next(kit), shape, F32) * scale

    def gain(shape):
        return 1.0 + nrm(shape, 0.05)

    u = jax.random.uniform(next(kit), (DEPTH, RNN_WIDTH), F32, minval=0.9, maxval=0.999)
    s = u ** (1.0 / LRU_C)
    lam = jnp.log(s) - jnp.log1p(-s)
    return {
        "x_prompt": nrm((BATCH, SEQ, D_MODEL), 1.0),
        "x_sample": nrm((DEC_BATCH, DEC_SEQ, D_MODEL), 1.0),
        "cache_mem_k": nrm((DEPTH, DEC_BATCH, MEM_LEN, XA_HEADS, XA_HD), 1.0),
        "cache_mem_v": nrm((DEPTH, DEC_BATCH, MEM_LEN, XA_HEADS, XA_HD), 1.0),
        "state_rnn_h": nrm((DEPTH, DEC_BATCH, RNN_WIDTH), 0.5),
        "state_rnn_conv": nrm((DEPTH, DEC_BATCH, RNN_CONV_W - 1, RNN_WIDTH), 1.0),
        "state_hg": nrm((DEPTH, DEC_BATCH, HG_HEADS, HG_DK, HG_DV), 0.5),
        "state_ffn_conv": nrm((DEPTH, DEC_BATCH, FFN_CONV_W - 1, FFN_DIM), 1.0),
        "mem_prompt": nrm((BATCH, MEM_LEN, D_MODEL), 1.0),
        "pre_mix_norm": gain((DEPTH, D_MODEL)),
        "w_in": nrm((DEPTH, D_MODEL, IN_COLS), D_MODEL ** -0.5),
        "rnn_conv_w": nrm((DEPTH, RNN_CONV_W, RNN_WIDTH), RNN_CONV_W ** -0.5),
        "rnn_conv_b": nrm((DEPTH, RNN_WIDTH), 0.01),
        "lru_wa": nrm((DEPTH, RNN_BLOCKS, RNN_BW, RNN_BW), RNN_BW ** -0.5),
        "lru_ba": nrm((DEPTH, RNN_BLOCKS, RNN_BW), 0.01),
        "lru_wx": nrm((DEPTH, RNN_BLOCKS, RNN_BW, RNN_BW), RNN_BW ** -0.5),
        "lru_bx": nrm((DEPTH, RNN_BLOCKS, RNN_BW), 0.01),
        "lru_lambda": lam,
        "hg_lb": nrm((DEPTH + 1, HG_WIDTH), 0.5),
        "hg_norm": gain((DEPTH, HG_DV)),
        "mem_norm": gain((DEPTH, D_MODEL)),
        "w_mem_kv": nrm((DEPTH, D_MODEL, 2 * XA_WIDTH), D_MODEL ** -0.5),
        "w_branch": nrm((DEPTH, N_BRANCH, BRANCH_WIDTH, D_MODEL), BRANCH_WIDTH ** -0.5),
        "b_gate": nrm((DEPTH, N_BRANCH, D_MODEL), 0.01),
        "w_out": nrm((DEPTH, D_MODEL, D_MODEL), D_MODEL ** -0.5),
        "post_mix_norm": gain((DEPTH, D_MODEL)),
        "pre_ffn_norm": gain((DEPTH, D_MODEL)),
        "w_ffn_up": nrm((DEPTH, D_MODEL, 2 * FFN_DIM), D_MODEL ** -0.5),
        "ffn_conv_w": nrm((DEPTH, FFN_CONV_W, FFN_DIM), FFN_CONV_W ** -0.5),
        "ffn_conv_b": nrm((DEPTH, FFN_DIM), 0.01),
        "w_ffn_down": nrm((DEPTH, FFN_DIM, D_MODEL), FFN_DIM ** -0.5),
        "post_ffn_norm": gain((DEPTH, D_MODEL)),
    }


def reference(x_prompt, x_sample, cache_mem_k, cache_mem_v, state_rnn_h, state_rnn_conv, state_hg,
              state_ffn_conv, mem_prompt, pre_mix_norm, w_in, rnn_conv_w, rnn_conv_b, lru_wa, lru_ba,
              lru_wx, lru_bx, lru_lambda, hg_lb, hg_norm, mem_norm, w_mem_kv, w_branch, b_gate, w_out,
              post_mix_norm, pre_ffn_norm, w_ffn_up, ffn_conv_w, ffn_conv_b, w_ffn_down, post_ffn_norm):
    lb_all = jnp.cumsum(jax.nn.softmax(hg_lb.astype(F32), axis=0), axis=0)
    yp, ys = x_prompt, x_sample
    dt = x_prompt.dtype
    mk_p_l, mv_p_l, hp_l, rcp_l, sp_l, fcp_l = [], [], [], [], [], []
    hs_l, rcs_l, ss_l, fcs_l = [], [], [], []
    for l in range(DEPTH):
        p = {
            'pre_mix_norm': pre_mix_norm[l], 'w_in': w_in[l], 'rnn_conv_w': rnn_conv_w[l],
            'rnn_conv_b': rnn_conv_b[l], 'lru_wa': lru_wa[l], 'lru_ba': lru_ba[l], 'lru_wx': lru_wx[l],
            'lru_bx': lru_bx[l], 'lru_lambda': lru_lambda[l], 'hg_norm': hg_norm[l],
            'w_branch': w_branch[l], 'b_gate': b_gate[l], 'w_out': w_out[l],
            'post_mix_norm': post_mix_norm[l], 'pre_ffn_norm': pre_ffn_norm[l], 'w_ffn_up': w_ffn_up[l],
            'ffn_conv_w': ffn_conv_w[l], 'ffn_conv_b': ffn_conv_b[l], 'w_ffn_down': w_ffn_down[l],
            'post_ffn_norm': post_ffn_norm[l],
        }
        lb = lb_all[l]
        mk_p, mv_p = memory_kv(mem_prompt, mem_norm[l], w_mem_kv[l])
        B = yp.shape[0]
        yp, h_p, rc_p, s_p, fc_p = trunk_layer(
            yp, mk_p, mv_p, jnp.zeros((B, RNN_WIDTH), dt), jnp.zeros((B, RNN_CONV_W - 1, RNN_WIDTH), dt),
            jnp.zeros((B, HG_HEADS, HG_DK, HG_DV), dt), jnp.zeros((B, FFN_CONV_W - 1, FFN_DIM), dt), p, lb)
        ys, h_s, rc_s, s_s, fc_s = trunk_layer(
            ys, cache_mem_k[l], cache_mem_v[l], state_rnn_h[l], state_rnn_conv[l], state_hg[l],
            state_ffn_conv[l], p, lb)
        mk_p_l.append(mk_p); mv_p_l.append(mv_p); hp_l.append(h_p); rcp_l.append(rc_p)
        sp_l.append(s_p); fcp_l.append(fc_p)
        hs_l.append(h_s); rcs_l.append(rc_s); ss_l.append(s_s); fcs_l.append(fc_s)
    mem_k_prompt = jnp.stack(mk_p_l)
    mem_v_prompt = jnp.stack(mv_p_l)
    rnn_h_prompt = jnp.stack(hp_l)
    rnn_conv_prompt = jnp.stack(rcp_l)
    hg_prompt = jnp.stack(sp_l)
    ffn_conv_prompt = jnp.stack(fcp_l)
    rnn_h_sample = jnp.stack(hs_l)
    rnn_conv_sample = jnp.stack(rcs_l)
    hg_sample = jnp.stack(ss_l)
    ffn_conv_sample = jnp.stack(fcs_l)
    return (yp, ys, mem_k_prompt, mem_v_prompt, rnn_h_prompt, rnn_conv_prompt, hg_prompt, ffn_conv_prompt,
            rnn_h_sample, rnn_conv_sample, hg_sample, ffn_conv_sample)
```

```python
import functools

import jax
import jax.numpy as jnp
from jax import lax
from jax.experimental import pallas as pl
from jax.experimental.pallas import tpu as pltpu

F32 = jnp.float32
BF16 = jnp.bfloat16

D_MODEL = 2048
RNN_WIDTH = 1024
RNN_BLOCKS = 16
RNN_CONV_W = 4
LRU_C = 8.0
HG_HEADS = 8
HG_DK = 128
HG_DV = 128
HG_CHUNK = 64
HG_SUB = 16
MEM_LEN = 256
XA_HEADS = 4
XA_HD = 256
BRANCH_WIDTH = 1024
N_BRANCH = 3
FFN_DIM = 5632
FFN_CONV_W = 3
EPS = 1e-6
IN_COLS = 6 * BRANCH_WIDTH + N_BRANCH * D_MODEL

COL_RNN, COL_HQ, COL_HF, COL_HI, COL_HO, COL_XQ = range(6)
GATE_COL0 = 3

HALO = 8
VMEM_LIMIT = 56 * 1024 * 1024


def _params(sem):
    return pltpu.CompilerParams(dimension_semantics=sem, vmem_limit_bytes=VMEM_LIMIT)


def _rms_scale(x):
    return lax.rsqrt(jnp.mean(x * x, axis=-1, keepdims=True) + EPS)


def _norm_matmul_kernel(x_ref, g_ref, w_ref, o_ref, xn_ref):
    @pl.when(pl.program_id(1) == 0)
    def _():
        x = x_ref[...]
        xn_ref[...] = (x * _rms_scale(x) * g_ref[...]).astype(BF16)

    o_ref[...] = jnp.dot(xn_ref[...], w_ref[...], preferred_element_type=F32)


def _norm_matmul(x, g, w, tm, tn, name):
    n, d = x.shape
    c = w.shape[1]
    return pl.pallas_call(
        _norm_matmul_kernel,
        grid=(n // tm, c // tn),
        in_specs=[
            pl.BlockSpec((tm, d), lambda i, j: (i, 0)),
            pl.BlockSpec((1, d), lambda i, j: (0, 0)),
            pl.BlockSpec((d, tn), lambda i, j: (0, j)),
        ],
        out_specs=pl.BlockSpec((tm, tn), lambda i, j: (i, j)),
        out_shape=jax.ShapeDtypeStruct((n, c), F32),
        scratch_shapes=[pltpu.VMEM((tm, d), BF16)],
        compiler_params=_params(("arbitrary", "arbitrary")),
        name=name,
    )(x, g.reshape(1, d), w)


def _rglru_kernel(z_ref, rc0_ref, h0_ref, cw_ref, cb_ref, wa_ref, ba_ref, wx_ref, bx_ref, lam_ref,
                  ya_ref, h1_ref, rc1_ref, xp_ref, a_ref, b_ref, hs_ref, hc_ref, *, nb, T):
    it = pl.program_id(1)
    nt = pl.num_programs(1)
    hist = RNN_CONV_W - 1

    @pl.when(it == 0)
    def _():
        xp_ref[:, HALO - hist:HALO, :] = rc0_ref[...]
        hc_ref[...] = h0_ref[...]

    xp_ref[:, HALO:HALO + T, :] = z_ref[...]
    cw = cw_ref[...]
    xr = cb_ref[...][None]
    for j in range(RNN_CONV_W):
        xr = xr + xp_ref[:, HALO - hist + j:HALO - hist + j + T, :] * cw[j:j + 1][None]
    xp_ref[:, HALO - hist:HALO, :] = xp_ref[:, HALO + T - hist:HALO + T, :]

    xr2 = xr.reshape(nb * T, RNN_WIDTH)
    xb = xr2.astype(BF16)
    r = jax.nn.sigmoid(jnp.dot(xb, wa_ref[...], preferred_element_type=F32) + ba_ref[...])
    ig = jax.nn.sigmoid(jnp.dot(xb, wx_ref[...], preferred_element_type=F32) + bx_ref[...])
    nl = -lam_ref[...]
    softplus = jnp.maximum(nl, 0.0) + jnp.log1p(jnp.exp(-jnp.abs(nl)))
    log_a = -LRU_C * r * softplus
    a = jnp.exp(log_a)
    a_ref[...] = a.reshape(nb, T, RNN_WIDTH)
    one_minus_a2 = -jnp.tanh(log_a) * (a * a + 1.0)
    b_ref[...] = (jnp.sqrt(one_minus_a2) * (ig * xr2)).reshape(nb, T, RNN_WIDTH)

    hs = [hc_ref[bi:bi + 1, :] for bi in range(nb)]
    for t in range(T):
        for bi in range(nb):
            hs[bi] = a_ref[bi, t:t + 1, :] * hs[bi] + b_ref[bi, t:t + 1, :]
            hs_ref[bi, t:t + 1, :] = hs[bi]
    for bi in range(nb):
        hc_ref[bi:bi + 1, :] = hs[bi]
    ya_ref[...] = hs_ref[...].astype(BF16)

    @pl.when(it == nt - 1)
    def _():
        h1_ref[...] = hc_ref[...]
        rc1_ref[...] = xp_ref[:, HALO - hist:HALO, :]


def _rglru(z3, rc0, h0, cw, cb, wa, ba, wx, bx, lam, nb, T):
    B, L, _ = z3.shape
    C = RNN_WIDTH
    hist = RNN_CONV_W - 1
    vec = lambda: pl.BlockSpec((1, C), lambda b, t: (0, 0))
    mat = lambda: pl.BlockSpec((C, C), lambda b, t: (0, 0))
    return pl.pallas_call(
        functools.partial(_rglru_kernel, nb=nb, T=T),
        grid=(B // nb, L // T),
        in_specs=[
            pl.BlockSpec((nb, T, C), lambda b, t: (b, t, COL_RNN)),
            pl.BlockSpec((nb, hist, C), lambda b, t: (b, 0, 0)),
            pl.BlockSpec((nb, C), lambda b, t: (b, 0)),
            pl.BlockSpec((RNN_CONV_W, C), lambda b, t: (0, 0)),
            vec(), mat(), vec(), mat(), vec(), vec(),
        ],
        out_specs=[
            pl.BlockSpec((nb, T, C), lambda b, t: (b, t, 0)),
            pl.BlockSpec((nb, C), lambda b, t: (b, 0)),
            pl.BlockSpec((nb, hist, C), lambda b, t: (b, 0, 0)),
        ],
        out_shape=[
            jax.ShapeDtypeStruct((B, L, C), BF16),
            jax.ShapeDtypeStruct((B, C), F32),
            jax.ShapeDtypeStruct((B, hist, C), F32),
        ],
        scratch_shapes=[
            pltpu.VMEM((nb, HALO + T, C), F32),
            pltpu.VMEM((nb, T, C), F32),
            pltpu.VMEM((nb, T, C), F32),
            pltpu.VMEM((nb, T, C), F32),
            pltpu.VMEM((nb, C), F32),
        ],
        compiler_params=_params(("arbitrary", "arbitrary")),
        name="rglru",
    )(z3, rc0, h0, cw, cb.reshape(1, C), wa, ba.reshape(1, C), wx, bx.reshape(1, C), lam.reshape(1, C))


def _cumsum_rows(x):
    n = x.shape[0]
    row = lax.broadcasted_iota(jnp.int32, x.shape, 0)
    s = 1
    while s < n:
        x = x + jnp.where(row >= s, pltpu.roll(x, s, 0), 0.0)
        s *= 2
    return x


def _dot_nt(a, b):
    return lax.dot_general(a, b, (((1,), (1,)), ((), ())), preferred_element_type=F32)


def _hgrn2_chunk_head(qh, fr, vh, ogh, lbh, gnh, st, cs):
    f = lbh + (1.0 - lbh) * jax.nn.sigmoid(fr)
    k = 1.0 - f
    bc = _cumsum_rows(jnp.log(f))
    vb = vh.astype(BF16)
    o = _dot_nt((qh * jnp.exp(bc)).astype(BF16), st.astype(BF16))

    nblk = cs // HG_SUB
    row = lax.broadcasted_iota(jnp.int32, (cs, HG_DK), 0)
    tloc = lax.broadcasted_iota(jnp.int32, (HG_SUB, HG_DK), 0)
    lane = lax.broadcasted_iota(jnp.int32, (HG_SUB, HG_DK), 1)
    ones = jnp.ones((HG_DK, HG_DK), BF16)
    a_blocks = []
    for blk in range(nblk):
        r0 = blk * HG_SUB
        bcb = bc[r0:r0 + HG_SUB]
        qb = qh[r0:r0 + HG_SUB]
        kb = k[r0:r0 + HG_SUB]
        pair = []
        for s in range(HG_SUB):
            decay = jnp.where(tloc >= s, jnp.exp(jnp.minimum(bcb - bcb[s:s + 1], 0.0)), 0.0)
            pair.append((decay * (qb * kb[s:s + 1])).astype(BF16))
        lane_sum = jnp.dot(jnp.concatenate(pair, axis=0), ones, preferred_element_type=F32)
        a_blk = jnp.zeros((HG_SUB, HG_DK), F32)
        for s in range(HG_SUB):
            a_blk = jnp.where(lane == r0 + s, lane_sum[s * HG_SUB:(s + 1) * HG_SUB], a_blk)
        a_blk = a_blk[:, :cs]
        if blk > 0:
            br = bc[r0 - 1:r0]
            qt = (qb * jnp.exp(bcb - br)).astype(BF16)
            kt = jnp.where(row < r0, k * jnp.exp(jnp.minimum(br - bc, 0.0)), 0.0).astype(BF16)
            a_blk = a_blk + _dot_nt(qt, kt)
        a_blocks.append(a_blk)
    a = jnp.concatenate(a_blocks, axis=0)
    o = o + jnp.dot(a.astype(BF16), vb, preferred_element_type=F32)

    bl = bc[cs - 1:cs]
    kd = (k * jnp.exp(bl - bc)).astype(BF16)
    st_new = st * jnp.exp(bl) + jnp.dot(vh.T.astype(BF16), kd, preferred_element_type=F32)
    y = o * _rms_scale(o) * gnh * jax.nn.sigmoid(ogh)
    return y, st_new


def _hgrn2_kernel(q_ref, f_ref, v_ref, og_ref, s0_ref, lbp_ref, gn_ref, yb_ref, s1_ref,
                  st_ref, lb_ref, *, nb, T, cs, layer):
    it = pl.program_id(1)
    nt = pl.num_programs(1)

    @pl.when(it == 0)
    def _():
        p = lbp_ref[...]
        e = jnp.exp(p - jnp.max(p, axis=0, keepdims=True))
        sm = e / jnp.sum(e, axis=0, keepdims=True)
        lb_ref[...] = jnp.sum(sm[:layer + 1], axis=0, keepdims=True)

        def init(b, c):
            for h in range(HG_HEADS):
                st_ref[b, h] = s0_ref[b, h].T
            return c
        lax.fori_loop(0, nb, init, 0)

    nch = T // cs

    def chunk(idx, c):
        b = idx // nch
        rows = pl.ds(pl.multiple_of((idx % nch) * cs, cs), cs)
        for h in range(HG_HEADS):
            hl = slice(h * HG_DK, (h + 1) * HG_DK)
            y, st_new = _hgrn2_chunk_head(
                q_ref[b, rows, hl], f_ref[b, rows, hl], v_ref[b, rows, hl], og_ref[b, rows, hl],
                lb_ref[:, hl], gn_ref[...], st_ref[b, h], cs)
            yb_ref[b, rows, hl] = y.astype(BF16)
            st_ref[b, h] = st_new
        return c
    lax.fori_loop(0, nb * nch, chunk, 0)

    @pl.when(it == nt - 1)
    def _():
        def fin(b, c):
            for h in range(HG_HEADS):
                s1_ref[b, h] = st_ref[b, h].T
            return c
        lax.fori_loop(0, nb, fin, 0)


def _hgrn2(z3, s0, hg_lb, gn, layer, nb, T):
    B, L, _ = z3.shape
    C = HG_HEADS * HG_DK
    cs = min(HG_CHUNK, L)
    zcol = lambda col: pl.BlockSpec((nb, T, C), lambda b, t: (b, t, col))
    st_spec = lambda: pl.BlockSpec((nb, HG_HEADS, HG_DK, HG_DV), lambda b, t: (b, 0, 0, 0))
    nl = hg_lb.shape[0]
    return pl.pallas_call(
        functools.partial(_hgrn2_kernel, nb=nb, T=T, cs=cs, layer=layer),
        grid=(B // nb, L // T),
        in_specs=[
            zcol(COL_HQ), zcol(COL_HF), zcol(COL_HI), zcol(COL_HO),
            st_spec(),
            pl.BlockSpec((nl, C), lambda b, t: (0, 0)),
            pl.BlockSpec((1, HG_DV), lambda b, t: (0, 0)),
        ],
        out_specs=[pl.BlockSpec((nb, T, C), lambda b, t: (b, t, 0)), st_spec()],
        out_shape=[
            jax.ShapeDtypeStruct((B, L, C), BF16),
            jax.ShapeDtypeStruct((B, HG_HEADS, HG_DK, HG_DV), F32),
        ],
        scratch_shapes=[
            pltpu.VMEM((nb, HG_HEADS, HG_DV, HG_DK), F32),
            pltpu.VMEM((1, C), F32),
        ],
        compiler_params=_params(("arbitrary", "arbitrary")),
        name="hgrn2",
    )(z3, z3, z3, z3, s0, hg_lb, gn.reshape(1, HG_DV))


def _xattn_kernel(q_ref, k_ref, v_ref, o_ref):
    for h in range(XA_HEADS):
        hl = slice(h * XA_HD, (h + 1) * XA_HD)
        s = _dot_nt(q_ref[0, :, hl].astype(BF16), k_ref[0, :, hl].astype(BF16)) * (XA_HD ** -0.5)
        e = jnp.exp(s - jnp.max(s, axis=-1, keepdims=True))
        p = e / jnp.sum(e, axis=-1, keepdims=True)
        o = jnp.dot(p.astype(BF16), v_ref[0, :, hl].astype(BF16), preferred_element_type=F32)
        o_ref[0, :, hl] = o.astype(BF16)


def _xattn(z3, mk, mv, T):
    B, L, _ = z3.shape
    C = XA_HEADS * XA_HD
    mem = lambda: pl.BlockSpec((1, MEM_LEN, C), lambda b, t: (b, 0, 0))
    return pl.pallas_call(
        _xattn_kernel,
        grid=(B, L // T),
        in_specs=[pl.BlockSpec((1, T, C), lambda b, t: (b, t, COL_XQ)), mem(), mem()],
        out_specs=pl.BlockSpec((1, T, C), lambda b, t: (b, t, 0)),
        out_shape=jax.ShapeDtypeStruct((B, L, C), BF16),
        compiler_params=_params(("arbitrary", "arbitrary")),
        name="xattn",
    )(z3, mk, mv)


def _merge_kernel(x_ref, ya_ref, yb_ref, yc_ref, g0_ref, g1_ref, g2_ref, bg_ref, wb_ref, wo_ref,
                  pg_ref, o_ref):
    m = jnp.zeros(o_ref.shape, F32)
    for nb, (y_ref, g_ref) in enumerate(((ya_ref, g0_ref), (yb_ref, g1_ref), (yc_ref, g2_ref))):
        gate = jax.nn.sigmoid(g_ref[...] + bg_ref[nb:nb + 1, :])
        m = m + gate * jnp.dot(y_ref[...], wb_ref[nb], preferred_element_type=F32)
    y = jnp.dot(m.astype(BF16), wo_ref[...], preferred_element_type=F32)
    o_ref[...] = x_ref[...] + y * _rms_scale(y) * pg_ref[...]


def _merge(x, ya, yb, yc, z, b_gate, wb, wo, pg, tm):
    n, d = x.shape
    c = BRANCH_WIDTH
    once = pl.Buffered(1)
    branch = lambda: pl.BlockSpec((tm, c), lambda i: (i, 0))
    gate = lambda nb: pl.BlockSpec((tm, d), lambda i: (i, GATE_COL0 + nb))
    return pl.pallas_call(
        _merge_kernel,
        grid=(n // tm,),
        in_specs=[
            pl.BlockSpec((tm, d), lambda i: (i, 0)),
            branch(), branch(), branch(),
            gate(0), gate(1), gate(2),
            pl.BlockSpec((N_BRANCH, d), lambda i: (0, 0)),
            pl.BlockSpec((N_BRANCH, c, d), lambda i: (0, 0, 0), pipeline_mode=once),
            pl.BlockSpec((d, d), lambda i: (0, 0), pipeline_mode=once),
            pl.BlockSpec((1, d), lambda i: (0, 0)),
        ],
        out_specs=pl.BlockSpec((tm, d), lambda i: (i, 0)),
        out_shape=jax.ShapeDtypeStruct((n, d), F32),
        compiler_params=_params(("arbitrary",)),
        name="merge",
    )(x, ya, yb, yc, z, z, z, b_gate, wb, wo, pg.reshape(1, d))


def _ffn_kernel(x_ref, g_ref, wu_ref, wv_ref, cw_ref, cb_ref, fc0_ref, wd_ref, pg_ref,
                o_ref, fc1_ref, xn_ref, acc_ref, hal_ref, car_ref, *, nb, T, tf):
    it = pl.program_id(1)
    j = pl.program_id(2)
    nj = pl.num_programs(2)
    hist = FFN_CONV_W - 1

    @pl.when(j == 0)
    def _():
        x = x_ref[...].reshape(nb * T, D_MODEL)
        xn_ref[...] = (x * _rms_scale(x) * g_ref[...]).astype(BF16)
        acc_ref[...] = jnp.zeros_like(acc_ref)

    @pl.when(it == 0)
    def _():
        car_ref[j] = fc0_ref[...]

    xn = xn_ref[...]
    u = jnp.dot(xn, wu_ref[...], preferred_element_type=F32)
    v = jnp.dot(xn, wv_ref[...], preferred_element_type=F32)
    hal_ref[:, HALO - hist:HALO, :] = car_ref[j]
    hal_ref[:, HALO:HALO + T, :] = u.reshape(nb, T, tf)
    cw = cw_ref[...]
    uc = cb_ref[...][None]
    for jj in range(FFN_CONV_W):
        uc = uc + hal_ref[:, HALO - hist + jj:HALO - hist + jj + T, :] * cw[jj:jj + 1][None]
    tail = hal_ref[:, HALO + T - hist:HALO + T, :]
    car_ref[j] = tail
    act = (jax.nn.gelu(uc).reshape(nb * T, tf) * v).astype(BF16)
    acc_ref[...] += jnp.dot(act, wd_ref[...], preferred_element_type=F32)

    @pl.when(j == nj - 1)
    def _():
        y = acc_ref[...]
        out = x_ref[...].reshape(nb * T, D_MODEL) + y * _rms_scale(y) * pg_ref[...]
        o_ref[...] = out.reshape(nb, T, D_MODEL)

    @pl.when((j == nj - 1) & (it == pl.num_programs(1) - 1))
    def _():
        for jj in range(FFN_DIM // tf):
            fc1_ref[:, :, jj * tf:(jj + 1) * tf] = car_ref[jj]


def _ffn(x3, g, w_up, cw, cb, fc0, w_down, pg, nb, T, tf):
    B, L, d = x3.shape
    F = FFN_DIM
    nj = F // tf
    hist = FFN_CONV_W - 1
    return pl.pallas_call(
        functools.partial(_ffn_kernel, nb=nb, T=T, tf=tf),
        grid=(B // nb, L // T, nj),
        in_specs=[
            pl.BlockSpec((nb, T, d), lambda b, t, j: (b, t, 0)),
            pl.BlockSpec((1, d), lambda b, t, j: (0, 0)),
            pl.BlockSpec((d, tf), lambda b, t, j: (0, j)),
            pl.BlockSpec((d, tf), lambda b, t, j: (0, nj + j)),
            pl.BlockSpec((FFN_CONV_W, tf), lambda b, t, j: (0, j)),
            pl.BlockSpec((1, tf), lambda b, t, j: (0, j)),
            pl.BlockSpec((nb, hist, tf), lambda b, t, j: (b, 0, j)),
            pl.BlockSpec((tf, d), lambda b, t, j: (j, 0)),
            pl.BlockSpec((1, d), lambda b, t, j: (0, 0)),
        ],
        out_specs=[
            pl.BlockSpec((nb, T, d), lambda b, t, j: (b, t, 0)),
            pl.BlockSpec((nb, hist, F), lambda b, t, j: (b, 0, 0)),
        ],
        out_shape=[
            jax.ShapeDtypeStruct((B, L, d), F32),
            jax.ShapeDtypeStruct((B, hist, F), F32),
        ],
        scratch_shapes=[
            pltpu.VMEM((nb * T, d), BF16),
            pltpu.VMEM((nb * T, d), F32),
            pltpu.VMEM((nb, HALO + T, tf), F32),
            pltpu.VMEM((nj, nb, hist, tf), F32),
        ],
        compiler_params=_params(("arbitrary", "arbitrary", "arbitrary")),
        name="ffn",
    )(x3, g.reshape(1, d), w_up, w_up, cw, cb.reshape(1, F), fc0, w_down, pg.reshape(1, d))


def _block_diag(w):
    nblk, bi, bj = w.shape
    eye = jnp.eye(nblk, dtype=w.dtype)
    return jnp.einsum("hij,hg->higj", w, eye).reshape(nblk * bi, nblk * bj)


def _tiles(B, L):
    T = min(L, 256)
    return dict(
        proj_tm=min(B * L, 1024), proj_tn=1024,
        rnn_nb=B if B * T <= 512 else 512 // T, rnn_T=T,
        hg_nb=min(B, 4), hg_T=T,
        xa_T=min(L, 512),
        merge_tm=256,
        ffn_nb=B if B * T <= 512 else 512 // T, ffn_T=T, ffn_tf=512,
    )


def _trunk_layer(x, mk, mv, h0, rc0, s0, fc0, w, layer):
    B, L, d = x.shape
    n = B * L
    t = _tiles(B, L)
    x2 = x.reshape(n, d)
    z = _norm_matmul(x2, w["pre_mix_norm"], w["w_in"], t["proj_tm"], t["proj_tn"], "in_proj")
    z3 = z.reshape(B, L, IN_COLS)
    ya, h1, rc1 = _rglru(z3, rc0, h0, w["rnn_conv_w"], w["rnn_conv_b"], w["lru_wa"], w["lru_ba"],
                         w["lru_wx"], w["lru_bx"], w["lru_lambda"], t["rnn_nb"], t["rnn_T"])
    yb, s1 = _hgrn2(z3, s0, w["hg_lb"], w["hg_norm"], layer, t["hg_nb"], t["hg_T"])
    yc = _xattn(z3, mk, mv, t["xa_T"])
    c = BRANCH_WIDTH
    x1 = _merge(x2, ya.reshape(n, c), yb.reshape(n, c), yc.reshape(n, c), z, w["b_gate"],
                w["w_branch"], w["w_out"], w["post_mix_norm"], t["merge_tm"])
    xo, fc1 = _ffn(x1.reshape(B, L, d), w["pre_ffn_norm"], w["w_ffn_up"], w["ffn_conv_w"],
                   w["ffn_conv_b"], fc0, w["w_ffn_down"], w["post_ffn_norm"],
                   t["ffn_nb"], t["ffn_T"], t["ffn_tf"])
    return xo, h1, rc1, s1, fc1


def kernel(x_prompt, x_sample, cache_mem_k, cache_mem_v, state_rnn_h, state_rnn_conv, state_hg,
           state_ffn_conv, mem_prompt, pre_mix_norm, w_in, rnn_conv_w, rnn_conv_b, lru_wa, lru_ba,
           lru_wx, lru_bx, lru_lambda, hg_lb, hg_norm, mem_norm, w_mem_kv, w_branch, b_gate, w_out,
           post_mix_norm, pre_ffn_norm, w_ffn_up, ffn_conv_w, ffn_conv_b, w_ffn_down, post_ffn_norm):
    depth = w_in.shape[0]
    Bp = x_prompt.shape[0]
    Bs = x_sample.shape[0]
    xa_w = XA_HEADS * XA_HD
    yp, ys = x_prompt, x_sample
    outs = [[] for _ in range(10)]
    for l in range(depth):
        w = {
            "pre_mix_norm": pre_mix_norm[l], "w_in": w_in[l].astype(BF16),
            "rnn_conv_w": rnn_conv_w[l], "rnn_conv_b": rnn_conv_b[l],
            "lru_wa": _block_diag(lru_wa[l]).astype(BF16), "lru_ba": lru_ba[l],
            "lru_wx": _block_diag(lru_wx[l]).astype(BF16), "lru_bx": lru_bx[l],
            "lru_lambda": lru_lambda[l], "hg_lb": hg_lb, "hg_norm": hg_norm[l],
            "w_branch": w_branch[l].astype(BF16), "b_gate": b_gate[l],
            "w_out": w_out[l].astype(BF16), "post_mix_norm": post_mix_norm[l],
            "pre_ffn_norm": pre_ffn_norm[l], "w_ffn_up": w_ffn_up[l].astype(BF16),
            "ffn_conv_w": ffn_conv_w[l], "ffn_conv_b": ffn_conv_b[l],
            "w_ffn_down": w_ffn_down[l].astype(BF16), "post_ffn_norm": post_ffn_norm[l],
        }
        mem2 = mem_prompt.reshape(Bp * MEM_LEN, D_MODEL)
        kv = _norm_matmul(mem2, mem_norm[l], w_mem_kv[l].astype(BF16), Bp * MEM_LEN, 1024, "mem_kv")
        kv = kv.reshape(Bp, MEM_LEN, 2 * xa_w)
        mk_p, mv_p = kv[..., :xa_w], kv[..., xa_w:]
        yp, h_p, rc_p, s_p, fc_p = _trunk_layer(
            yp, mk_p, mv_p, jnp.zeros((Bp, RNN_WIDTH), F32),
            jnp.zeros((Bp, RNN_CONV_W - 1, RNN_WIDTH), F32),
            jnp.zeros((Bp, HG_HEADS, HG_DK, HG_DV), F32),
            jnp.zeros((Bp, FFN_CONV_W - 1, FFN_DIM), F32), w, l)
        ys, h_s, rc_s, s_s, fc_s = _trunk_layer(
            ys, cache_mem_k[l].reshape(Bs, MEM_LEN, xa_w), cache_mem_v[l].reshape(Bs, MEM_LEN, xa_w),
            state_rnn_h[l], state_rnn_conv[l], state_hg[l], state_ffn_conv[l], w, l)
        layer_out = (mk_p.reshape(Bp, MEM_LEN, XA_HEADS, XA_HD), mv_p.reshape(Bp, MEM_LEN, XA_HEADS, XA_HD),
                     h_p, rc_p, s_p, fc_p, h_s, rc_s, s_s, fc_s)
        for acc, val in zip(outs, layer_out):
            acc.append(val)
    return (yp, ys) + tuple(jnp.stack(o) for o in outs)
```

```python
import functools

import jax
import jax.numpy as jnp
from jax import lax
from jax.experimental import pallas as pl
from jax.experimental.pallas import tpu as pltpu

F32 = jnp.float32
BF16 = jnp.bfloat16

D_MODEL = 2048
RNN_WIDTH = 1024
RNN_BLOCKS = 16
RNN_CONV_W = 4
LRU_C = 8.0
HG_HEADS = 8
HG_DK = 128
HG_DV = 128
HG_CHUNK = 64
HG_SUB = 16
HG_SKEW = 3
MEM_LEN = 256
XA_HEADS = 4
XA_HD = 256
BRANCH_WIDTH = 1024
N_BRANCH = 3
FFN_DIM = 5632
FFN_CONV_W = 3
EPS = 1e-6
LOG2E = 1.4426950408889634
NEG_BIG = -1e30
IN_COLS = 6 * BRANCH_WIDTH + N_BRANCH * D_MODEL

COL_RNN, COL_HQ, COL_HF, COL_HI, COL_HO, COL_XQ = range(6)
GATE_COL0 = 3

HALO = 8
VMEM_LIMIT = 56 * 1024 * 1024


def _params(sem):
    return pltpu.CompilerParams(dimension_semantics=sem, vmem_limit_bytes=VMEM_LIMIT)


def _rms_scale(x):
    return lax.rsqrt(jnp.mean(x * x, axis=-1, keepdims=True) + EPS)


def _norm_matmul_kernel(x_ref, g_ref, w_ref, o_ref, xn_ref):
    @pl.when(pl.program_id(1) == 0)
    def _():
        x = x_ref[...]
        xn_ref[...] = (x * _rms_scale(x) * g_ref[...]).astype(BF16)

    o_ref[...] = jnp.dot(xn_ref[...], w_ref[...], preferred_element_type=F32)


def _norm_matmul(x, g, w, tm, tn, name):
    n, d = x.shape
    c = w.shape[1]
    return pl.pallas_call(
        _norm_matmul_kernel,
        grid=(n // tm, c // tn),
        in_specs=[
            pl.BlockSpec((tm, d), lambda i, j: (i, 0)),
            pl.BlockSpec((1, d), lambda i, j: (0, 0)),
            pl.BlockSpec((d, tn), lambda i, j: (0, j)),
        ],
        out_specs=pl.BlockSpec((tm, tn), lambda i, j: (i, j)),
        out_shape=jax.ShapeDtypeStruct((n, c), F32),
        scratch_shapes=[pltpu.VMEM((tm, d), BF16)],
        compiler_params=_params(("arbitrary", "arbitrary")),
        name=name,
    )(x, g.reshape(1, d), w)


def _rglru_kernel(z_ref, rc0_ref, h0_ref, cw_ref, cb_ref, wa_ref, ba_ref, wx_ref, bx_ref, lam_ref,
                  ya_ref, h1_ref, rc1_ref, xp_ref, a_ref, b_ref, hs_ref, hc_ref, *, nb, T):
    it = pl.program_id(1)
    nt = pl.num_programs(1)
    hist = RNN_CONV_W - 1

    @pl.when(it == 0)
    def _():
        xp_ref[:, HALO - hist:HALO, :] = rc0_ref[...]
        hc_ref[...] = h0_ref[...]

    xp_ref[:, HALO:HALO + T, :] = z_ref[...]
    cw = cw_ref[...]
    xr = cb_ref[...][None]
    for j in range(RNN_CONV_W):
        xr = xr + xp_ref[:, HALO - hist + j:HALO - hist + j + T, :] * cw[j:j + 1][None]
    xp_ref[:, HALO - hist:HALO, :] = xp_ref[:, HALO + T - hist:HALO + T, :]

    xr2 = xr.reshape(nb * T, RNN_WIDTH)
    xb = xr2.astype(BF16)
    r = jax.nn.sigmoid(jnp.dot(xb, wa_ref[...], preferred_element_type=F32) + ba_ref[...])
    ig = jax.nn.sigmoid(jnp.dot(xb, wx_ref[...], preferred_element_type=F32) + bx_ref[...])
    nl = -lam_ref[...]
    softplus = jnp.maximum(nl, 0.0) + jnp.log1p(jnp.exp(-jnp.abs(nl)))
    log_a = -LRU_C * r * softplus
    a = jnp.exp(log_a)
    a_ref[...] = a.reshape(nb, T, RNN_WIDTH)
    one_minus_a2 = -jnp.tanh(log_a) * (a * a + 1.0)
    b_ref[...] = (jnp.sqrt(one_minus_a2) * (ig * xr2)).reshape(nb, T, RNN_WIDTH)

    hs = [hc_ref[bi:bi + 1, :] for bi in range(nb)]
    for t in range(T):
        for bi in range(nb):
            hs[bi] = a_ref[bi, t:t + 1, :] * hs[bi] + b_ref[bi, t:t + 1, :]
            hs_ref[bi, t:t + 1, :] = hs[bi]
    for bi in range(nb):
        hc_ref[bi:bi + 1, :] = hs[bi]
    ya_ref[...] = hs_ref[...].astype(BF16)

    @pl.when(it == nt - 1)
    def _():
        h1_ref[...] = hc_ref[...]
        rc1_ref[...] = xp_ref[:, HALO - hist:HALO, :]


def _rglru(z3, rc0, h0, cw, cb, wa, ba, wx, bx, lam, nb, T):
    B, L, _ = z3.shape
    C = RNN_WIDTH
    hist = RNN_CONV_W - 1
    vec = lambda: pl.BlockSpec((1, C), lambda b, t: (0, 0))
    mat = lambda: pl.BlockSpec((C, C), lambda b, t: (0, 0))
    return pl.pallas_call(
        functools.partial(_rglru_kernel, nb=nb, T=T),
        grid=(B // nb, L // T),
        in_specs=[
            pl.BlockSpec((nb, T, C), lambda b, t: (b, t, COL_RNN)),
            pl.BlockSpec((nb, hist, C), lambda b, t: (b, 0, 0)),
            pl.BlockSpec((nb, C), lambda b, t: (b, 0)),
            pl.BlockSpec((RNN_CONV_W, C), lambda b, t: (0, 0)),
            vec(), mat(), vec(), mat(), vec(), vec(),
        ],
        out_specs=[
            pl.BlockSpec((nb, T, C), lambda b, t: (b, t, 0)),
            pl.BlockSpec((nb, C), lambda b, t: (b, 0)),
            pl.BlockSpec((nb, hist, C), lambda b, t: (b, 0, 0)),
        ],
        out_shape=[
            jax.ShapeDtypeStruct((B, L, C), BF16),
            jax.ShapeDtypeStruct((B, C), F32),
            jax.ShapeDtypeStruct((B, hist, C), F32),
        ],
        scratch_shapes=[
            pltpu.VMEM((nb, HALO + T, C), F32),
            pltpu.VMEM((nb, T, C), F32),
            pltpu.VMEM((nb, T, C), F32),
            pltpu.VMEM((nb, T, C), F32),
            pltpu.VMEM((nb, C), F32),
        ],
        compiler_params=_params(("arbitrary", "arbitrary")),
        name="rglru",
    )(z3, rc0, h0, cw, cb.reshape(1, C), wa, ba.reshape(1, C), wx, bx.reshape(1, C), lam.reshape(1, C))


def _dot_nt(a, b):
    return lax.dot_general(a, b, (((1,), (1,)), ((), ())), preferred_element_type=F32)


def _split3(x):
    hi = x.astype(BF16)
    r = x - hi.astype(F32)
    mid = r.astype(BF16)
    lo = (r - mid.astype(F32)).astype(BF16)
    return jnp.concatenate([hi, mid, lo], axis=0)


def _hgrn2_scores(qh, bc, ck, ck_row, st, cs):
    o = _dot_nt((qh * jnp.exp2(bc)).astype(BF16), st.astype(BF16))

    nblk = cs // HG_SUB
    a_rows = [None] * nblk
    half = cs // 2
    while half >= HG_SUB:
        npair = cs // (2 * half)
        q_parts, k_parts = [], []
        for p in range(npair):
            lo, mid, hi = 2 * half * p, 2 * half * p + half, 2 * half * (p + 1)
            br = bc[mid - 1:mid]
            q_parts.append(qh[mid:hi] * jnp.exp2(bc[mid:hi] - br))
            k_parts += [jnp.exp2(br - ck[lo:mid]), jnp.zeros((half, HG_DK), F32)]
        off = _dot_nt(jnp.concatenate(q_parts, axis=0).astype(BF16),
                      jnp.concatenate(k_parts, axis=0).astype(BF16))
        if npair > 1:
            lg = half.bit_length() - 1
            rowp = lax.shift_right_logical(lax.broadcasted_iota(jnp.int32, off.shape, 0), lg)
            colp = lax.shift_right_logical(lax.broadcasted_iota(jnp.int32, off.shape, 1), lg + 1)
            off = jnp.where(rowp == colp, off, 0.0)
        for p in range(npair):
            for sub in range(half // HG_SUB):
                blk = (2 * half * p + half) // HG_SUB + sub
                piece = off[p * half + sub * HG_SUB:p * half + (sub + 1) * HG_SUB]
                a_rows[blk] = piece if a_rows[blk] is None else a_rows[blk] + piece
        half //= 2

    hs = HG_SUB // 2
    ones = jnp.ones((HG_DK, HG_DK), BF16)
    zero = jnp.zeros((hs, HG_DK), F32)
    lane_sums = []
    for blk in range(nblk):
        r0 = blk * HG_SUB
        b_lo, b_hi = bc[r0:r0 + hs], bc[r0 + hs:r0 + HG_SUB]
        q_lo, q_hi = qh[r0:r0 + hs], qh[r0 + hs:r0 + HG_SUB]
        pair = []
        for s in range(HG_SUB):
            cks = ck_row(r0 + s)
            d_lo = q_lo * jnp.exp2(b_lo - cks) if s < hs else zero
            d_hi = q_hi * jnp.exp2(b_hi - cks)
            pair.append(jnp.concatenate([d_lo, d_hi], axis=0).astype(BF16))
        lane_sums.append(jnp.dot(jnp.concatenate(pair, axis=0), ones, preferred_element_type=F32))
    return o, a_rows, lane_sums


def _hgrn2_output(scores, vh, ogh, gnh, bl, ck, st, cs):
    o, a_rows, lane_sums = scores
    hs = HG_SUB // 2
    lane = lax.broadcasted_iota(jnp.int32, (hs, HG_DK), 1)
    tloc = lax.broadcasted_iota(jnp.int32, (hs, HG_DK), 0)
    lo_mask = [(lane == s) & (tloc >= s) for s in range(hs)]
    hi_mask = [lane == s for s in range(hs)] + [(lane == s) & (tloc >= s - hs) for s in range(hs, HG_SUB)]
    zero = jnp.zeros((hs, HG_DK), F32)
    a_rows = list(a_rows)
    for blk, lane_sum in enumerate(lane_sums):
        a_lo, a_hi = zero, zero
        for s in range(HG_SUB):
            if s < hs:
                a_lo = jnp.where(lo_mask[s], lane_sum[s * HG_SUB:s * HG_SUB + hs], a_lo)
            a_hi = jnp.where(hi_mask[s], lane_sum[s * HG_SUB + hs:(s + 1) * HG_SUB], a_hi)
        a_d = jnp.concatenate([a_lo, a_hi], axis=0)
        if blk:
            a_d = pltpu.roll(a_d, blk * HG_SUB, 1)
        a_d = a_d[:, :cs]
        a_rows[blk] = a_d if a_rows[blk] is None else a_rows[blk] + a_d
    a = jnp.concatenate(a_rows, axis=0)
    o = o + jnp.dot(a.astype(BF16), vh.astype(BF16), preferred_element_type=F32)
    kd = jnp.exp2(bl - ck).astype(BF16)
    st_new = st * jnp.exp2(bl) + jnp.dot(vh.T.astype(BF16), kd, preferred_element_type=F32)
    y = o * _rms_scale(o) * gnh * jax.nn.sigmoid(ogh)
    return y, st_new


def _hgrn2_kernel(q_ref, f_ref, v_ref, og_ref, s0_ref, lbp_ref, gn_ref, yb_ref, s1_ref,
                  lb_ref, bc_ref, ck_ref, *st_refs, nb, T, cs, layer):
    it = pl.program_id(1)
    nt = pl.num_programs(1)
    row = lax.broadcasted_iota(jnp.int32, (cs, 3 * cs), 0)
    col = lax.broadcasted_iota(jnp.int32, (cs, 3 * cs), 1)
    tri3 = (row >= (col & (cs - 1))).astype(BF16)

    @pl.when(it == 0)
    def _():
        p = lbp_ref[...]
        e = jnp.exp(p - jnp.max(p, axis=0, keepdims=True))
        sm = e / jnp.sum(e, axis=0, keepdims=True)
        lb_ref[...] = jnp.sum(sm[:layer + 1], axis=0, keepdims=True)

        def init(b, c):
            for h in range(HG_HEADS):
                st_refs[h][b] = s0_ref[b, h].T
            return c
        lax.fori_loop(0, nb, init, 0)

    nch = T // cs

    def chunk(idx, c):
        b = idx // nch
        rows = pl.ds(pl.multiple_of((idx % nch) * cs, cs), cs)
        lb = lb_ref[...]
        f = lb + (1.0 - lb) * jax.nn.sigmoid(f_ref[b, rows, :])
        bc = jnp.dot(tri3, _split3(jnp.log(f) * LOG2E), preferred_element_type=F32)
        bc_ref[...] = bc
        ck_ref[...] = bc - jnp.log(1.0 - f) * LOG2E

        def scores(h):
            hl = slice(h * HG_DK, (h + 1) * HG_DK)
            return _hgrn2_scores(q_ref[b, rows, hl], bc_ref[:, hl], ck_ref[:, hl],
                                 lambda r: ck_ref[r:r + 1, hl], st_refs[h][b], cs)

        def output(h, sc):
            hl = slice(h * HG_DK, (h + 1) * HG_DK)
            y, st_new = _hgrn2_output(sc, v_ref[b, rows, hl], og_ref[b, rows, hl], gn_ref[...],
                                      bc_ref[cs - 1:cs, hl], ck_ref[:, hl], st_refs[h][b], cs)
            yb_ref[b, rows, hl] = y.astype(BF16)
            st_refs[h][b] = st_new

        pending = {}
        for h in range(HG_HEADS + HG_SKEW):
            if h < HG_HEADS:
                pending[h] = scores(h)
            if h >= HG_SKEW:
                output(h - HG_SKEW, pending.pop(h - HG_SKEW))
        return c
    lax.fori_loop(0, nb * nch, chunk, 0)

    @pl.when(it == nt - 1)
    def _():
        def fin(b, c):
            for h in range(HG_HEADS):
                s1_ref[b, h] = st_refs[h][b].T
            return c
        lax.fori_loop(0, nb, fin, 0)


def _hgrn2(z3, s0, hg_lb, gn, layer, nb, T):
    B, L, _ = z3.shape
    C = HG_HEADS * HG_DK
    cs = min(HG_CHUNK, L)
    zcol = lambda col: pl.BlockSpec((nb, T, C), lambda b, t: (b, t, col))
    st_spec = lambda: pl.BlockSpec((nb, HG_HEADS, HG_DK, HG_DV), lambda b, t: (b, 0, 0, 0))
    nl = hg_lb.shape[0]
    return pl.pallas_call(
        functools.partial(_hgrn2_kernel, nb=nb, T=T, cs=cs, layer=layer),
        grid=(B // nb, L // T),
        in_specs=[
            zcol(COL_HQ), zcol(COL_HF), zcol(COL_HI), zcol(COL_HO),
            st_spec(),
            pl.BlockSpec((nl, C), lambda b, t: (0, 0)),
            pl.BlockSpec((1, HG_DV), lambda b, t: (0, 0)),
        ],
        out_specs=[pl.BlockSpec((nb, T, C), lambda b, t: (b, t, 0)), st_spec()],
        out_shape=[
            jax.ShapeDtypeStruct((B, L, C), BF16),
            jax.ShapeDtypeStruct((B, HG_HEADS, HG_DK, HG_DV), F32),
        ],
        scratch_shapes=[
            pltpu.VMEM((1, C), F32),
            pltpu.VMEM((cs, C), F32),
            pltpu.VMEM((cs, C), F32),
        ] + [pltpu.VMEM((nb, HG_DV, HG_DK), F32) for _ in range(HG_HEADS)],
        compiler_params=_params(("arbitrary", "arbitrary")),
        name="hgrn2",
    )(z3, z3, z3, z3, s0, hg_lb, gn.reshape(1, HG_DV))


def _xattn_kernel(q_ref, k_ref, v_ref, o_ref):
    for h in range(XA_HEADS):
        hl = slice(h * XA_HD, (h + 1) * XA_HD)
        s = _dot_nt(q_ref[0, :, hl].astype(BF16), k_ref[0, :, hl].astype(BF16)) * (XA_HD ** -0.5)
        e = jnp.exp(s - jnp.max(s, axis=-1, keepdims=True))
        p = e / jnp.sum(e, axis=-1, keepdims=True)
        o = jnp.dot(p.astype(BF16), v_ref[0, :, hl].astype(BF16), preferred_element_type=F32)
        o_ref[0, :, hl] = o.astype(BF16)


def _xattn(z3, mk, mv, T):
    B, L, _ = z3.shape
    C = XA_HEADS * XA_HD
    mem = lambda: pl.BlockSpec((1, MEM_LEN, C), lambda b, t: (b, 0, 0))
    return pl.pallas_call(
        _xattn_kernel,
        grid=(B, L // T),
        in_specs=[pl.BlockSpec((1, T, C), lambda b, t: (b, t, COL_XQ)), mem(), mem()],
        out_specs=pl.BlockSpec((1, T, C), lambda b, t: (b, t, 0)),
        out_shape=jax.ShapeDtypeStruct((B, L, C), BF16),
        compiler_params=_params(("arbitrary", "arbitrary")),
        name="xattn",
    )(z3, mk, mv)


def _merge_kernel(x_ref, ya_ref, yb_ref, yc_ref, g0_ref, g1_ref, g2_ref, bg_ref, wb_ref, wo_ref,
                  pg_ref, o_ref):
    m = jnp.zeros(o_ref.shape, F32)
    for nb, (y_ref, g_ref) in enumerate(((ya_ref, g0_ref), (yb_ref, g1_ref), (yc_ref, g2_ref))):
        gate = jax.nn.sigmoid(g_ref[...] + bg_ref[nb:nb + 1, :])
        m = m + gate * jnp.dot(y_ref[...], wb_ref[nb], preferred_element_type=F32)
    y = jnp.dot(m.astype(BF16), wo_ref[...], preferred_element_type=F32)
    o_ref[...] = x_ref[...] + y * _rms_scale(y) * pg_ref[...]


def _merge(x, ya, yb, yc, z, b_gate, wb, wo, pg, tm):
    n, d = x.shape
    c = BRANCH_WIDTH
    once = pl.Buffered(1)
    branch = lambda: pl.BlockSpec((tm, c), lambda i: (i, 0))
    gate = lambda nb: pl.BlockSpec((tm, d), lambda i: (i, GATE_COL0 + nb))
    return pl.pallas_call(
        _merge_kernel,
        grid=(n // tm,),
        in_specs=[
            pl.BlockSpec((tm, d), lambda i: (i, 0)),
            branch(), branch(), branch(),
            gate(0), gate(1), gate(2),
            pl.BlockSpec((N_BRANCH, d), lambda i: (0, 0)),
            pl.BlockSpec((N_BRANCH, c, d), lambda i: (0, 0, 0), pipeline_mode=once),
            pl.BlockSpec((d, d), lambda i: (0, 0), pipeline_mode=once),
            pl.BlockSpec((1, d), lambda i: (0, 0)),
        ],
        out_specs=pl.BlockSpec((tm, d), lambda i: (i, 0)),
        out_shape=jax.ShapeDtypeStruct((n, d), F32),
        compiler_params=_params(("arbitrary",)),
        name="merge",
    )(x, ya, yb, yc, z, z, z, b_gate, wb, wo, pg.reshape(1, d))


def _ffn_kernel(x_ref, g_ref, wu_ref, wv_ref, cw_ref, cb_ref, fc0_ref, wd_ref, pg_ref,
                o_ref, fc1_ref, xn_ref, acc_ref, hal_ref, car_ref, *, nb, T, tf):
    it = pl.program_id(1)
    j = pl.program_id(2)
    nj = pl.num_programs(2)
    hist = FFN_CONV_W - 1

    @pl.when(j == 0)
    def _():
        x = x_ref[...].reshape(nb * T, D_MODEL)
        xn_ref[...] = (x * _rms_scale(x) * g_ref[...]).astype(BF16)
        acc_ref[...] = jnp.zeros_like(acc_ref)

    @pl.when(it == 0)
    def _():
        car_ref[j] = fc0_ref[...]

    xn = xn_ref[...]
    u = jnp.dot(xn, wu_ref[...], preferred_element_type=F32)
    v = jnp.dot(xn, wv_ref[...], preferred_element_type=F32)
    hal_ref[:, HALO - hist:HALO, :] = car_ref[j]
    hal_ref[:, HALO:HALO + T, :] = u.reshape(nb, T, tf)
    cw = cw_ref[...]
    uc = cb_ref[...][None]
    for jj in range(FFN_CONV_W):
        uc = uc + hal_ref[:, HALO - hist + jj:HALO - hist + jj + T, :] * cw[jj:jj + 1][None]
    tail = hal_ref[:, HALO + T - hist:HALO + T, :]
    car_ref[j] = tail
    act = (jax.nn.gelu(uc).reshape(nb * T, tf) * v).astype(BF16)
    acc_ref[...] += jnp.dot(act, wd_ref[...], preferred_element_type=F32)

    @pl.when(j == nj - 1)
    def _():
        y = acc_ref[...]
        out = x_ref[...].reshape(nb * T, D_MODEL) + y * _rms_scale(y) * pg_ref[...]
        o_ref[...] = out.reshape(nb, T, D_MODEL)

    @pl.when((j == nj - 1) & (it == pl.num_programs(1) - 1))
    def _():
        for jj in range(FFN_DIM // tf):
            fc1_ref[:, :, jj * tf:(jj + 1) * tf] = car_ref[jj]


def _ffn(x3, g, w_up, cw, cb, fc0, w_down, pg, nb, T, tf):
    B, L, d = x3.shape
    F = FFN_DIM
    nj = F // tf
    hist = FFN_CONV_W - 1
    return pl.pallas_call(
        functools.partial(_ffn_kernel, nb=nb, T=T, tf=tf),
        grid=(B // nb, L // T, nj),
        in_specs=[
            pl.BlockSpec((nb, T, d), lambda b, t, j: (b, t, 0)),
            pl.BlockSpec((1, d), lambda b, t, j: (0, 0)),
            pl.BlockSpec((d, tf), lambda b, t, j: (0, j)),
            pl.BlockSpec((d, tf), lambda b, t, j: (0, nj + j)),
            pl.BlockSpec((FFN_CONV_W, tf), lambda b, t, j: (0, j)),
            pl.BlockSpec((1, tf), lambda b, t, j: (0, j)),
            pl.BlockSpec((nb, hist, tf), lambda b, t, j: (b, 0, j)),
            pl.BlockSpec((tf, d), lambda b, t, j: (j, 0)),
            pl.BlockSpec((1, d), lambda b, t, j: (0, 0)),
        ],
        out_specs=[
            pl.BlockSpec((nb, T, d), lambda b, t, j: (b, t, 0)),
            pl.BlockSpec((nb, hist, F), lambda b, t, j: (b, 0, 0)),
        ],
        out_shape=[
            jax.ShapeDtypeStruct((B, L, d), F32),
            jax.ShapeDtypeStruct((B, hist, F), F32),
        ],
        scratch_shapes=[
            pltpu.VMEM((nb * T, d), BF16),
            pltpu.VMEM((nb * T, d), F32),
            pltpu.VMEM((nb, HALO + T, tf), F32),
            pltpu.VMEM((nj, nb, hist, tf), F32),
        ],
        compiler_params=_params(("arbitrary", "arbitrary", "arbitrary")),
        name="ffn",
    )(x3, g.reshape(1, d), w_up, w_up, cw, cb.reshape(1, F), fc0, w_down, pg.reshape(1, d))


def _block_diag(w):
    nblk, bi, bj = w.shape
    eye = jnp.eye(nblk, dtype=w.dtype)
    return jnp.einsum("hij,hg->higj", w, eye).reshape(nblk * bi, nblk * bj)


def _tiles(B, L):
    T = min(L, 256)
    return dict(
        proj_tm=min(B * L, 1024), proj_tn=1024,
        rnn_nb=B if B * T <= 512 else 512 // T, rnn_T=T,
        hg_nb=min(B, 4), hg_T=T,
        xa_T=min(L, 512),
        merge_tm=256,
        ffn_nb=B if B * T <= 512 else 512 // T, ffn_T=T, ffn_tf=512,
    )


def _trunk_layer(x, mk, mv, h0, rc0, s0, fc0, w, layer):
    B, L, d = x.shape
    n = B * L
    t = _tiles(B, L)
    x2 = x.reshape(n, d)
    z = _norm_matmul(x2, w["pre_mix_norm"], w["w_in"], t["proj_tm"], t["proj_tn"], "in_proj")
    z3 = z.reshape(B, L, IN_COLS)
    ya, h1, rc1 = _rglru(z3, rc0, h0, w["rnn_conv_w"], w["rnn_conv_b"], w["lru_wa"], w["lru_ba"],
                         w["lru_wx"], w["lru_bx"], w["lru_lambda"], t["rnn_nb"], t["rnn_T"])
    yb, s1 = _hgrn2(z3, s0, w["hg_lb"], w["hg_norm"], layer, t["hg_nb"], t["hg_T"])
    yc = _xattn(z3, mk, mv, t["xa_T"])
    c = BRANCH_WIDTH
    x1 = _merge(x2, ya.reshape(n, c), yb.reshape(n, c), yc.reshape(n, c), z, w["b_gate"],
                w["w_branch"], w["w_out"], w["post_mix_norm"], t["merge_tm"])
    xo, fc1 = _ffn(x1.reshape(B, L, d), w["pre_ffn_norm"], w["w_ffn_up"], w["ffn_conv_w"],
                   w["ffn_conv_b"], fc0, w["w_ffn_down"], w["post_ffn_norm"],
                   t["ffn_nb"], t["ffn_T"], t["ffn_tf"])
    return xo, h1, rc1, s1, fc1


def kernel(x_prompt, x_sample, cache_mem_k, cache_mem_v, state_rnn_h, state_rnn_conv, state_hg,
           state_ffn_conv, mem_prompt, pre_mix_norm, w_in, rnn_conv_w, rnn_conv_b, lru_wa, lru_ba,
           lru_wx, lru_bx, lru_lambda, hg_lb, hg_norm, mem_norm, w_mem_kv, w_branch, b_gate, w_out,
           post_mix_norm, pre_ffn_norm, w_ffn_up, ffn_conv_w, ffn_conv_b, w_ffn_down, post_ffn_norm):
    depth = w_in.shape[0]
    Bp = x_prompt.shape[0]
    Bs = x_sample.shape[0]
    xa_w = XA_HEADS * XA_HD
    yp, ys = x_prompt, x_sample
    outs = [[] for _ in range(10)]
    for l in range(depth):
        w = {
            "pre_mix_norm": pre_mix_norm[l], "w_in": w_in[l].astype(BF16),
            "rnn_conv_w": rnn_conv_w[l], "rnn_conv_b": rnn_conv_b[l],
            "lru_wa": _block_diag(lru_wa[l]).astype(BF16), "lru_ba": lru_ba[l],
            "lru_wx": _block_diag(lru_wx[l]).astype(BF16), "lru_bx": lru_bx[l],
            "lru_lambda": lru_lambda[l], "hg_lb": hg_lb, "hg_norm": hg_norm[l],
            "w_branch": w_branch[l].astype(BF16), "b_gate": b_gate[l],
            "w_out": w_out[l].astype(BF16), "post_mix_norm": post_mix_norm[l],
            "pre_ffn_norm": pre_ffn_norm[l], "w_ffn_up": w_ffn_up[l].astype(BF16),
            "ffn_conv_w": ffn_conv_w[l], "ffn_conv_b": ffn_conv_b[l],
            "w_ffn_down": w_ffn_down[l].astype(BF16), "post_ffn_norm": post_ffn_norm[l],
        }
        mem2 = mem_prompt.reshape(Bp * MEM_LEN, D_MODEL)
        kv = _norm_matmul(mem2, mem_norm[l], w_mem_kv[l].astype(BF16), Bp * MEM_LEN, 1024, "mem_kv")
        kv = kv.reshape(Bp, MEM_LEN, 2 * xa_w)
        mk_p, mv_p = kv[..., :xa_w], kv[..., xa_w:]
        yp, h_p, rc_p, s_p, fc_p = _trunk_layer(
            yp, mk_p, mv_p, jnp.zeros((Bp, RNN_WIDTH), F32),
            jnp.zeros((Bp, RNN_CONV_W - 1, RNN_WIDTH), F32),
            jnp.zeros((Bp, HG_HEADS, HG_DK, HG_DV), F32),
            jnp.zeros((Bp, FFN_CONV_W - 1, FFN_DIM), F32), w, l)
        ys, h_s, rc_s, s_s, fc_s = _trunk_layer(
            ys, cache_mem_k[l].reshape(Bs, MEM_LEN, xa_w), cache_mem_v[l].reshape(Bs, MEM_LEN, xa_w),
            state_rnn_h[l], state_rnn_conv[l], state_hg[l], state_ffn_conv[l], w, l)
        layer_out = (mk_p.reshape(Bp, MEM_LEN, XA_HEADS, XA_HD), mv_p.reshape(Bp, MEM_LEN, XA_HEADS, XA_HD),
                     h_p, rc_p, s_p, fc_p, h_s, rc_s, s_s, fc_s)
        for acc, val in zip(outs, layer_out):
            acc.append(val)
    return (yp, ys) + tuple(jnp.stack(o) for o in outs)
```

```python
import functools

import jax
import jax.numpy as jnp
from jax import lax
from jax.experimental import pallas as pl
from jax.experimental.pallas import tpu as pltpu

F32 = jnp.float32
BF16 = jnp.bfloat16

D_MODEL = 2048
RNN_WIDTH = 1024
RNN_BLOCKS = 16
RNN_CONV_W = 4
LRU_C = 8.0
HG_HEADS = 8
HG_DK = 128
HG_DV = 128
HG_CHUNK = 64
HG_SUB = 16
HG_SKEW = 3
MEM_LEN = 256
XA_HEADS = 4
XA_HD = 256
BRANCH_WIDTH = 1024
N_BRANCH = 3
FFN_DIM = 5632
FFN_CONV_W = 3
EPS = 1e-6
LOG2E = 1.4426950408889634
NEG_BIG = -1e30
IN_COLS = 6 * BRANCH_WIDTH + N_BRANCH * D_MODEL

COL_RNN, COL_XQ = 0, 1
GATE_COL0 = 1
COL_HQ, COL_HF, COL_HI, COL_HO = 8, 9, 10, 11
HG_COLS = 4
MXU_COLS = 256

HALO = 8
VMEM_LIMIT = 56 * 1024 * 1024


def _params(sem):
    return pltpu.CompilerParams(dimension_semantics=sem, vmem_limit_bytes=VMEM_LIMIT)


def _rms_scale(x):
    return lax.rsqrt(jnp.mean(x * x, axis=-1, keepdims=True) + EPS)


def _norm_matmul_kernel(x_ref, g_ref, w_ref, o_ref, xn_ref):
    @pl.when(pl.program_id(1) == 0)
    def _():
        x = x_ref[...]
        xn_ref[...] = (x * _rms_scale(x) * g_ref[...]).astype(BF16)

    o_ref[...] = jnp.dot(xn_ref[...], w_ref[...], preferred_element_type=F32)


def _norm_matmul(x, g, w, tm, tn, name):
    n, d = x.shape
    c = w.shape[1]
    return pl.pallas_call(
        _norm_matmul_kernel,
        grid=(n // tm, c // tn),
        in_specs=[
            pl.BlockSpec((tm, d), lambda i, j: (i, 0)),
            pl.BlockSpec((1, d), lambda i, j: (0, 0)),
            pl.BlockSpec((d, tn), lambda i, j: (0, j)),
        ],
        out_specs=pl.BlockSpec((tm, tn), lambda i, j: (i, j)),
        out_shape=jax.ShapeDtypeStruct((n, c), F32),
        scratch_shapes=[pltpu.VMEM((tm, d), BF16)],
        compiler_params=_params(("arbitrary", "arbitrary")),
        name=name,
    )(x, g.reshape(1, d), w)


def _rglru_kernel(z_ref, rc0_ref, h0_ref, cw_ref, cb_ref, wa_ref, ba_ref, wx_ref, bx_ref, lam_ref,
                  ya_ref, h1_ref, rc1_ref, xp_ref, a_ref, b_ref, hs_ref, hc_ref, *, nb, T):
    it = pl.program_id(1)
    nt = pl.num_programs(1)
    hist = RNN_CONV_W - 1

    @pl.when(it == 0)
    def _():
        xp_ref[:, HALO - hist:HALO, :] = rc0_ref[...]
        hc_ref[...] = h0_ref[...]

    xp_ref[:, HALO:HALO + T, :] = z_ref[...]
    cw = cw_ref[...]
    xr = cb_ref[...][None]
    for j in range(RNN_CONV_W):
        xr = xr + xp_ref[:, HALO - hist + j:HALO - hist + j + T, :] * cw[j:j + 1][None]
    xp_ref[:, HALO - hist:HALO, :] = xp_ref[:, HALO + T - hist:HALO + T, :]

    xr2 = xr.reshape(nb * T, RNN_WIDTH)
    xb = xr2.astype(BF16)
    r = jax.nn.sigmoid(jnp.dot(xb, wa_ref[...], preferred_element_type=F32) + ba_ref[...])
    ig = jax.nn.sigmoid(jnp.dot(xb, wx_ref[...], preferred_element_type=F32) + bx_ref[...])
    nl = -lam_ref[...]
    softplus = jnp.maximum(nl, 0.0) + jnp.log1p(jnp.exp(-jnp.abs(nl)))
    log_a = -LRU_C * r * softplus
    a = jnp.exp(log_a)
    a_ref[...] = a.reshape(nb, T, RNN_WIDTH)
    one_minus_a2 = -jnp.tanh(log_a) * (a * a + 1.0)
    b_ref[...] = (jnp.sqrt(one_minus_a2) * (ig * xr2)).reshape(nb, T, RNN_WIDTH)

    hs = [hc_ref[bi:bi + 1, :] for bi in range(nb)]
    for t in range(T):
        for bi in range(nb):
            hs[bi] = a_ref[bi, t:t + 1, :] * hs[bi] + b_ref[bi, t:t + 1, :]
            hs_ref[bi, t:t + 1, :] = hs[bi]
    for bi in range(nb):
        hc_ref[bi:bi + 1, :] = hs[bi]
    ya_ref[...] = hs_ref[...].astype(BF16)

    @pl.when(it == nt - 1)
    def _():
        h1_ref[...] = hc_ref[...]
        rc1_ref[...] = xp_ref[:, HALO - hist:HALO, :]


def _rglru(z3, rc0, h0, cw, cb, wa, ba, wx, bx, lam, nb, T):
    B, L, _ = z3.shape
    C = RNN_WIDTH
    hist = RNN_CONV_W - 1
    vec = lambda: pl.BlockSpec((1, C), lambda b, t: (0, 0))
    mat = lambda: pl.BlockSpec((C, C), lambda b, t: (0, 0))
    return pl.pallas_call(
        functools.partial(_rglru_kernel, nb=nb, T=T),
        grid=(B // nb, L // T),
        in_specs=[
            pl.BlockSpec((nb, T, C), lambda b, t: (b, t, COL_RNN)),
            pl.BlockSpec((nb, hist, C), lambda b, t: (b, 0, 0)),
            pl.BlockSpec((nb, C), lambda b, t: (b, 0)),
            pl.BlockSpec((RNN_CONV_W, C), lambda b, t: (0, 0)),
            vec(), mat(), vec(), mat(), vec(), vec(),
        ],
        out_specs=[
            pl.BlockSpec((nb, T, C), lambda b, t: (b, t, 0)),
            pl.BlockSpec((nb, C), lambda b, t: (b, 0)),
            pl.BlockSpec((nb, hist, C), lambda b, t: (b, 0, 0)),
        ],
        out_shape=[
            jax.ShapeDtypeStruct((B, L, C), BF16),
            jax.ShapeDtypeStruct((B, C), F32),
            jax.ShapeDtypeStruct((B, hist, C), F32),
        ],
        scratch_shapes=[
            pltpu.VMEM((nb, HALO + T, C), F32),
            pltpu.VMEM((nb, T, C), F32),
            pltpu.VMEM((nb, T, C), F32),
            pltpu.VMEM((nb, T, C), F32),
            pltpu.VMEM((nb, C), F32),
        ],
        compiler_params=_params(("arbitrary", "arbitrary")),
        name="rglru",
    )(z3, rc0, h0, cw, cb.reshape(1, C), wa, ba.reshape(1, C), wx, bx.reshape(1, C), lam.reshape(1, C))


def _dot_nt(a, b):
    return lax.dot_general(a, b, (((1,), (1,)), ((), ())), preferred_element_type=F32)


def _split3(x):
    hi = x.astype(BF16)
    r = x - hi.astype(F32)
    mid = r.astype(BF16)
    lo = (r - mid.astype(F32)).astype(BF16)
    return jnp.concatenate([hi, mid, lo], axis=0)


def _hgrn2_scores(qh, bc, ck, ck_row, st, cs, mxu_lane_sum=True):
    o = _dot_nt((qh * jnp.exp2(bc)).astype(BF16), st.astype(BF16))

    nblk = cs // HG_SUB
    a_rows = [None] * nblk
    half = cs // 2
    while half >= HG_SUB:
        npair = cs // (2 * half)
        q_parts, k_parts = [], []
        for p in range(npair):
            lo, mid, hi = 2 * half * p, 2 * half * p + half, 2 * half * (p + 1)
            br = bc[mid - 1:mid]
            q_parts.append(qh[mid:hi] * jnp.exp2(bc[mid:hi] - br))
            k_parts += [jnp.exp2(br - ck[lo:mid]), jnp.zeros((half, HG_DK), F32)]
        off = _dot_nt(jnp.concatenate(q_parts, axis=0).astype(BF16),
                      jnp.concatenate(k_parts, axis=0).astype(BF16))
        if npair > 1:
            lg = half.bit_length() - 1
            rowp = lax.shift_right_logical(lax.broadcasted_iota(jnp.int32, off.shape, 0), lg)
            colp = lax.shift_right_logical(lax.broadcasted_iota(jnp.int32, off.shape, 1), lg + 1)
            off = jnp.where(rowp == colp, off, 0.0)
        for p in range(npair):
            for sub in range(half // HG_SUB):
                blk = (2 * half * p + half) // HG_SUB + sub
                piece = off[p * half + sub * HG_SUB:p * half + (sub + 1) * HG_SUB]
                a_rows[blk] = piece if a_rows[blk] is None else a_rows[blk] + piece
        half //= 2

    hs = HG_SUB // 2
    ones = jnp.ones((HG_DK, HG_DK), BF16)
    zero = jnp.zeros((hs, HG_DK), F32)
    lane_sums = []
    for blk in range(nblk):
        r0 = blk * HG_SUB
        b_lo, b_hi = bc[r0:r0 + hs], bc[r0 + hs:r0 + HG_SUB]
        q_lo, q_hi = qh[r0:r0 + hs], qh[r0 + hs:r0 + HG_SUB]
        pair = []
        for s in range(HG_SUB):
            cks = ck_row(r0 + s)
            d_lo = q_lo * jnp.exp2(b_lo - cks) if s < hs else None
            d_hi = q_hi * jnp.exp2(b_hi - cks)
            if mxu_lane_sum:
                pair.append(jnp.concatenate([zero if d_lo is None else d_lo, d_hi], axis=0).astype(BF16))
            else:
                pair.append((None if d_lo is None else jnp.sum(d_lo, axis=-1, keepdims=True),
                             jnp.sum(d_hi, axis=-1, keepdims=True)))
        if mxu_lane_sum:
            r = jnp.dot(jnp.concatenate(pair, axis=0), ones, preferred_element_type=F32)
            pair = [(r[s * HG_SUB:s * HG_SUB + hs] if s < hs else None,
                     r[s * HG_SUB + hs:(s + 1) * HG_SUB]) for s in range(HG_SUB)]
        lane_sums.append(pair)
    return o, a_rows, lane_sums


def _hgrn2_output(scores, vh, ogh, gnh, bl, ck, st, cs):
    o, a_rows, lane_sums = scores
    return _hgrn2_finish(o, _hgrn2_assemble(a_rows, lane_sums, cs), vh, ogh, gnh, bl, ck, st)


def _hgrn2_assemble(a_rows, lane_sums, cs):
    hs = HG_SUB // 2
    lane = lax.broadcasted_iota(jnp.int32, (hs, HG_DK), 1)
    tloc = lax.broadcasted_iota(jnp.int32, (hs, HG_DK), 0)
    lo_mask = [(lane == s) & (tloc >= s) for s in range(hs)]
    hi_mask = [lane == s for s in range(hs)] + [(lane == s) & (tloc >= s - hs) for s in range(hs, HG_SUB)]
    zero = jnp.zeros((hs, HG_DK), F32)
    a_rows = list(a_rows)
    for blk, pair in enumerate(lane_sums):
        a_lo, a_hi = zero, zero
        for s, (r_lo, r_hi) in enumerate(pair):
            if s < hs:
                a_lo = jnp.where(lo_mask[s], r_lo, a_lo)
            a_hi = jnp.where(hi_mask[s], r_hi, a_hi)
        a_d = jnp.concatenate([a_lo, a_hi], axis=0)
        if blk:
            a_d = pltpu.roll(a_d, blk * HG_SUB, 1)
        a_d = a_d[:, :cs]
        a_rows[blk] = a_d if a_rows[blk] is None else a_rows[blk] + a_d
    return jnp.concatenate(a_rows, axis=0)


def _hgrn2_finish(o, a, vh, ogh, gnh, bl, ck, st):
    o = o + jnp.dot(a.astype(BF16), vh.astype(BF16), preferred_element_type=F32)
    kd = jnp.exp2(bl - ck).astype(BF16)
    st_new = st * jnp.exp2(bl) + jnp.dot(vh.T.astype(BF16), kd, preferred_element_type=F32)
    y = o * _rms_scale(o) * gnh * jax.nn.sigmoid(ogh)
    return y, st_new


def _hgrn2_kernel(q_ref, f_ref, v_ref, og_ref, s0_ref, lbp_ref, gn_ref, yb_ref, s1_ref,
                  lb_ref, bc_ref, ck_ref, *st_refs, nb, T, cs, layer):
    it = pl.program_id(1)
    nt = pl.num_programs(1)
    row = lax.broadcasted_iota(jnp.int32, (cs, 3 * cs), 0)
    col = lax.broadcasted_iota(jnp.int32, (cs, 3 * cs), 1)
    tri3 = (row >= (col & (cs - 1))).astype(BF16)

    @pl.when(it == 0)
    def _():
        p = lbp_ref[...]
        e = jnp.exp(p - jnp.max(p, axis=0, keepdims=True))
        sm = e / jnp.sum(e, axis=0, keepdims=True)
        lb_ref[...] = jnp.sum(sm[:layer + 1], axis=0, keepdims=True)

        def init(b, c):
            for h in range(HG_HEADS):
                st_refs[h][b] = s0_ref[b, h].T
            return c
        lax.fori_loop(0, nb, init, 0)

    nch = T // cs

    def chunk(idx, c):
        b = idx // nch
        rows = pl.ds(pl.multiple_of((idx % nch) * cs, cs), cs)
        lb = lb_ref[...]
        f = lb + (1.0 - lb) * jax.nn.sigmoid(f_ref[b, rows, :])
        bc = jnp.dot(tri3, _split3(jnp.log(f) * LOG2E), preferred_element_type=F32)
        bc_ref[...] = bc
        ck_ref[...] = bc - jnp.log(1.0 - f) * LOG2E

        def scores(h):
            hl = slice(h * HG_DK, (h + 1) * HG_DK)
            return _hgrn2_scores(q_ref[b, rows, hl], bc_ref[:, hl], ck_ref[:, hl],
                                 lambda r: ck_ref[r:r + 1, hl], st_refs[h][b], cs)

        def output(h, sc):
            hl = slice(h * HG_DK, (h + 1) * HG_DK)
            y, st_new = _hgrn2_output(sc, v_ref[b, rows, hl], og_ref[b, rows, hl], gn_ref[...],
                                      bc_ref[cs - 1:cs, hl], ck_ref[:, hl], st_refs[h][b], cs)
            yb_ref[b, rows, hl] = y.astype(BF16)
            st_refs[h][b] = st_new

        pending = {}
        for h in range(HG_HEADS + HG_SKEW):
            if h < HG_HEADS:
                pending[h] = scores(h)
            if h >= HG_SKEW:
                output(h - HG_SKEW, pending.pop(h - HG_SKEW))
        return c
    lax.fori_loop(0, nb * nch, chunk, 0)

    @pl.when(it == nt - 1)
    def _():
        def fin(b, c):
            for h in range(HG_HEADS):
                s1_ref[b, h] = st_refs[h][b].T
            return c
        lax.fori_loop(0, nb, fin, 0)


def _hgrn2(z3, s0, hg_lb, gn, layer, nb, T):
    B, L, _ = z3.shape
    C = HG_HEADS * HG_DK
    cs = min(HG_CHUNK, L)
    zcol = lambda col: pl.BlockSpec((nb, T, C), lambda b, t: (b, t, col))
    st_spec = lambda: pl.BlockSpec((nb, HG_HEADS, HG_DK, HG_DV), lambda b, t: (b, 0, 0, 0))
    nl = hg_lb.shape[0]
    return pl.pallas_call(
        functools.partial(_hgrn2_kernel, nb=nb, T=T, cs=cs, layer=layer),
        grid=(B // nb, L // T),
        in_specs=[
            zcol(COL_HQ), zcol(COL_HF), zcol(COL_HI), zcol(COL_HO),
            st_spec(),
            pl.BlockSpec((nl, C), lambda b, t: (0, 0)),
            pl.BlockSpec((1, HG_DV), lambda b, t: (0, 0)),
        ],
        out_specs=[pl.BlockSpec((nb, T, C), lambda b, t: (b, t, 0)), st_spec()],
        out_shape=[
            jax.ShapeDtypeStruct((B, L, C), BF16),
            jax.ShapeDtypeStruct((B, HG_HEADS, HG_DK, HG_DV), F32),
        ],
        scratch_shapes=[
            pltpu.VMEM((1, C), F32),
            pltpu.VMEM((cs, C), F32),
            pltpu.VMEM((cs, C), F32),
        ] + [pltpu.VMEM((nb, HG_DV, HG_DK), F32) for _ in range(HG_HEADS)],
        compiler_params=_params(("arbitrary", "arbitrary")),
        name="hgrn2",
    )(z3, z3, z3, z3, s0, hg_lb, gn.reshape(1, HG_DV))


def _proj_hg_kernel(x_ref, g_ref, w_ref, s0_ref, lbp_ref, gn_ref, o_ref, yb_ref, s1_ref,
                    xn_ref, zk_ref, rin_ref, ybs_ref, lb_ref, bc_ref, ck_ref, oi_ref, a_ref, *st_refs,
                    cs, layer, tiles_per_batch, npiece):
    i = pl.program_id(0)
    j = pl.program_id(1)
    nj = pl.num_programs(1)
    tm = xn_ref.shape[0]
    nch = tm // cs
    C = HG_HEADS * HG_DK
    finishing = (i >= 1) & (j >= 1) & (j <= nch)
    cur = j & 1
    prv = 1 - cur
    tile_in_batch = lax.rem(jnp.maximum(i - 1, 0), tiles_per_batch)
    row = lax.broadcasted_iota(jnp.int32, (cs, 3 * cs), 0)
    col = lax.broadcasted_iota(jnp.int32, (cs, 3 * cs), 1)
    tri3 = (row >= (col & (cs - 1))).astype(BF16)

    @pl.when(j == 0)
    def _():
        x = x_ref[...]
        xn_ref[...] = (x * _rms_scale(x) * g_ref[...]).astype(BF16)

    @pl.when((i == 0) & (j == 0))
    def _():
        p = lbp_ref[...]
        e = jnp.exp(p - jnp.max(p, axis=0, keepdims=True))
        sm = e / jnp.sum(e, axis=0, keepdims=True)
        lb_ref[...] = jnp.sum(sm[:layer + 1], axis=0, keepdims=True)
        for s in range(HG_COLS + 1):
            zk_ref[s] = jnp.zeros((tm, C), F32)
        for ref in (rin_ref, bc_ref, ck_ref, oi_ref, a_ref) + tuple(st_refs):
            ref[...] = jnp.zeros(ref.shape, F32)

    @pl.when((i >= 1) & (tile_in_batch == 0) & (j == 0))
    def _():
        for h in range(HG_HEADS):
            st_refs[h][...] = s0_ref[0, h].T

    rows = pl.ds(pl.multiple_of(jnp.minimum(j, nch - 1) * cs, cs), cs)
    for s in range(HG_COLS):
        rin_ref[cur, s] = zk_ref[s, rows, :]
    keep = jnp.where(j >= nj - HG_COLS, j - (nj - HG_COLS), HG_COLS)
    pw = o_ref.shape[1] // npiece

    def piece(k):
        res = jnp.dot(xn_ref[...], w_ref[:, k * pw:(k + 1) * pw], preferred_element_type=F32)
        o_ref[:, k * pw:(k + 1) * pw] = res
        zk_ref[keep, :, k * pw:(k + 1) * pw] = res

    piece(0)
    for h in range(HG_HEADS):
        hl = slice(h * HG_DK, (h + 1) * HG_DK)
        st = st_refs[h][...]
        y, st_new = _hgrn2_finish(oi_ref[prv, h], a_ref[prv, h], rin_ref[prv, 2, :, hl],
                                  rin_ref[prv, 3, :, hl], gn_ref[...], bc_ref[prv, cs - 1:cs, hl],
                                  ck_ref[prv, :, hl], st)
        ybs_ref[:, hl] = y.astype(BF16)
        st_refs[h][...] = jnp.where(finishing, st_new, st)
    piece(1)
    lb = lb_ref[...]
    f = lb + (1.0 - lb) * jax.nn.sigmoid(rin_ref[cur, 1])
    bc = jnp.dot(tri3, _split3(jnp.log(f) * LOG2E), preferred_element_type=F32)
    bc_ref[cur] = bc
    ck_ref[cur] = bc - jnp.log(1.0 - f) * LOG2E
    piece(2)
    for h in range(HG_HEADS):
        hl = slice(h * HG_DK, (h + 1) * HG_DK)
        o_inter, a_rows, lane_sums = _hgrn2_scores(
            rin_ref[cur, 0, :, hl], bc_ref[cur, :, hl], ck_ref[cur, :, hl],
            lambda r, hl=hl: ck_ref[cur, r:r + 1, hl], st_refs[h][...], cs, mxu_lane_sum=False)
        oi_ref[cur, h] = o_inter
        a_ref[cur, h] = _hgrn2_assemble(a_rows, lane_sums, cs)
    for k in range(3, npiece):
        piece(k)

    @pl.when(finishing)
    def _():
        yb_ref[...] = ybs_ref[...]

    @pl.when(finishing & (tile_in_batch == tiles_per_batch - 1) & (j == nch))
    def _():
        for h in range(HG_HEADS):
            s1_ref[0, h] = st_refs[h][...].T


def _proj_with_hgrn2(x, g, w, s0, hg_lb, gn, layer, L, tm, tn):
    n, d = x.shape
    c = w.shape[1]
    cs = HG_CHUNK
    C = HG_HEADS * HG_DK
    nrow, nj, nch = n // tm, c // tn, tm // cs
    assert tn == C and L % tm == 0 and tm % cs == 0 and nch <= nj - HG_COLS
    tiles_per_batch = L // tm
    last = nrow - 1
    prev = lambda i: jnp.maximum(i - 1, 0)
    batch = lambda i: prev(i) // tiles_per_batch
    col = lambda i, j: jnp.where(i <= last, j, nj - 1)
    st = lambda: pl.BlockSpec((1, HG_HEADS, HG_DK, HG_DV), lambda i, j: (batch(i), 0, 0, 0))
    nl = hg_lb.shape[0]
    return pl.pallas_call(
        functools.partial(_proj_hg_kernel, cs=cs, layer=layer, tiles_per_batch=tiles_per_batch,
                          npiece=tn // MXU_COLS),
        grid=(nrow + 1, nj),
        in_specs=[
            pl.BlockSpec((tm, d), lambda i, j: (jnp.minimum(i, last), 0)),
            pl.BlockSpec((1, d), lambda i, j: (0, 0)),
            pl.BlockSpec((d, tn), lambda i, j: (0, col(i, j))),
            st(),
            pl.BlockSpec((nl, C), lambda i, j: (0, 0)),
            pl.BlockSpec((1, HG_DV), lambda i, j: (0, 0)),
        ],
        out_specs=[
            pl.BlockSpec((tm, tn), lambda i, j: (jnp.minimum(i, last), col(i, j))),
            pl.BlockSpec((cs, C), lambda i, j: (
                jnp.where(i >= 1, (i - 1) * nch + jnp.clip(j - 1, 0, nch - 1), 0), 0)),
            st(),
        ],
        out_shape=[
            jax.ShapeDtypeStruct((n, c), F32),
            jax.ShapeDtypeStruct((n, C), BF16),
            jax.ShapeDtypeStruct((n // L, HG_HEADS, HG_DK, HG_DV), F32),
        ],
        scratch_shapes=[
            pltpu.VMEM((tm, d), BF16),
            pltpu.VMEM((HG_COLS + 1, tm, C), F32),
            pltpu.VMEM((2, HG_COLS, cs, C), F32),
            pltpu.VMEM((cs, C), BF16),
            pltpu.VMEM((1, C), F32),
            pltpu.VMEM((2, cs, C), F32),
            pltpu.VMEM((2, cs, C), F32),
            pltpu.VMEM((2, HG_HEADS, cs, HG_DV), F32),
            pltpu.VMEM((2, HG_HEADS, cs, cs), F32),
        ] + [pltpu.VMEM((HG_DV, HG_DK), F32) for _ in range(HG_HEADS)],
        compiler_params=_params(("arbitrary", "arbitrary")),
        name="proj_hg",
    )(x, g.reshape(1, d), w, s0, hg_lb, gn.reshape(1, HG_DV))


def _xattn_kernel(q_ref, k_ref, v_ref, o_ref):
    for h in range(XA_HEADS):
        hl = slice(h * XA_HD, (h + 1) * XA_HD)
        s = _dot_nt(q_ref[0, :, hl].astype(BF16), k_ref[0, :, hl].astype(BF16)) * (XA_HD ** -0.5)
        e = jnp.exp(s - jnp.max(s, axis=-1, keepdims=True))
        p = e / jnp.sum(e, axis=-1, keepdims=True)
        o = jnp.dot(p.astype(BF16), v_ref[0, :, hl].astype(BF16), preferred_element_type=F32)
        o_ref[0, :, hl] = o.astype(BF16)


def _xattn(z3, mk, mv, T):
    B, L, _ = z3.shape
    C = XA_HEADS * XA_HD
    mem = lambda: pl.BlockSpec((1, MEM_LEN, C), lambda b, t: (b, 0, 0))
    return pl.pallas_call(
        _xattn_kernel,
        grid=(B, L // T),
        in_specs=[pl.BlockSpec((1, T, C), lambda b, t: (b, t, COL_XQ)), mem(), mem()],
        out_specs=pl.BlockSpec((1, T, C), lambda b, t: (b, t, 0)),
        out_shape=jax.ShapeDtypeStruct((B, L, C), BF16),
        compiler_params=_params(("arbitrary", "arbitrary")),
        name="xattn",
    )(z3, mk, mv)


def _merge_kernel(x_ref, ya_ref, yb_ref, yc_ref, g0_ref, g1_ref, g2_ref, bg_ref, wb_ref, wo_ref,
                  pg_ref, o_ref):
    m = jnp.zeros(o_ref.shape, F32)
    for nb, (y_ref, g_ref) in enumerate(((ya_ref, g0_ref), (yb_ref, g1_ref), (yc_ref, g2_ref))):
        gate = jax.nn.sigmoid(g_ref[...] + bg_ref[nb:nb + 1, :])
        m = m + gate * jnp.dot(y_ref[...], wb_ref[nb], preferred_element_type=F32)
    y = jnp.dot(m.astype(BF16), wo_ref[...], preferred_element_type=F32)
    o_ref[...] = x_ref[...] + y * _rms_scale(y) * pg_ref[...]


def _merge(x, ya, yb, yc, z, b_gate, wb, wo, pg, tm):
    n, d = x.shape
    c = BRANCH_WIDTH
    once = pl.Buffered(1)
    branch = lambda: pl.BlockSpec((tm, c), lambda i: (i, 0))
    gate = lambda nb: pl.BlockSpec((tm, d), lambda i: (i, GATE_COL0 + nb))
    return pl.pallas_call(
        _merge_kernel,
        grid=(n // tm,),
        in_specs=[
            pl.BlockSpec((tm, d), lambda i: (i, 0)),
            branch(), branch(), branch(),
            gate(0), gate(1), gate(2),
            pl.BlockSpec((N_BRANCH, d), lambda i: (0, 0)),
            pl.BlockSpec((N_BRANCH, c, d), lambda i: (0, 0, 0), pipeline_mode=once),
            pl.BlockSpec((d, d), lambda i: (0, 0), pipeline_mode=once),
            pl.BlockSpec((1, d), lambda i: (0, 0)),
        ],
        out_specs=pl.BlockSpec((tm, d), lambda i: (i, 0)),
        out_shape=jax.ShapeDtypeStruct((n, d), F32),
        compiler_params=_params(("arbitrary",)),
        name="merge",
    )(x, ya, yb, yc, z, z, z, b_gate, wb, wo, pg.reshape(1, d))


def _ffn_kernel(x_ref, g_ref, wu_ref, wv_ref, cw_ref, cb_ref, fc0_ref, wd_ref, pg_ref,
                o_ref, fc1_ref, xn_ref, acc_ref, hal_ref, car_ref, *, nb, T, tf):
    it = pl.program_id(1)
    j = pl.program_id(2)
    nj = pl.num_programs(2)
    hist = FFN_CONV_W - 1

    @pl.when(j == 0)
    def _():
        x = x_ref[...].reshape(nb * T, D_MODEL)
        xn_ref[...] = (x * _rms_scale(x) * g_ref[...]).astype(BF16)
        acc_ref[...] = jnp.zeros_like(acc_ref)

    @pl.when(it == 0)
    def _():
        car_ref[j] = fc0_ref[...]

    xn = xn_ref[...]
    u = jnp.dot(xn, wu_ref[...], preferred_element_type=F32)
    v = jnp.dot(xn, wv_ref[...], preferred_element_type=F32)
    hal_ref[:, HALO - hist:HALO, :] = car_ref[j]
    hal_ref[:, HALO:HALO + T, :] = u.reshape(nb, T, tf)
    cw = cw_ref[...]
    uc = cb_ref[...][None]
    for jj in range(FFN_CONV_W):
        uc = uc + hal_ref[:, HALO - hist + jj:HALO - hist + jj + T, :] * cw[jj:jj + 1][None]
    tail = hal_ref[:, HALO + T - hist:HALO + T, :]
    car_ref[j] = tail
    act = (jax.nn.gelu(uc).reshape(nb * T, tf) * v).astype(BF16)
    acc_ref[...] += jnp.dot(act, wd_ref[...], preferred_element_type=F32)

    @pl.when(j == nj - 1)
    def _():
        y = acc_ref[...]
        out = x_ref[...].reshape(nb * T, D_MODEL) + y * _rms_scale(y) * pg_ref[...]
        o_ref[...] = out.reshape(nb, T, D_MODEL)

    @pl.when((j == nj - 1) & (it == pl.num_programs(1) - 1))
    def _():
        for jj in range(FFN_DIM // tf):
            fc1_ref[:, :, jj * tf:(jj + 1) * tf] = car_ref[jj]


def _ffn(x3, g, w_up, cw, cb, fc0, w_down, pg, nb, T, tf):
    B, L, d = x3.shape
    F = FFN_DIM
    nj = F // tf
    hist = FFN_CONV_W - 1
    return pl.pallas_call(
        functools.partial(_ffn_kernel, nb=nb, T=T, tf=tf),
        grid=(B // nb, L // T, nj),
        in_specs=[
            pl.BlockSpec((nb, T, d), lambda b, t, j: (b, t, 0)),
            pl.BlockSpec((1, d), lambda b, t, j: (0, 0)),
            pl.BlockSpec((d, tf), lambda b, t, j: (0, j)),
            pl.BlockSpec((d, tf), lambda b, t, j: (0, nj + j)),
            pl.BlockSpec((FFN_CONV_W, tf), lambda b, t, j: (0, j)),
            pl.BlockSpec((1, tf), lambda b, t, j: (0, j)),
            pl.BlockSpec((nb, hist, tf), lambda b, t, j: (b, 0, j)),
            pl.BlockSpec((tf, d), lambda b, t, j: (j, 0)),
            pl.BlockSpec((1, d), lambda b, t, j: (0, 0)),
        ],
        out_specs=[
            pl.BlockSpec((nb, T, d), lambda b, t, j: (b, t, 0)),
            pl.BlockSpec((nb, hist, F), lambda b, t, j: (b, 0, 0)),
        ],
        out_shape=[
            jax.ShapeDtypeStruct((B, L, d), F32),
            jax.ShapeDtypeStruct((B, hist, F), F32),
        ],
        scratch_shapes=[
            pltpu.VMEM((nb * T, d), BF16),
            pltpu.VMEM((nb * T, d), F32),
            pltpu.VMEM((nb, HALO + T, tf), F32),
            pltpu.VMEM((nj, nb, hist, tf), F32),
        ],
        compiler_params=_params(("arbitrary", "arbitrary", "arbitrary")),
        name="ffn",
    )(x3, g.reshape(1, d), w_up, w_up, cw, cb.reshape(1, F), fc0, w_down, pg.reshape(1, d))


def _block_diag(w):
    nblk, bi, bj = w.shape
    eye = jnp.eye(nblk, dtype=w.dtype)
    return jnp.einsum("hij,hg->higj", w, eye).reshape(nblk * bi, nblk * bj)


def _reorder_in_cols(w):
    c, hg = BRANCH_WIDTH, HG_COLS * BRANCH_WIDTH
    return jnp.concatenate([w[:, :c], w[:, c + hg:], w[:, c:c + hg]], axis=1)


def _tiles(B, L):
    T = min(L, 256)
    fuse_hg = L % 512 == 0
    return dict(
        fuse_hg=fuse_hg, proj_tm=512 if fuse_hg else min(B * L, 1024), proj_tn=1024,
        rnn_nb=B if B * T <= 512 else 512 // T, rnn_T=T,
        hg_nb=min(B, 4), hg_T=T,
        xa_T=min(L, 512),
        merge_tm=256,
        ffn_nb=B if B * T <= 512 else 512 // T, ffn_T=T, ffn_tf=512,
    )


def _trunk_layer(x, mk, mv, h0, rc0, s0, fc0, w, layer):
    B, L, d = x.shape
    n = B * L
    t = _tiles(B, L)
    x2 = x.reshape(n, d)
    if t["fuse_hg"]:
        z, yb, s1 = _proj_with_hgrn2(x2, w["pre_mix_norm"], w["w_in"], s0, w["hg_lb"], w["hg_norm"],
                                     layer, L, t["proj_tm"], t["proj_tn"])
        z3 = z.reshape(B, L, IN_COLS)
    else:
        z = _norm_matmul(x2, w["pre_mix_norm"], w["w_in"], t["proj_tm"], t["proj_tn"], "in_proj")
        z3 = z.reshape(B, L, IN_COLS)
        yb, s1 = _hgrn2(z3, s0, w["hg_lb"], w["hg_norm"], layer, t["hg_nb"], t["hg_T"])
    ya, h1, rc1 = _rglru(z3, rc0, h0, w["rnn_conv_w"], w["rnn_conv_b"], w["lru_wa"], w["lru_ba"],
                         w["lru_wx"], w["lru_bx"], w["lru_lambda"], t["rnn_nb"], t["rnn_T"])
    yc = _xattn(z3, mk, mv, t["xa_T"])
    c = BRANCH_WIDTH
    x1 = _merge(x2, ya.reshape(n, c), yb.reshape(n, c), yc.reshape(n, c), z, w["b_gate"],
                w["w_branch"], w["w_out"], w["post_mix_norm"], t["merge_tm"])
    xo, fc1 = _ffn(x1.reshape(B, L, d), w["pre_ffn_norm"], w["w_ffn_up"], w["ffn_conv_w"],
                   w["ffn_conv_b"], fc0, w["w_ffn_down"], w["post_ffn_norm"],
                   t["ffn_nb"], t["ffn_T"], t["ffn_tf"])
    return xo, h1, rc1, s1, fc1


def kernel(x_prompt, x_sample, cache_mem_k, cache_mem_v, state_rnn_h, state_rnn_conv, state_hg,
           state_ffn_conv, mem_prompt, pre_mix_norm, w_in, rnn_conv_w, rnn_conv_b, lru_wa, lru_ba,
           lru_wx, lru_bx, lru_lambda, hg_lb, hg_norm, mem_norm, w_mem_kv, w_branch, b_gate, w_out,
           post_mix_norm, pre_ffn_norm, w_ffn_up, ffn_conv_w, ffn_conv_b, w_ffn_down, post_ffn_norm):
    depth = w_in.shape[0]
    Bp = x_prompt.shape[0]
    Bs = x_sample.shape[0]
    xa_w = XA_HEADS * XA_HD
    yp, ys = x_prompt, x_sample
    outs = [[] for _ in range(10)]
    for l in range(depth):
        w = {
            "pre_mix_norm": pre_mix_norm[l], "w_in": _reorder_in_cols(w_in[l]).astype(BF16),
            "rnn_conv_w": rnn_conv_w[l], "rnn_conv_b": rnn_conv_b[l],
            "lru_wa": _block_diag(lru_wa[l]).astype(BF16), "lru_ba": lru_ba[l],
            "lru_wx": _block_diag(lru_wx[l]).astype(BF16), "lru_bx": lru_bx[l],
            "lru_lambda": lru_lambda[l], "hg_lb": hg_lb, "hg_norm": hg_norm[l],
            "w_branch": w_branch[l].astype(BF16), "b_gate": b_gate[l],
            "w_out": w_out[l].astype(BF16), "post_mix_norm": post_mix_norm[l],
            "pre_ffn_norm": pre_ffn_norm[l], "w_ffn_up": w_ffn_up[l].astype(BF16),
            "ffn_conv_w": ffn_conv_w[l], "ffn_conv_b": ffn_conv_b[l],
            "w_ffn_down": w_ffn_down[l].astype(BF16), "post_ffn_norm": post_ffn_norm[l],
        }
        mem2 = mem_prompt.reshape(Bp * MEM_LEN, D_MODEL)
        kv = _norm_matmul(mem2, mem_norm[l], w_mem_kv[l].astype(BF16), Bp * MEM_LEN, 1024, "mem_kv")
        kv = kv.reshape(Bp, MEM_LEN, 2 * xa_w)
        mk_p, mv_p = kv[..., :xa_w], kv[..., xa_w:]
        yp, h_p, rc_p, s_p, fc_p = _trunk_layer(
            yp, mk_p, mv_p, jnp.zeros((Bp, RNN_WIDTH), F32),
            jnp.zeros((Bp, RNN_CONV_W - 1, RNN_WIDTH), F32),
            jnp.zeros((Bp, HG_HEADS, HG_DK, HG_DV), F32),
            jnp.zeros((Bp, FFN_CONV_W - 1, FFN_DIM), F32), w, l)
        ys, h_s, rc_s, s_s, fc_s = _trunk_layer(
            ys, cache_mem_k[l].reshape(Bs, MEM_LEN, xa_w), cache_mem_v[l].reshape(Bs, MEM_LEN, xa_w),
            state_rnn_h[l], state_rnn_conv[l], state_hg[l], state_ffn_conv[l], w, l)
        layer_out = (mk_p.reshape(Bp, MEM_LEN, XA_HEADS, XA_HD), mv_p.reshape(Bp, MEM_LEN, XA_HEADS, XA_HD),
                     h_p, rc_p, s_p, fc_p, h_s, rc_s, s_s, fc_s)
        for acc, val in zip(outs, layer_out):
            acc.append(val)
    return (yp, ys) + tuple(jnp.stack(o) for o in outs)
```

```python
import functools

import jax
import jax.numpy as jnp
from jax import lax
from jax.experimental import pallas as pl
from jax.experimental.pallas import tpu as pltpu

F32 = jnp.float32
BF16 = jnp.bfloat16

D_MODEL = 2048
RNN_WIDTH = 1024
RNN_BLOCKS = 16
RNN_CONV_W = 4
LRU_C = 8.0
HG_HEADS = 8
HG_DK = 128
HG_DV = 128
HG_CHUNK = 64
HG_SUB = 16
HG_SKEW = 3
MEM_LEN = 256
XA_HEADS = 4
XA_HD = 256
BRANCH_WIDTH = 1024
N_BRANCH = 3
FFN_DIM = 5632
FFN_CONV_W = 3
EPS = 1e-6
LOG2E = 1.4426950408889634
IN_COLS = 6 * BRANCH_WIDTH + N_BRANCH * D_MODEL

COL_RNN, COL_HQ, COL_HF, COL_HI, COL_HO, COL_XQ = range(6)
GATE_COL0 = 3

HALO = 8
VMEM_LIMIT = 56 * 1024 * 1024


def _params(sem):
    return pltpu.CompilerParams(dimension_semantics=sem, vmem_limit_bytes=VMEM_LIMIT)


def _rms_scale(x):
    return lax.rsqrt(jnp.mean(x * x, axis=-1, keepdims=True) + EPS)


_sigmoid = jax.nn.sigmoid


def _norm_matmul_kernel(x_ref, g_ref, w_ref, *rest, emit_w, aliased):
    rest = rest[1:] if aliased else rest
    o_ref, xn_ref = rest[0], rest[-1]

    @pl.when(pl.program_id(1) == 0)
    def _():
        x = x_ref[...]
        xn_ref[...] = (x * _rms_scale(x) * g_ref[...]).astype(BF16)

    w = w_ref[...].astype(BF16)
    if emit_w:
        rest[1][...] = w
    o_ref[...] = jnp.dot(xn_ref[...], w, preferred_element_type=F32)


def _norm_matmul(x, g, w, tm, tn, name, tiles=None, emit_w=False, into=None):
    n, d = x.shape
    c = w.shape[1]
    t0, t1 = tiles if tiles is not None else (0, n // tm)
    assert not emit_w or t1 - t0 == 1
    in_specs = [
        pl.BlockSpec((tm, d), lambda i, j: (i + t0, 0)),
        pl.BlockSpec((1, d), lambda i, j: (0, 0)),
        pl.BlockSpec((d, tn), lambda i, j: (0, j)),
    ]
    args = [x, g.reshape(1, d), w]
    out_specs = [pl.BlockSpec((tm, tn), lambda i, j: (i + t0, j))]
    out_shape = [jax.ShapeDtypeStruct((n, c), F32)]
    if into is not None:
        in_specs.append(pl.BlockSpec(memory_space=pl.ANY))
        args.append(into)
    if emit_w:
        out_specs.append(pl.BlockSpec((d, tn), lambda i, j: (0, j)))
        out_shape.append(jax.ShapeDtypeStruct((d, c), BF16))
    res = pl.pallas_call(
        functools.partial(_norm_matmul_kernel, emit_w=emit_w, aliased=into is not None),
        grid=(t1 - t0, c // tn),
        in_specs=in_specs,
        out_specs=out_specs,
        out_shape=out_shape,
        scratch_shapes=[pltpu.VMEM((tm, d), BF16)],
        input_output_aliases={3: 0} if into is not None else {},
        compiler_params=_params(("arbitrary", "arbitrary")),
        name=name,
    )(*args)
    return res if emit_w else res[0]


def _rglru_kernel(z_ref, rc0_ref, h0_ref, cw_ref, cb_ref, wa_ref, ba_ref, wx_ref, bx_ref, lam_ref,
                  ya_ref, h1_ref, rc1_ref, xp_ref, a_ref, b_ref, hs_ref, hc_ref, *, nb, T):
    it = pl.program_id(1)
    nt = pl.num_programs(1)
    hist = RNN_CONV_W - 1

    @pl.when(it == 0)
    def _():
        xp_ref[:, HALO - hist:HALO, :] = rc0_ref[...]
        hc_ref[...] = h0_ref[...]

    xp_ref[:, HALO:HALO + T, :] = z_ref[...]
    cw = cw_ref[...]
    xr = cb_ref[...][None]
    for j in range(RNN_CONV_W):
        xr = xr + xp_ref[:, HALO - hist + j:HALO - hist + j + T, :] * cw[j:j + 1][None]
    xp_ref[:, HALO - hist:HALO, :] = xp_ref[:, HALO + T - hist:HALO + T, :]

    xr2 = xr.reshape(nb * T, RNN_WIDTH)
    xb = xr2.astype(BF16)
    r = _sigmoid(jnp.dot(xb, wa_ref[...], preferred_element_type=F32) + ba_ref[...])
    ig = _sigmoid(jnp.dot(xb, wx_ref[...], preferred_element_type=F32) + bx_ref[...])
    nl = -lam_ref[...]
    softplus = jnp.maximum(nl, 0.0) + jnp.log1p(jnp.exp(-jnp.abs(nl)))
    log_a = -LRU_C * r * softplus
    a = jnp.exp(log_a)
    a_ref[...] = a.reshape(nb, T, RNN_WIDTH)
    one_minus_a2 = -jnp.tanh(log_a) * (a * a + 1.0)
    b_ref[...] = (jnp.sqrt(one_minus_a2) * (ig * xr2)).reshape(nb, T, RNN_WIDTH)

    hs = [hc_ref[bi:bi + 1, :] for bi in range(nb)]
    for t in range(T):
        for bi in range(nb):
            hs[bi] = a_ref[bi, t:t + 1, :] * hs[bi] + b_ref[bi, t:t + 1, :]
            hs_ref[bi, t:t + 1, :] = hs[bi]
    for bi in range(nb):
        hc_ref[bi:bi + 1, :] = hs[bi]
    ya_ref[...] = hs_ref[...].astype(BF16)

    @pl.when(it == nt - 1)
    def _():
        h1_ref[...] = hc_ref[...]
        rc1_ref[...] = xp_ref[:, HALO - hist:HALO, :]


def _rglru(z3, rc0, h0, cw, cb, wa, ba, wx, bx, lam, nb, T):
    B, L, _ = z3.shape
    C = RNN_WIDTH
    hist = RNN_CONV_W - 1
    vec = lambda: pl.BlockSpec((1, C), lambda b, t: (0, 0))
    mat = lambda: pl.BlockSpec((C, C), lambda b, t: (0, 0))
    return pl.pallas_call(
        functools.partial(_rglru_kernel, nb=nb, T=T),
        grid=(B // nb, L // T),
        in_specs=[
            pl.BlockSpec((nb, T, C), lambda b, t: (b, t, COL_RNN)),
            pl.BlockSpec((nb, hist, C), lambda b, t: (b, 0, 0)),
            pl.BlockSpec((nb, C), lambda b, t: (b, 0)),
            pl.BlockSpec((RNN_CONV_W, C), lambda b, t: (0, 0)),
            vec(), mat(), vec(), mat(), vec(), vec(),
        ],
        out_specs=[
            pl.BlockSpec((nb, T, C), lambda b, t: (b, t, 0)),
            pl.BlockSpec((nb, C), lambda b, t: (b, 0)),
            pl.BlockSpec((nb, hist, C), lambda b, t: (b, 0, 0)),
        ],
        out_shape=[
            jax.ShapeDtypeStruct((B, L, C), BF16),
            jax.ShapeDtypeStruct((B, C), F32),
            jax.ShapeDtypeStruct((B, hist, C), F32),
        ],
        scratch_shapes=[
            pltpu.VMEM((nb, HALO + T, C), F32),
            pltpu.VMEM((nb, T, C), F32),
            pltpu.VMEM((nb, T, C), F32),
            pltpu.VMEM((nb, T, C), F32),
            pltpu.VMEM((nb, C), F32),
        ],
        compiler_params=_params(("arbitrary", "arbitrary")),
        name="rglru",
    )(z3, rc0, h0, cw, cb.reshape(1, C), wa, ba.reshape(1, C), wx, bx.reshape(1, C), lam.reshape(1, C))


def _dot_nt(a, b):
    return lax.dot_general(a, b, (((1,), (1,)), ((), ())), preferred_element_type=F32)


def _split3(x):
    hi = x.astype(BF16)
    r = x - hi.astype(F32)
    mid = r.astype(BF16)
    lo = (r - mid.astype(F32)).astype(BF16)
    return jnp.concatenate([hi, mid, lo], axis=0)


def _hgrn2_scores(qh, bc, ck, ck_row, st, cs):
    o = _dot_nt((qh * jnp.exp2(bc)).astype(BF16), st.astype(BF16))

    nblk = cs // HG_SUB
    a_rows = [None] * nblk
    half = cs // 2
    while half >= HG_SUB:
        npair = cs // (2 * half)
        q_parts, k_parts = [], []
        for p in range(npair):
            lo, mid, hi = 2 * half * p, 2 * half * p + half, 2 * half * (p + 1)
            br = bc[mid - 1:mid]
            q_parts.append(qh[mid:hi] * jnp.exp2(bc[mid:hi] - br))
            k_parts += [jnp.exp2(br - ck[lo:mid]), jnp.zeros((half, HG_DK), F32)]
        off = _dot_nt(jnp.concatenate(q_parts, axis=0).astype(BF16),
                      jnp.concatenate(k_parts, axis=0).astype(BF16))
        if npair > 1:
            lg = half.bit_length() - 1
            rowp = lax.shift_right_logical(lax.broadcasted_iota(jnp.int32, off.shape, 0), lg)
            colp = lax.shift_right_logical(lax.broadcasted_iota(jnp.int32, off.shape, 1), lg + 1)
            off = jnp.where(rowp == colp, off, 0.0)
        for p in range(npair):
            for sub in range(half // HG_SUB):
                blk = (2 * half * p + half) // HG_SUB + sub
                piece = off[p * half + sub * HG_SUB:p * half + (sub + 1) * HG_SUB]
                a_rows[blk] = piece if a_rows[blk] is None else a_rows[blk] + piece
        half //= 2

    hs = HG_SUB // 2
    ones = jnp.ones((HG_DK, HG_DK), BF16)
    zero = jnp.zeros((hs, HG_DK), F32)
    lane_sums = []
    for blk in range(nblk):
        r0 = blk * HG_SUB
        b_lo, b_hi = bc[r0:r0 + hs], bc[r0 + hs:r0 + HG_SUB]
        q_lo, q_hi = qh[r0:r0 + hs], qh[r0 + hs:r0 + HG_SUB]
        pair = []
        for s in range(HG_SUB):
            cks = ck_row(r0 + s)
            d_lo = q_lo * jnp.exp2(b_lo - cks) if s < hs else zero
            d_hi = q_hi * jnp.exp2(b_hi - cks)
            pair.append(jnp.concatenate([d_lo, d_hi], axis=0).astype(BF16))
        lane_sums.append(jnp.dot(jnp.concatenate(pair, axis=0), ones, preferred_element_type=F32))
    return o, a_rows, lane_sums


def _hgrn2_output(scores, vh, ogh, gnh, bl, ck, st, cs):
    o, a_rows, lane_sums = scores
    hs = HG_SUB // 2
    lane = lax.broadcasted_iota(jnp.int32, (hs, HG_DK), 1)
    tloc = lax.broadcasted_iota(jnp.int32, (hs, HG_DK), 0)
    lo_mask = [(lane == s) & (tloc >= s) for s in range(hs)]
    hi_mask = [lane == s for s in range(hs)] + [(lane == s) & (tloc >= s - hs) for s in range(hs, HG_SUB)]
    zero = jnp.zeros((hs, HG_DK), F32)
    a_rows = list(a_rows)
    for blk, lane_sum in enumerate(lane_sums):
        a_lo, a_hi = zero, zero
        for s in range(HG_SUB):
            if s < hs:
                a_lo = jnp.where(lo_mask[s], lane_sum[s * HG_SUB:s * HG_SUB + hs], a_lo)
            a_hi = jnp.where(hi_mask[s], lane_sum[s * HG_SUB + hs:(s + 1) * HG_SUB], a_hi)
        a_d = jnp.concatenate([a_lo, a_hi], axis=0)
        if blk:
            a_d = pltpu.roll(a_d, blk * HG_SUB, 1)
        a_d = a_d[:, :cs]
        a_rows[blk] = a_d if a_rows[blk] is None else a_rows[blk] + a_d
    a = jnp.concatenate(a_rows, axis=0)
    o = o + jnp.dot(a.astype(BF16), vh.astype(BF16), preferred_element_type=F32)
    kd = jnp.exp2(bl - ck).astype(BF16)
    st_new = st * jnp.exp2(bl) + jnp.dot(vh.T.astype(BF16), kd, preferred_element_type=F32)
    y = o * _rms_scale(o) * gnh * _sigmoid(ogh)
    return y, st_new


def _hgrn2_kernel(q_ref, f_ref, v_ref, og_ref, s0_ref, lbp_ref, gn_ref, yb_ref, s1_ref,
                  lb_ref, bc_ref, ck_ref, *st_refs, nb, T, cs, layer):
    it = pl.program_id(1)
    nt = pl.num_programs(1)
    row = lax.broadcasted_iota(jnp.int32, (cs, 3 * cs), 0)
    col = lax.broadcasted_iota(jnp.int32, (cs, 3 * cs), 1)
    tri3 = (row >= (col & (cs - 1))).astype(BF16)

    @pl.when(it == 0)
    def _():
        p = lbp_ref[...]
        e = jnp.exp(p - jnp.max(p, axis=0, keepdims=True))
        sm = e / jnp.sum(e, axis=0, keepdims=True)
        lb_ref[...] = jnp.sum(sm[:layer + 1], axis=0, keepdims=True)

        def init(b, c):
            for h in range(HG_HEADS):
                st_refs[h][b] = s0_ref[b, h].T
            return c
        lax.fori_loop(0, nb, init, 0)

    nch = T // cs

    def chunk(idx, c):
        b = idx // nch
        rows = pl.ds(pl.multiple_of((idx % nch) * cs, cs), cs)
        lb = lb_ref[...]
        f = lb + (1.0 - lb) * _sigmoid(f_ref[b, rows, :])
        bc = jnp.dot(tri3, _split3(jnp.log(f) * LOG2E), preferred_element_type=F32)
        bc_ref[...] = bc
        ck_ref[...] = bc - jnp.log(1.0 - f) * LOG2E

        def scores(h):
            hl = slice(h * HG_DK, (h + 1) * HG_DK)
            return _hgrn2_scores(q_ref[b, rows, hl], bc_ref[:, hl], ck_ref[:, hl],
                                 lambda r: ck_ref[r:r + 1, hl], st_refs[h][b], cs)

        def output(h, sc):
            hl = slice(h * HG_DK, (h + 1) * HG_DK)
            y, st_new = _hgrn2_output(sc, v_ref[b, rows, hl], og_ref[b, rows, hl], gn_ref[...],
                                      bc_ref[cs - 1:cs, hl], ck_ref[:, hl], st_refs[h][b], cs)
            yb_ref[b, rows, hl] = y.astype(BF16)
            st_refs[h][b] = st_new

        pending = {}
        for h in range(HG_HEADS + HG_SKEW):
            if h < HG_HEADS:
                pending[h] = scores(h)
            if h >= HG_SKEW:
                output(h - HG_SKEW, pending.pop(h - HG_SKEW))
        return c
    lax.fori_loop(0, nb * nch, chunk, 0)

    @pl.when(it == nt - 1)
    def _():
        def fin(b, c):
            for h in range(HG_HEADS):
                s1_ref[b, h] = st_refs[h][b].T
            return c
        lax.fori_loop(0, nb, fin, 0)


def _hgrn2(z3, s0, hg_lb, gn, layer, nb, T):
    B, L, _ = z3.shape
    C = HG_HEADS * HG_DK
    cs = min(HG_CHUNK, L)
    zcol = lambda col: pl.BlockSpec((nb, T, C), lambda b, t: (b, t, col))
    st_spec = lambda: pl.BlockSpec((nb, HG_HEADS, HG_DK, HG_DV), lambda b, t: (b, 0, 0, 0))
    nl = hg_lb.shape[0]
    return pl.pallas_call(
        functools.partial(_hgrn2_kernel, nb=nb, T=T, cs=cs, layer=layer),
        grid=(B // nb, L // T),
        in_specs=[
            zcol(COL_HQ), zcol(COL_HF), zcol(COL_HI), zcol(COL_HO),
            st_spec(),
            pl.BlockSpec((nl, C), lambda b, t: (0, 0)),
            pl.BlockSpec((1, HG_DV), lambda b, t: (0, 0)),
        ],
        out_specs=[pl.BlockSpec((nb, T, C), lambda b, t: (b, t, 0)), st_spec()],
        out_shape=[
            jax.ShapeDtypeStruct((B, L, C), BF16),
            jax.ShapeDtypeStruct((B, HG_HEADS, HG_DK, HG_DV), F32),
        ],
        scratch_shapes=[
            pltpu.VMEM((1, C), F32),
            pltpu.VMEM((cs, C), F32),
            pltpu.VMEM((cs, C), F32),
        ] + [pltpu.VMEM((nb, HG_DV, HG_DK), F32) for _ in range(HG_HEADS)],
        compiler_params=_params(("arbitrary", "arbitrary")),
        name="hgrn2",
    )(z3, z3, z3, z3, s0, hg_lb, gn.reshape(1, HG_DV))


def _xattn_kernel(q_ref, k_ref, v_ref, o_ref):
    for h in range(XA_HEADS):
        hl = slice(h * XA_HD, (h + 1) * XA_HD)
        s = _dot_nt(q_ref[0, :, hl].astype(BF16), k_ref[0, :, hl].astype(BF16)) * (XA_HD ** -0.5)
        e = jnp.exp(s - jnp.max(s, axis=-1, keepdims=True))
        p = e / jnp.sum(e, axis=-1, keepdims=True)
        o = jnp.dot(p.astype(BF16), v_ref[0, :, hl].astype(BF16), preferred_element_type=F32)
        o_ref[0, :, hl] = o.astype(BF16)


def _xattn(z3, mk, mv, T):
    B, L, _ = z3.shape
    C = XA_HEADS * XA_HD
    mem = lambda: pl.BlockSpec((1, MEM_LEN, C), lambda b, t: (b, 0, 0))
    return pl.pallas_call(
        _xattn_kernel,
        grid=(B, L // T),
        in_specs=[pl.BlockSpec((1, T, C), lambda b, t: (b, t, COL_XQ)), mem(), mem()],
        out_specs=pl.BlockSpec((1, T, C), lambda b, t: (b, t, 0)),
        out_shape=jax.ShapeDtypeStruct((B, L, C), BF16),
        compiler_params=_params(("arbitrary", "arbitrary")),
        name="xattn",
    )(z3, mk, mv)


def _xattn_cached_kernel(q_ref, k_ref, v_ref, o_ref):
    for h in range(XA_HEADS):
        hl = slice(h * XA_HD, (h + 1) * XA_HD)
        s = _dot_nt(q_ref[0, :, hl].astype(BF16), k_ref[:, h, :].astype(BF16)) * (XA_HD ** -0.5)
        e = jnp.exp(s - jnp.max(s, axis=-1, keepdims=True))
        p = e / jnp.sum(e, axis=-1, keepdims=True)
        o = jnp.dot(p.astype(BF16), v_ref[:, h, :].astype(BF16), preferred_element_type=F32)
        o_ref[0, :, hl] = o.astype(BF16)


def _xattn_cached(z3, cache_k, cache_v, layer):
    B, L, _ = z3.shape
    C = XA_HEADS * XA_HD
    mem = lambda: pl.BlockSpec((None, None, MEM_LEN, XA_HEADS, XA_HD), lambda b: (layer, b, 0, 0, 0))
    return pl.pallas_call(
        _xattn_cached_kernel,
        grid=(B,),
        in_specs=[pl.BlockSpec((1, L, C), lambda b: (b, 0, COL_XQ)), mem(), mem()],
        out_specs=pl.BlockSpec((1, L, C), lambda b: (b, 0, 0)),
        out_shape=jax.ShapeDtypeStruct((B, L, C), BF16),
        compiler_params=_params(("arbitrary",)),
        name="xattn_cached",
    )(z3, cache_k, cache_v)


def _merge_kernel(x_ref, ya_ref, yb_ref, yc_ref, g0_ref, g1_ref, g2_ref, bg_ref, wb_ref, wo_ref,
                  pg_ref, o_ref):
    m = jnp.zeros(o_ref.shape, F32)
    for nb, (y_ref, g_ref) in enumerate(((ya_ref, g0_ref), (yb_ref, g1_ref), (yc_ref, g2_ref))):
        gate = _sigmoid(g_ref[...] + bg_ref[nb:nb + 1, :])
        m = m + gate * jnp.dot(y_ref[...], wb_ref[nb], preferred_element_type=F32)
    y = jnp.dot(m.astype(BF16), wo_ref[...], preferred_element_type=F32)
    o_ref[...] = x_ref[...] + y * _rms_scale(y) * pg_ref[...]


def _merge(x, ya, yb, yc, z, b_gate, wb, wo, pg, tm):
    n, d = x.shape
    c = BRANCH_WIDTH
    once = pl.Buffered(1)
    branch = lambda: pl.BlockSpec((tm, c), lambda i: (i, 0))
    gate = lambda nb: pl.BlockSpec((tm, d), lambda i: (i, GATE_COL0 + nb))
    return pl.pallas_call(
        _merge_kernel,
        grid=(n // tm,),
        in_specs=[
            pl.BlockSpec((tm, d), lambda i: (i, 0)),
            branch(), branch(), branch(),
            gate(0), gate(1), gate(2),
            pl.BlockSpec((N_BRANCH, d), lambda i: (0, 0)),
            pl.BlockSpec((N_BRANCH, c, d), lambda i: (0, 0, 0), pipeline_mode=once),
            pl.BlockSpec((d, d), lambda i: (0, 0), pipeline_mode=once),
            pl.BlockSpec((1, d), lambda i: (0, 0)),
        ],
        out_specs=pl.BlockSpec((tm, d), lambda i: (i, 0)),
        out_shape=jax.ShapeDtypeStruct((n, d), F32),
        compiler_params=_params(("arbitrary",)),
        name="merge",
    )(x, ya, yb, yc, z, z, z, b_gate, wb, wo, pg.reshape(1, d))


def _ffn_kernel(x_ref, g_ref, wu_ref, wv_ref, cw_ref, cb_ref, fc0_ref, wd_ref, pg_ref, *rest,
                nb, T, tf, emit_w, aliased):
    rest = rest[1:] if aliased else rest
    o_ref, fc1_ref = rest[:2]
    xn_ref, acc_ref, hal_ref, car_ref = rest[-4:]
    it = pl.program_id(1)
    j = pl.program_id(2)
    nj = pl.num_programs(2)
    hist = FFN_CONV_W - 1

    @pl.when(j == 0)
    def _():
        x = x_ref[...].reshape(nb * T, D_MODEL)
        xn_ref[...] = (x * _rms_scale(x) * g_ref[...]).astype(BF16)
        acc_ref[...] = jnp.zeros_like(acc_ref)

    @pl.when(it == 0)
    def _():
        car_ref[j] = fc0_ref[...]

    wu, wv, wd = (r[...].astype(BF16) for r in (wu_ref, wv_ref, wd_ref))
    if emit_w:
        for out_ref, wb in zip(rest[2:5], (wu, wv, wd)):
            out_ref[...] = wb
    xn = xn_ref[...]
    u = jnp.dot(xn, wu, preferred_element_type=F32)
    v = jnp.dot(xn, wv, preferred_element_type=F32)
    hal_ref[:, HALO - hist:HALO, :] = car_ref[j]
    hal_ref[:, HALO:HALO + T, :] = u.reshape(nb, T, tf)
    cw = cw_ref[...]
    uc = cb_ref[...][None]
    for jj in range(FFN_CONV_W):
        uc = uc + hal_ref[:, HALO - hist + jj:HALO - hist + jj + T, :] * cw[jj:jj + 1][None]
    tail = hal_ref[:, HALO + T - hist:HALO + T, :]
    car_ref[j] = tail
    act = (jax.nn.gelu(uc).reshape(nb * T, tf) * v).astype(BF16)
    acc_ref[...] += jnp.dot(act, wd, preferred_element_type=F32)

    @pl.when(j == nj - 1)
    def _():
        y = acc_ref[...]
        out = x_ref[...].reshape(nb * T, D_MODEL) + y * _rms_scale(y) * pg_ref[...]
        o_ref[...] = out.reshape(nb, T, D_MODEL)

    @pl.when((j == nj - 1) & (it == pl.num_programs(1) - 1))
    def _():
        for jj in range(FFN_DIM // tf):
            fc1_ref[:, :, jj * tf:(jj + 1) * tf] = car_ref[jj]


def _ffn(x3, g, wu, wv, v_off, cw, cb, fc0, w_down, pg, nb, T, tf, tiles=None, emit_w=False, into=None):
    B, L, d = x3.shape
    F = FFN_DIM
    nj = F // tf
    hist = FFN_CONV_W - 1
    t0, t1 = tiles if tiles is not None else (0, L // T)
    assert not emit_w or (t1 - t0 == 1 and B == nb)
    rows = lambda: pl.BlockSpec((nb, T, d), lambda b, t, j: (b, t + t0, 0))
    in_specs = [
        rows(),
        pl.BlockSpec((1, d), lambda b, t, j: (0, 0)),
        pl.BlockSpec((d, tf), lambda b, t, j: (0, j)),
        pl.BlockSpec((d, tf), lambda b, t, j: (0, v_off // tf + j)),
        pl.BlockSpec((FFN_CONV_W, tf), lambda b, t, j: (0, j)),
        pl.BlockSpec((1, tf), lambda b, t, j: (0, j)),
        pl.BlockSpec((nb, hist, tf), lambda b, t, j: (b, 0, j)),
        pl.BlockSpec((tf, d), lambda b, t, j: (j, 0)),
        pl.BlockSpec((1, d), lambda b, t, j: (0, 0)),
    ]
    args = [x3, g.reshape(1, d), wu, wv, cw, cb.reshape(1, F), fc0, w_down, pg.reshape(1, d)]
    out_specs = [rows(), pl.BlockSpec((nb, hist, F), lambda b, t, j: (b, 0, 0))]
    out_shape = [jax.ShapeDtypeStruct((B, L, d), F32), jax.ShapeDtypeStruct((B, hist, F), F32)]
    if into is not None:
        in_specs.append(pl.BlockSpec(memory_space=pl.ANY))
        args.append(into)
    if emit_w:
        out_specs += [pl.BlockSpec((d, tf), lambda b, t, j: (0, j)),
                      pl.BlockSpec((d, tf), lambda b, t, j: (0, j)),
                      pl.BlockSpec((tf, d), lambda b, t, j: (j, 0))]
        out_shape += [jax.ShapeDtypeStruct((d, F), BF16), jax.ShapeDtypeStruct((d, F), BF16),
                      jax.ShapeDtypeStruct((F, d), BF16)]
    res = pl.pallas_call(
        functools.partial(_ffn_kernel, nb=nb, T=T, tf=tf, emit_w=emit_w, aliased=into is not None),
        grid=(B // nb, t1 - t0, nj),
        in_specs=in_specs,
        out_specs=out_specs,
        out_shape=out_shape,
        scratch_shapes=[
            pltpu.VMEM((nb * T, d), BF16),
            pltpu.VMEM((nb * T, d), F32),
            pltpu.VMEM((nb, HALO + T, tf), F32),
            pltpu.VMEM((nj, nb, hist, tf), F32),
        ],
        input_output_aliases={9: 0} if into is not None else {},
        compiler_params=_params(("arbitrary", "arbitrary", "arbitrary")),
        name="ffn",
    )(*args)
    return res


def _block_diag(w):
    nblk, bi, bj = w.shape
    eye = jnp.eye(nblk, dtype=w.dtype)
    return jnp.einsum("hij,hg->higj", w, eye).reshape(nblk * bi, nblk * bj)


def _tiles(B, L):
    T = min(L, 256)
    return dict(
        proj_tm=min(B * L, 1024), proj_tn=1024,
        rnn_nb=B if B * T <= 512 else 512 // T, rnn_T=T,
        hg_nb=min(B, 4), hg_T=T,
        xa_T=min(L, 512),
        merge_tm=256,
        ffn_nb=B if B * T <= 512 else 512 // T, ffn_T=T, ffn_tf=512,
    )


def _trunk_layer(x, mk, mv, h0, rc0, s0, fc0, w, layer):
    B, L, d = x.shape
    n = B * L
    t = _tiles(B, L)
    x2 = x.reshape(n, d)
    made = {}
    tm, tn = t["proj_tm"], t["proj_tn"]
    if w["w_in"].dtype == F32:
        z, made["w_in"] = _norm_matmul(x2, w["pre_mix_norm"], w["w_in"], tm, tn // 2, "in_proj",
                                       tiles=(0, 1), emit_w=True)
        if n > tm:
            z = _norm_matmul(x2, w["pre_mix_norm"], made["w_in"], tm, tn, "in_proj",
                             tiles=(1, n // tm), into=z)
    else:
        z = _norm_matmul(x2, w["pre_mix_norm"], w["w_in"], tm, tn, "in_proj")
    z3 = z.reshape(B, L, IN_COLS)
    ya, h1, rc1 = _rglru(z3, rc0, h0, w["rnn_conv_w"], w["rnn_conv_b"], w["lru_wa"], w["lru_ba"],
                         w["lru_wx"], w["lru_bx"], w["lru_lambda"], t["rnn_nb"], t["rnn_T"])
    yb, s1 = _hgrn2(z3, s0, w["hg_lb"], w["hg_norm"], layer, t["hg_nb"], t["hg_T"])
    yc = _xattn(z3, mk, mv, t["xa_T"]) if mv is not None else _xattn_cached(z3, *mk, layer)
    c = BRANCH_WIDTH
    x1 = _merge(x2, ya.reshape(n, c), yb.reshape(n, c), yc.reshape(n, c), z, w["b_gate"],
                w["w_branch"], w["w_out"], w["post_mix_norm"], t["merge_tm"]).reshape(B, L, d)
    nb, T, tf = t["ffn_nb"], t["ffn_T"], t["ffn_tf"]
    ffn = functools.partial(_ffn, x1, w["pre_ffn_norm"], cw=w["ffn_conv_w"], cb=w["ffn_conv_b"],
                            pg=w["post_ffn_norm"], nb=nb, T=T)
    if "w_ffn_up" in w:
        xo, fc1, made["w_ffn_u"], made["w_ffn_v"], made["w_ffn_down"] = ffn(
            wu=w["w_ffn_up"], wv=w["w_ffn_up"], v_off=FFN_DIM, fc0=fc0, w_down=w["w_ffn_down"],
            tf=tf // 2, tiles=(0, 1), emit_w=True)
        if L > T:
            xo, fc1 = ffn(wu=made["w_ffn_u"], wv=made["w_ffn_v"], v_off=0, fc0=fc1,
                          w_down=made["w_ffn_down"], tf=tf, tiles=(1, L // T), into=xo)
    else:
        xo, fc1 = ffn(wu=w["w_ffn_u"], wv=w["w_ffn_v"], v_off=0, fc0=fc0, w_down=w["w_ffn_down"], tf=tf)
    return xo, h1, rc1, s1, fc1, made


def kernel(x_prompt, x_sample, cache_mem_k, cache_mem_v, state_rnn_h, state_rnn_conv, state_hg,
           state_ffn_conv, mem_prompt, pre_mix_norm, w_in, rnn_conv_w, rnn_conv_b, lru_wa, lru_ba,
           lru_wx, lru_bx, lru_lambda, hg_lb, hg_norm, mem_norm, w_mem_kv, w_branch, b_gate, w_out,
           post_mix_norm, pre_ffn_norm, w_ffn_up, ffn_conv_w, ffn_conv_b, w_ffn_down, post_ffn_norm):
    depth = w_in.shape[0]
    Bp = x_prompt.shape[0]
    Bs = x_sample.shape[0]
    xa_w = XA_HEADS * XA_HD
    yp, ys = x_prompt, x_sample
    outs = [[] for _ in range(10)]
    for l in range(depth):
        w = {
            "pre_mix_norm": pre_mix_norm[l], "w_in": w_in[l],
            "rnn_conv_w": rnn_conv_w[l], "rnn_conv_b": rnn_conv_b[l],
            "lru_wa": _block_diag(lru_wa[l]).astype(BF16), "lru_ba": lru_ba[l],
            "lru_wx": _block_diag(lru_wx[l]).astype(BF16), "lru_bx": lru_bx[l],
            "lru_lambda": lru_lambda[l], "hg_lb": hg_lb, "hg_norm": hg_norm[l],
            "w_branch": w_branch[l].astype(BF16), "b_gate": b_gate[l],
            "w_out": w_out[l].astype(BF16), "post_mix_norm": post_mix_norm[l],
            "pre_ffn_norm": pre_ffn_norm[l], "w_ffn_up": w_ffn_up[l],
            "ffn_conv_w": ffn_conv_w[l], "ffn_conv_b": ffn_conv_b[l],
            "w_ffn_down": w_ffn_down[l], "post_ffn_norm": post_ffn_norm[l],
        }
        mem2 = mem_prompt.reshape(Bp * MEM_LEN, D_MODEL)
        kv = _norm_matmul(mem2, mem_norm[l], w_mem_kv[l], Bp * MEM_LEN, 1024, "mem_kv")
        kv = kv.reshape(Bp, MEM_LEN, 2 * xa_w)
        mk_p, mv_p = kv[..., :xa_w], kv[..., xa_w:]
        yp, h_p, rc_p, s_p, fc_p, made = _trunk_layer(
            yp, mk_p, mv_p, jnp.zeros((Bp, RNN_WIDTH), F32),
            jnp.zeros((Bp, RNN_CONV_W - 1, RNN_WIDTH), F32),
            jnp.zeros((Bp, HG_HEADS, HG_DK, HG_DV), F32),
            jnp.zeros((Bp, FFN_CONV_W - 1, FFN_DIM), F32), w, l)
        w = {k: v for k, v in w.items() if k != "w_ffn_up"} | made
        ys, h_s, rc_s, s_s, fc_s, _ = _trunk_layer(
            ys, (cache_mem_k, cache_mem_v), None,
            state_rnn_h[l], state_rnn_conv[l], state_hg[l], state_ffn_conv[l], w, l)
        layer_out = (mk_p.reshape(Bp, MEM_LEN, XA_HEADS, XA_HD), mv_p.reshape(Bp, MEM_LEN, XA_HEADS, XA_HD),
                     h_p, rc_p, s_p, fc_p, h_s, rc_s, s_s, fc_s)
        for acc, val in zip(outs, layer_out):
            acc.append(val)
    return (yp, ys) + tuple(jnp.stack(o) for o in outs)
```

```python
import functools

import jax
import jax.numpy as jnp
from jax import lax
from jax.experimental import pallas as pl
from jax.experimental.pallas import tpu as pltpu

F32 = jnp.float32
BF16 = jnp.bfloat16

D_MODEL = 2048
RNN_WIDTH = 1024
RNN_BLOCKS = 16
RNN_CONV_W = 4
LRU_C = 8.0
HG_HEADS = 8
HG_DK = 128
HG_DV = 128
HG_CHUNK = 128
HG_SUB_MAX = 16
SUBLANES = 8
HG_SKEW = 8
MEM_LEN = 256
XA_HEADS = 4
XA_HD = 256
BRANCH_WIDTH = 1024
N_BRANCH = 3
FFN_DIM = 5632
FFN_CONV_W = 3
EPS = 1e-6
LOG2E = 1.4426950408889634
IN_COLS = 6 * BRANCH_WIDTH + N_BRANCH * D_MODEL

COL_RNN, COL_HQ, COL_HF, COL_HI, COL_HO, COL_XQ = range(6)
GATE_COL0 = 3

HALO = 8
VMEM_LIMIT = 56 * 1024 * 1024


def _params(sem):
    return pltpu.CompilerParams(dimension_semantics=sem, vmem_limit_bytes=VMEM_LIMIT)


def _rms_scale(x):
    return lax.rsqrt(jnp.mean(x * x, axis=-1, keepdims=True) + EPS)


_sigmoid = jax.nn.sigmoid


def _norm_matmul_kernel(x_ref, g_ref, w_ref, *rest, emit_w, aliased):
    rest = rest[1:] if aliased else rest
    o_ref, xn_ref = rest[0], rest[-1]

    @pl.when(pl.program_id(1) == 0)
    def _():
        x = x_ref[...]
        xn_ref[...] = (x * _rms_scale(x) * g_ref[...]).astype(BF16)

    w = w_ref[...].astype(BF16)
    if emit_w:
        rest[1][...] = w
    o_ref[...] = jnp.dot(xn_ref[...], w, preferred_element_type=F32)


def _norm_matmul(x, g, w, tm, tn, name, tiles=None, emit_w=False, into=None):
    n, d = x.shape
    c = w.shape[1]
    t0, t1 = tiles if tiles is not None else (0, n // tm)
    assert not emit_w or t1 - t0 == 1
    in_specs = [
        pl.BlockSpec((tm, d), lambda i, j: (i + t0, 0)),
        pl.BlockSpec((1, d), lambda i, j: (0, 0)),
        pl.BlockSpec((d, tn), lambda i, j: (0, j)),
    ]
    args = [x, g.reshape(1, d), w]
    out_specs = [pl.BlockSpec((tm, tn), lambda i, j: (i + t0, j))]
    out_shape = [jax.ShapeDtypeStruct((n, c), F32)]
    if into is not None:
        in_specs.append(pl.BlockSpec(memory_space=pl.ANY))
        args.append(into)
    if emit_w:
        out_specs.append(pl.BlockSpec((d, tn), lambda i, j: (0, j)))
        out_shape.append(jax.ShapeDtypeStruct((d, c), BF16))
    res = pl.pallas_call(
        functools.partial(_norm_matmul_kernel, emit_w=emit_w, aliased=into is not None),
        grid=(t1 - t0, c // tn),
        in_specs=in_specs,
        out_specs=out_specs,
        out_shape=out_shape,
        scratch_shapes=[pltpu.VMEM((tm, d), BF16)],
        input_output_aliases={3: 0} if into is not None else {},
        compiler_params=_params(("arbitrary", "arbitrary")),
        name=name,
    )(*args)
    return res if emit_w else res[0]


def _rglru_kernel(z_ref, rc0_ref, h0_ref, cw_ref, cb_ref, wa_ref, ba_ref, wx_ref, bx_ref, lam_ref,
                  ya_ref, h1_ref, rc1_ref, xp_ref, a_ref, b_ref, hs_ref, hc_ref, *, nb, T):
    it = pl.program_id(1)
    nt = pl.num_programs(1)
    hist = RNN_CONV_W - 1

    @pl.when(it == 0)
    def _():
        xp_ref[:, HALO - hist:HALO, :] = rc0_ref[...]
        hc_ref[...] = h0_ref[...]

    xp_ref[:, HALO:HALO + T, :] = z_ref[...]
    cw = cw_ref[...]
    xr = cb_ref[...][None]
    for j in range(RNN_CONV_W):
        xr = xr + xp_ref[:, HALO - hist + j:HALO - hist + j + T, :] * cw[j:j + 1][None]
    xp_ref[:, HALO - hist:HALO, :] = xp_ref[:, HALO + T - hist:HALO + T, :]

    xr2 = xr.reshape(nb * T, RNN_WIDTH)
    xb = xr2.astype(BF16)
    r = _sigmoid(jnp.dot(xb, wa_ref[...], preferred_element_type=F32) + ba_ref[...])
    ig = _sigmoid(jnp.dot(xb, wx_ref[...], preferred_element_type=F32) + bx_ref[...])
    nl = -lam_ref[...]
    softplus = jnp.maximum(nl, 0.0) + jnp.log1p(jnp.exp(-jnp.abs(nl)))
    log_a = -LRU_C * r * softplus
    a = jnp.exp(log_a)
    a_ref[...] = a.reshape(nb, T, RNN_WIDTH)
    one_minus_a2 = -jnp.tanh(log_a) * (a * a + 1.0)
    b_ref[...] = (jnp.sqrt(one_minus_a2) * (ig * xr2)).reshape(nb, T, RNN_WIDTH)

    hs = [hc_ref[bi:bi + 1, :] for bi in range(nb)]
    for t in range(T):
        for bi in range(nb):
            hs[bi] = a_ref[bi, t:t + 1, :] * hs[bi] + b_ref[bi, t:t + 1, :]
            hs_ref[bi, t:t + 1, :] = hs[bi]
    for bi in range(nb):
        hc_ref[bi:bi + 1, :] = hs[bi]
    ya_ref[...] = hs_ref[...].astype(BF16)

    @pl.when(it == nt - 1)
    def _():
        h1_ref[...] = hc_ref[...]
        rc1_ref[...] = xp_ref[:, HALO - hist:HALO, :]


def _rglru(z3, rc0, h0, cw, cb, wa, ba, wx, bx, lam, nb, T):
    B, L, _ = z3.shape
    C = RNN_WIDTH
    hist = RNN_CONV_W - 1
    vec = lambda: pl.BlockSpec((1, C), lambda b, t: (0, 0))
    mat = lambda: pl.BlockSpec((C, C), lambda b, t: (0, 0))
    return pl.pallas_call(
        functools.partial(_rglru_kernel, nb=nb, T=T),
        grid=(B // nb, L // T),
        in_specs=[
            pl.BlockSpec((nb, T, C), lambda b, t: (b, t, COL_RNN)),
            pl.BlockSpec((nb, hist, C), lambda b, t: (b, 0, 0)),
            pl.BlockSpec((nb, C), lambda b, t: (b, 0)),
            pl.BlockSpec((RNN_CONV_W, C), lambda b, t: (0, 0)),
            vec(), mat(), vec(), mat(), vec(), vec(),
        ],
        out_specs=[
            pl.BlockSpec((nb, T, C), lambda b, t: (b, t, 0)),
            pl.BlockSpec((nb, C), lambda b, t: (b, 0)),
            pl.BlockSpec((nb, hist, C), lambda b, t: (b, 0, 0)),
        ],
        out_shape=[
            jax.ShapeDtypeStruct((B, L, C), BF16),
            jax.ShapeDtypeStruct((B, C), F32),
            jax.ShapeDtypeStruct((B, hist, C), F32),
        ],
        scratch_shapes=[
            pltpu.VMEM((nb, HALO + T, C), F32),
            pltpu.VMEM((nb, T, C), F32),
            pltpu.VMEM((nb, T, C), F32),
            pltpu.VMEM((nb, T, C), F32),
            pltpu.VMEM((nb, C), F32),
        ],
        compiler_params=_params(("arbitrary", "arbitrary")),
        name="rglru",
    )(z3, rc0, h0, cw, cb.reshape(1, C), wa, ba.reshape(1, C), wx, bx.reshape(1, C), lam.reshape(1, C))


def _dot_nt(a, b):
    return lax.dot_general(a, b, (((1,), (1,)), ((), ())), preferred_element_type=F32)


def _split3(x):
    hi = x.astype(BF16)
    r = x - hi.astype(F32)
    mid = r.astype(BF16)
    lo = (r - mid.astype(F32)).astype(BF16)
    return jnp.concatenate([hi, mid, lo], axis=0)


def _hg_sub_rows(cs):
    return HG_SUB_MAX if cs >= 4 * HG_SUB_MAX else SUBLANES


def _hgrn2_scores(qh, bc, ck, ck_row, st, cs):
    o = _dot_nt((qh * jnp.exp2(bc)).astype(BF16), st.astype(BF16))

    sub_rows = _hg_sub_rows(cs)
    nblk = cs // sub_rows
    a_rows = [None] * nblk
    half = cs // 2
    while half >= sub_rows:
        npair = cs // (2 * half)
        q_parts, k_parts = [], []
        for p in range(npair):
            lo, mid, hi = 2 * half * p, 2 * half * p + half, 2 * half * (p + 1)
            br = bc[mid - 1:mid]
            q_parts.append(qh[mid:hi] * jnp.exp2(bc[mid:hi] - br))
            k_parts += [jnp.exp2(br - ck[lo:mid]), jnp.zeros((half, HG_DK), F32)]
        off = _dot_nt(jnp.concatenate(q_parts, axis=0).astype(BF16),
                      jnp.concatenate(k_parts, axis=0).astype(BF16))
        if npair > 1:
            lg = half.bit_length() - 1
            rowp = lax.shift_right_logical(lax.broadcasted_iota(jnp.int32, off.shape, 0), lg)
            colp = lax.shift_right_logical(lax.broadcasted_iota(jnp.int32, off.shape, 1), lg + 1)
            off = jnp.where(rowp == colp, off, 0.0)
        for p in range(npair):
            for sub in range(half // sub_rows):
                blk = (2 * half * p + half) // sub_rows + sub
                piece = off[p * half + sub * sub_rows:p * half + (sub + 1) * sub_rows]
                a_rows[blk] = piece if a_rows[blk] is None else a_rows[blk] + piece
        half //= 2

    ones = jnp.ones((HG_DK, HG_DK), BF16)
    zero = jnp.zeros((SUBLANES, HG_DK), F32)
    groups = sub_rows // SUBLANES
    lane_sums = []
    for blk in range(nblk):
        r0 = blk * sub_rows
        bg = [bc[r0 + v * SUBLANES:r0 + (v + 1) * SUBLANES] for v in range(groups)]
        qg = [qh[r0 + v * SUBLANES:r0 + (v + 1) * SUBLANES] for v in range(groups)]
        pair = []
        for s in range(sub_rows):
            cks = ck_row(r0 + s)
            pair += [qg[v] * jnp.exp2(bg[v] - cks) if v >= s // SUBLANES else zero
                     for v in range(groups)]
        lane_sums.append(jnp.dot(jnp.concatenate(pair, axis=0).astype(BF16), ones,
                                 preferred_element_type=F32))
    return o, a_rows, lane_sums


def _hgrn2_output(scores, vh, ogh, gnh, bl, ck, st, cs):
    o, a_rows, lane_sums = scores
    sub_rows = _hg_sub_rows(cs)
    groups = sub_rows // SUBLANES
    lane = lax.broadcasted_iota(jnp.int32, (SUBLANES, HG_DK), 1)
    tloc = lax.broadcasted_iota(jnp.int32, (SUBLANES, HG_DK), 0)
    mask = [[(lane == s) & (tloc >= s - v * SUBLANES) if s > v * SUBLANES else (lane == s)
             for v in range(groups)] for s in range(sub_rows)]
    a_rows = list(a_rows)
    for blk, lane_sum in enumerate(lane_sums):
        a_g = [jnp.zeros((SUBLANES, HG_DK), F32)] * groups
        for s in range(sub_rows):
            for v in range(s // SUBLANES, groups):
                r = s * sub_rows + v * SUBLANES
                a_g[v] = jnp.where(mask[s][v], lane_sum[r:r + SUBLANES], a_g[v])
        a_d = jnp.concatenate(a_g, axis=0)
        if blk:
            a_d = pltpu.roll(a_d, blk * sub_rows, 1)
        a_d = a_d[:, :cs]
        a_rows[blk] = a_d if a_rows[blk] is None else a_rows[blk] + a_d
    a = jnp.concatenate(a_rows, axis=0)
    o = o + jnp.dot(a.astype(BF16), vh.astype(BF16), preferred_element_type=F32)
    kd = jnp.exp2(bl - ck).astype(BF16)
    st_new = st * jnp.exp2(bl) + jnp.dot(vh.T.astype(BF16), kd, preferred_element_type=F32)
    y = o * _rms_scale(o) * gnh * _sigmoid(ogh)
    return y, st_new


def _hgrn2_kernel(q_ref, f_ref, v_ref, og_ref, s0_ref, lbp_ref, gn_ref, yb_ref, s1_ref,
                  lb_ref, bc_ref, ck_ref, *st_refs, nb, T, cs, layer):
    it = pl.program_id(1)
    nt = pl.num_programs(1)
    row = lax.broadcasted_iota(jnp.int32, (cs, 3 * cs), 0)
    col = lax.broadcasted_iota(jnp.int32, (cs, 3 * cs), 1)
    tri3 = (row >= (col & (cs - 1))).astype(BF16)

    @pl.when(it == 0)
    def _():
        p = lbp_ref[...]
        e = jnp.exp(p - jnp.max(p, axis=0, keepdims=True))
        sm = e / jnp.sum(e, axis=0, keepdims=True)
        lb_ref[...] = jnp.sum(sm[:layer + 1], axis=0, keepdims=True)

        def init(b, c):
            for h in range(HG_HEADS):
                st_refs[h][b] = s0_ref[b, h].T
            return c
        lax.fori_loop(0, nb, init, 0)

    nch = T // cs

    def chunk(idx, c):
        b = idx // nch
        rows = pl.ds(pl.multiple_of((idx % nch) * cs, cs), cs)
        lb = lb_ref[...]
        f = lb + (1.0 - lb) * _sigmoid(f_ref[b, rows, :])
        bc = jnp.dot(tri3, _split3(jnp.log(f) * LOG2E), preferred_element_type=F32)
        bc_ref[...] = bc
        ck_ref[...] = bc - jnp.log(1.0 - f) * LOG2E

        def scores(h):
            hl = slice(h * HG_DK, (h + 1) * HG_DK)
            return _hgrn2_scores(q_ref[b, rows, hl], bc_ref[:, hl], ck_ref[:, hl],
                                 lambda r: ck_ref[r:r + 1, hl], st_refs[h][b], cs)

        def output(h, sc):
            hl = slice(h * HG_DK, (h + 1) * HG_DK)
            y, st_new = _hgrn2_output(sc, v_ref[b, rows, hl], og_ref[b, rows, hl], gn_ref[...],
                                      bc_ref[cs - 1:cs, hl], ck_ref[:, hl], st_refs[h][b], cs)
            yb_ref[b, rows, hl] = y.astype(BF16)
            st_refs[h][b] = st_new

        pending = {}
        for h in range(HG_HEADS + HG_SKEW):
            if h < HG_HEADS:
                pending[h] = scores(h)
            if h >= HG_SKEW:
                output(h - HG_SKEW, pending.pop(h - HG_SKEW))
        return c
    lax.fori_loop(0, nb * nch, chunk, 0)

    @pl.when(it == nt - 1)
    def _():
        def fin(b, c):
            for h in range(HG_HEADS):
                s1_ref[b, h] = st_refs[h][b].T
            return c
        lax.fori_loop(0, nb, fin, 0)


def _hgrn2(z3, s0, hg_lb, gn, layer, nb, T):
    B, L, _ = z3.shape
    C = HG_HEADS * HG_DK
    cs = min(HG_CHUNK, L)
    zcol = lambda col: pl.BlockSpec((nb, T, C), lambda b, t: (b, t, col))
    st_spec = lambda: pl.BlockSpec((nb, HG_HEADS, HG_DK, HG_DV), lambda b, t: (b, 0, 0, 0))
    nl = hg_lb.shape[0]
    return pl.pallas_call(
        functools.partial(_hgrn2_kernel, nb=nb, T=T, cs=cs, layer=layer),
        grid=(B // nb, L // T),
        in_specs=[
            zcol(COL_HQ), zcol(COL_HF), zcol(COL_HI), zcol(COL_HO),
            st_spec(),
            pl.BlockSpec((nl, C), lambda b, t: (0, 0)),
            pl.BlockSpec((1, HG_DV), lambda b, t: (0, 0)),
        ],
        out_specs=[pl.BlockSpec((nb, T, C), lambda b, t: (b, t, 0)), st_spec()],
        out_shape=[
            jax.ShapeDtypeStruct((B, L, C), BF16),
            jax.ShapeDtypeStruct((B, HG_HEADS, HG_DK, HG_DV), F32),
        ],
        scratch_shapes=[
            pltpu.VMEM((1, C), F32),
            pltpu.VMEM((cs, C), F32),
            pltpu.VMEM((cs, C), F32),
        ] + [pltpu.VMEM((nb, HG_DV, HG_DK), F32) for _ in range(HG_HEADS)],
        compiler_params=_params(("arbitrary", "arbitrary")),
        name="hgrn2",
    )(z3, z3, z3, z3, s0, hg_lb, gn.reshape(1, HG_DV))


def _xattn_kernel(q_ref, k_ref, v_ref, o_ref):
    for h in range(XA_HEADS):
        hl = slice(h * XA_HD, (h + 1) * XA_HD)
        s = _dot_nt(q_ref[0, :, hl].astype(BF16), k_ref[0, :, hl].astype(BF16)) * (XA_HD ** -0.5)
        e = jnp.exp(s - jnp.max(s, axis=-1, keepdims=True))
        p = e / jnp.sum(e, axis=-1, keepdims=True)
        o = jnp.dot(p.astype(BF16), v_ref[0, :, hl].astype(BF16), preferred_element_type=F32)
        o_ref[0, :, hl] = o.astype(BF16)


def _xattn(z3, mk, mv, T):
    B, L, _ = z3.shape
    C = XA_HEADS * XA_HD
    mem = lambda: pl.BlockSpec((1, MEM_LEN, C), lambda b, t: (b, 0, 0))
    return pl.pallas_call(
        _xattn_kernel,
        grid=(B, L // T),
        in_specs=[pl.BlockSpec((1, T, C), lambda b, t: (b, t, COL_XQ)), mem(), mem()],
        out_specs=pl.BlockSpec((1, T, C), lambda b, t: (b, t, 0)),
        out_shape=jax.ShapeDtypeStruct((B, L, C), BF16),
        compiler_params=_params(("arbitrary", "arbitrary")),
        name="xattn",
    )(z3, mk, mv)


def _xattn_cached_kernel(q_ref, k_ref, v_ref, o_ref):
    for h in range(XA_HEADS):
        hl = slice(h * XA_HD, (h + 1) * XA_HD)
        s = _dot_nt(q_ref[0, :, hl].astype(BF16), k_ref[:, h, :].astype(BF16)) * (XA_HD ** -0.5)
        e = jnp.exp(s - jnp.max(s, axis=-1, keepdims=True))
        p = e / jnp.sum(e, axis=-1, keepdims=True)
        o = jnp.dot(p.astype(BF16), v_ref[:, h, :].astype(BF16), preferred_element_type=F32)
        o_ref[0, :, hl] = o.astype(BF16)


def _xattn_cached(z3, cache_k, cache_v, layer):
    B, L, _ = z3.shape
    C = XA_HEADS * XA_HD
    mem = lambda: pl.BlockSpec((None, None, MEM_LEN, XA_HEADS, XA_HD), lambda b: (layer, b, 0, 0, 0))
    return pl.pallas_call(
        _xattn_cached_kernel,
        grid=(B,),
        in_specs=[pl.BlockSpec((1, L, C), lambda b: (b, 0, COL_XQ)), mem(), mem()],
        out_specs=pl.BlockSpec((1, L, C), lambda b: (b, 0, 0)),
        out_shape=jax.ShapeDtypeStruct((B, L, C), BF16),
        compiler_params=_params(("arbitrary",)),
        name="xattn_cached",
    )(z3, cache_k, cache_v)


def _merge_kernel(x_ref, ya_ref, yb_ref, yc_ref, g0_ref, g1_ref, g2_ref, bg_ref, wb_ref, wo_ref,
                  pg_ref, o_ref):
    m = jnp.zeros(o_ref.shape, F32)
    for nb, (y_ref, g_ref) in enumerate(((ya_ref, g0_ref), (yb_ref, g1_ref), (yc_ref, g2_ref))):
        gate = _sigmoid(g_ref[...] + bg_ref[nb:nb + 1, :])
        m = m + gate * jnp.dot(y_ref[...], wb_ref[nb], preferred_element_type=F32)
    y = jnp.dot(m.astype(BF16), wo_ref[...], preferred_element_type=F32)
    o_ref[...] = x_ref[...] + y * _rms_scale(y) * pg_ref[...]


def _merge(x, ya, yb, yc, z, b_gate, wb, wo, pg, tm):
    n, d = x.shape
    c = BRANCH_WIDTH
    once = pl.Buffered(1)
    branch = lambda: pl.BlockSpec((tm, c), lambda i: (i, 0))
    gate = lambda nb: pl.BlockSpec((tm, d), lambda i: (i, GATE_COL0 + nb))
    return pl.pallas_call(
        _merge_kernel,
        grid=(n // tm,),
        in_specs=[
            pl.BlockSpec((tm, d), lambda i: (i, 0)),
            branch(), branch(), branch(),
            gate(0), gate(1), gate(2),
            pl.BlockSpec((N_BRANCH, d), lambda i: (0, 0)),
            pl.BlockSpec((N_BRANCH, c, d), lambda i: (0, 0, 0), pipeline_mode=once),
            pl.BlockSpec((d, d), lambda i: (0, 0), pipeline_mode=once),
            pl.BlockSpec((1, d), lambda i: (0, 0)),
        ],
        out_specs=pl.BlockSpec((tm, d), lambda i: (i, 0)),
        out_shape=jax.ShapeDtypeStruct((n, d), F32),
        compiler_params=_params(("arbitrary",)),
        name="merge",
    )(x, ya, yb, yc, z, z, z, b_gate, wb, wo, pg.reshape(1, d))


def _ffn_kernel(x_ref, g_ref, wu_ref, wv_ref, cw_ref, cb_ref, fc0_ref, wd_ref, pg_ref, *rest,
                nb, T, tf, emit_w, aliased):
    rest = rest[1:] if aliased else rest
    o_ref, fc1_ref = rest[:2]
    xn_ref, acc_ref, hal_ref, car_ref = rest[-4:]
    it = pl.program_id(1)
    j = pl.program_id(2)
    nj = pl.num_programs(2)
    hist = FFN_CONV_W - 1

    @pl.when(j == 0)
    def _():
        x = x_ref[...].reshape(nb * T, D_MODEL)
        xn_ref[...] = (x * _rms_scale(x) * g_ref[...]).astype(BF16)
        acc_ref[...] = jnp.zeros_like(acc_ref)

    @pl.when(it == 0)
    def _():
        car_ref[j] = fc0_ref[...]

    wu, wv, wd = (r[...].astype(BF16) for r in (wu_ref, wv_ref, wd_ref))
    if emit_w:
        for out_ref, wb in zip(rest[2:5], (wu, wv, wd)):
            out_ref[...] = wb
    xn = xn_ref[...]
    u = jnp.dot(xn, wu, preferred_element_type=F32)
    v = jnp.dot(xn, wv, preferred_element_type=F32)
    hal_ref[:, HALO - hist:HALO, :] = car_ref[j]
    hal_ref[:, HALO:HALO + T, :] = u.reshape(nb, T, tf)
    cw = cw_ref[...]
    uc = cb_ref[...][None]
    for jj in range(FFN_CONV_W):
        uc = uc + hal_ref[:, HALO - hist + jj:HALO - hist + jj + T, :] * cw[jj:jj + 1][None]
    tail = hal_ref[:, HALO + T - hist:HALO + T, :]
    car_ref[j] = tail
    act = (jax.nn.gelu(uc).reshape(nb * T, tf) * v).astype(BF16)
    acc_ref[...] += jnp.dot(act, wd, preferred_element_type=F32)

    @pl.when(j == nj - 1)
    def _():
        y = acc_ref[...]
        out = x_ref[...].reshape(nb * T, D_MODEL) + y * _rms_scale(y) * pg_ref[...]
        o_ref[...] = out.reshape(nb, T, D_MODEL)

    @pl.when((j == nj - 1) & (it == pl.num_programs(1) - 1))
    def _():
        for jj in range(FFN_DIM // tf):
            fc1_ref[:, :, jj * tf:(jj + 1) * tf] = car_ref[jj]


def _ffn(x3, g, wu, wv, v_off, cw, cb, fc0, w_down, pg, nb, T, tf, tiles=None, emit_w=False, into=None):
    B, L, d = x3.shape
    F = FFN_DIM
    nj = F // tf
    hist = FFN_CONV_W - 1
    t0, t1 = tiles if tiles is not None else (0, L // T)
    assert not emit_w or (t1 - t0 == 1 and B == nb)
    rows = lambda: pl.BlockSpec((nb, T, d), lambda b, t, j: (b, t + t0, 0))
    in_specs = [
        rows(),
        pl.BlockSpec((1, d), lambda b, t, j: (0, 0)),
        pl.BlockSpec((d, tf), lambda b, t, j: (0, j)),
        pl.BlockSpec((d, tf), lambda b, t, j: (0, v_off // tf + j)),
        pl.BlockSpec((FFN_CONV_W, tf), lambda b, t, j: (0, j)),
        pl.BlockSpec((1, tf), lambda b, t, j: (0, j)),
        pl.BlockSpec((nb, hist, tf), lambda b, t, j: (b, 0, j)),
        pl.BlockSpec((tf, d), lambda b, t, j: (j, 0)),
        pl.BlockSpec((1, d), lambda b, t, j: (0, 0)),
    ]
    args = [x3, g.reshape(1, d), wu, wv, cw, cb.reshape(1, F), fc0, w_down, pg.reshape(1, d)]
    out_specs = [rows(), pl.BlockSpec((nb, hist, F), lambda b, t, j: (b, 0, 0))]
    out_shape = [jax.ShapeDtypeStruct((B, L, d), F32), jax.ShapeDtypeStruct((B, hist, F), F32)]
    if into is not None:
        in_specs.append(pl.BlockSpec(memory_space=pl.ANY))
        args.append(into)
    if emit_w:
        out_specs += [pl.BlockSpec((d, tf), lambda b, t, j: (0, j)),
                      pl.BlockSpec((d, tf), lambda b, t, j: (0, j)),
                      pl.BlockSpec((tf, d), lambda b, t, j: (j, 0))]
        out_shape += [jax.ShapeDtypeStruct((d, F), BF16), jax.ShapeDtypeStruct((d, F), BF16),
                      jax.ShapeDtypeStruct((F, d), BF16)]
    res = pl.pallas_call(
        functools.partial(_ffn_kernel, nb=nb, T=T, tf=tf, emit_w=emit_w, aliased=into is not None),
        grid=(B // nb, t1 - t0, nj),
        in_specs=in_specs,
        out_specs=out_specs,
        out_shape=out_shape,
        scratch_shapes=[
            pltpu.VMEM((nb * T, d), BF16),
            pltpu.VMEM((nb * T, d), F32),
            pltpu.VMEM((nb, HALO + T, tf), F32),
            pltpu.VMEM((nj, nb, hist, tf), F32),
        ],
        input_output_aliases={9: 0} if into is not None else {},
        compiler_params=_params(("arbitrary", "arbitrary", "arbitrary")),
        name="ffn",
    )(*args)
    return res


def _block_diag(w):
    nblk, bi, bj = w.shape
    eye = jnp.eye(nblk, dtype=w.dtype)
    return jnp.einsum("hij,hg->higj", w, eye).reshape(nblk * bi, nblk * bj)


def _tiles(B, L):
    T = min(L, 256)
    return dict(
        proj_tm=min(B * L, 1024), proj_tn=1536,
        rnn_nb=B if B * T <= 512 else 512 // T, rnn_T=T,
        hg_nb=min(B, 4), hg_T=T,
        xa_T=min(L, 512),
        merge_tm=256,
        ffn_nb=B if B * T <= 512 else 512 // T, ffn_T=T, ffn_tf=512,
    )


def _trunk_layer(x, mk, mv, h0, rc0, s0, fc0, w, layer):
    B, L, d = x.shape
    n = B * L
    t = _tiles(B, L)
    x2 = x.reshape(n, d)
    made = {}
    tm, tn = t["proj_tm"], t["proj_tn"]
    if w["w_in"].dtype == F32:
        z, made["w_in"] = _norm_matmul(x2, w["pre_mix_norm"], w["w_in"], tm, tn // 2, "in_proj",
                                       tiles=(0, 1), emit_w=True)
        if n > tm:
            z = _norm_matmul(x2, w["pre_mix_norm"], made["w_in"], tm, tn, "in_proj",
                             tiles=(1, n // tm), into=z)
    else:
        z = _norm_matmul(x2, w["pre_mix_norm"], w["w_in"], tm, tn, "in_proj")
    z3 = z.reshape(B, L, IN_COLS)
    ya, h1, rc1 = _rglru(z3, rc0, h0, w["rnn_conv_w"], w["rnn_conv_b"], w["lru_wa"], w["lru_ba"],
                         w["lru_wx"], w["lru_bx"], w["lru_lambda"], t["rnn_nb"], t["rnn_T"])
    yb, s1 = _hgrn2(z3, s0, w["hg_lb"], w["hg_norm"], layer, t["hg_nb"], t["hg_T"])
    yc = _xattn(z3, mk, mv, t["xa_T"]) if mv is not None else _xattn_cached(z3, *mk, layer)
    c = BRANCH_WIDTH
    x1 = _merge(x2, ya.reshape(n, c), yb.reshape(n, c), yc.reshape(n, c), z, w["b_gate"],
                w["w_branch"], w["w_out"], w["post_mix_norm"], t["merge_tm"]).reshape(B, L, d)
    nb, T, tf = t["ffn_nb"], t["ffn_T"], t["ffn_tf"]
    ffn = functools.partial(_ffn, x1, w["pre_ffn_norm"], cw=w["ffn_conv_w"], cb=w["ffn_conv_b"],
                            pg=w["post_ffn_norm"], nb=nb, T=T)
    if "w_ffn_up" in w:
        xo, fc1, made["w_ffn_u"], made["w_ffn_v"], made["w_ffn_down"] = ffn(
            wu=w["w_ffn_up"], wv=w["w_ffn_up"], v_off=FFN_DIM, fc0=fc0, w_down=w["w_ffn_down"],
            tf=tf // 2, tiles=(0, 1), emit_w=True)
        if L > T:
            xo, fc1 = ffn(wu=made["w_ffn_u"], wv=made["w_ffn_v"], v_off=0, fc0=fc1,
                          w_down=made["w_ffn_down"], tf=tf, tiles=(1, L // T), into=xo)
    else:
        xo, fc1 = ffn(wu=w["w_ffn_u"], wv=w["w_ffn_v"], v_off=0, fc0=fc0, w_down=w["w_ffn_down"], tf=tf)
    return xo, h1, rc1, s1, fc1, made


def kernel(x_prompt, x_sample, cache_mem_k, cache_mem_v, state_rnn_h, state_rnn_conv, state_hg,
           state_ffn_conv, mem_prompt, pre_mix_norm, w_in, rnn_conv_w, rnn_conv_b, lru_wa, lru_ba,
           lru_wx, lru_bx, lru_lambda, hg_lb, hg_norm, mem_norm, w_mem_kv, w_branch, b_gate, w_out,
           post_mix_norm, pre_ffn_norm, w_ffn_up, ffn_conv_w, ffn_conv_b, w_ffn_down, post_ffn_norm):
    depth = w_in.shape[0]
    Bp = x_prompt.shape[0]
    Bs = x_sample.shape[0]
    xa_w = XA_HEADS * XA_HD
    yp, ys = x_prompt, x_sample
    outs = [[] for _ in range(10)]
    for l in range(depth):
        w = {
            "pre_mix_norm": pre_mix_norm[l], "w_in": w_in[l],
            "rnn_conv_w": rnn_conv_w[l], "rnn_conv_b": rnn_conv_b[l],
            "lru_wa": _block_diag(lru_wa[l]).astype(BF16), "lru_ba": lru_ba[l],
            "lru_wx": _block_diag(lru_wx[l]).astype(BF16), "lru_bx": lru_bx[l],
            "lru_lambda": lru_lambda[l], "hg_lb": hg_lb, "hg_norm": hg_norm[l],
            "w_branch": w_branch[l].astype(BF16), "b_gate": b_gate[l],
            "w_out": w_out[l].astype(BF16), "post_mix_norm": post_mix_norm[l],
            "pre_ffn_norm": pre_ffn_norm[l], "w_ffn_up": w_ffn_up[l],
            "ffn_conv_w": ffn_conv_w[l], "ffn_conv_b": ffn_conv_b[l],
            "w_ffn_down": w_ffn_down[l], "post_ffn_norm": post_ffn_norm[l],
        }
        mem2 = mem_prompt.reshape(Bp * MEM_LEN, D_MODEL)
        kv = _norm_matmul(mem2, mem_norm[l], w_mem_kv[l], Bp * MEM_LEN, 1024, "mem_kv")
        kv = kv.reshape(Bp, MEM_LEN, 2 * xa_w)
        mk_p, mv_p = kv[..., :xa_w], kv[..., xa_w:]
        yp, h_p, rc_p, s_p, fc_p, made = _trunk_layer(
            yp, mk_p, mv_p, jnp.zeros((Bp, RNN_WIDTH), F32),
            jnp.zeros((Bp, RNN_CONV_W - 1, RNN_WIDTH), F32),
            jnp.zeros((Bp, HG_HEADS, HG_DK, HG_DV), F32),
            jnp.zeros((Bp, FFN_CONV_W - 1, FFN_DIM), F32), w, l)
        w = {k: v for k, v in w.items() if k != "w_ffn_up"} | made
        ys, h_s, rc_s, s_s, fc_s, _ = _trunk_layer(
            ys, (cache_mem_k, cache_mem_v), None,
            state_rnn_h[l], state_rnn_conv[l], state_hg[l], state_ffn_conv[l], w, l)
        layer_out = (mk_p.reshape(Bp, MEM_LEN, XA_HEADS, XA_HD), mv_p.reshape(Bp, MEM_LEN, XA_HEADS, XA_HD),
                     h_p, rc_p, s_p, fc_p, h_s, rc_s, s_s, fc_s)
        for acc, val in zip(outs, layer_out):
            acc.append(val)
    return (yp, ys) + tuple(jnp.stack(o) for o in outs)
```

```python
import functools

import jax
import jax.numpy as jnp
from jax import lax
from jax.experimental import pallas as pl
from jax.experimental.pallas import tpu as pltpu

F32 = jnp.float32
BF16 = jnp.bfloat16

D_MODEL = 2048
RNN_WIDTH = 1024
RNN_BLOCKS = 16
RNN_CONV_W = 4
LRU_C = 8.0
HG_HEADS = 8
HG_DK = 128
HG_DV = 128
HG_CHUNK = 128
HG_SUB_MAX = 16
SUBLANES = 8
HG_SKEW = 8
MEM_LEN = 256
XA_HEADS = 4
XA_HD = 256
BRANCH_WIDTH = 1024
N_BRANCH = 3
FFN_DIM = 5632
FFN_CONV_W = 3
EPS = 1e-6
LOG2E = 1.4426950408889634
IN_COLS = 6 * BRANCH_WIDTH + N_BRANCH * D_MODEL

COL_RNN, COL_HQ, COL_HF, COL_HI, COL_HO, COL_XQ = range(6)
GATE_COL0 = 3

HALO = 8
VMEM_LIMIT = 56 * 1024 * 1024


def _params(sem):
    return pltpu.CompilerParams(dimension_semantics=sem, vmem_limit_bytes=VMEM_LIMIT)


def _rms_scale(x):
    return lax.rsqrt(jnp.mean(x * x, axis=-1, keepdims=True) + EPS)


_sigmoid = jax.nn.sigmoid


def _norm_matmul_kernel(x_ref, g_ref, w_ref, *rest, emit_w, aliased):
    rest = rest[1:] if aliased else rest
    o_ref, xn_ref = rest[0], rest[-1]

    @pl.when(pl.program_id(1) == 0)
    def _():
        x = x_ref[...]
        xn_ref[...] = (x * _rms_scale(x) * g_ref[...]).astype(BF16)

    w = w_ref[...].astype(BF16)
    if emit_w:
        rest[1][...] = w
    o_ref[...] = jnp.dot(xn_ref[...], w, preferred_element_type=F32)


def _norm_matmul(x, g, w, tm, tn, name, tiles=None, emit_w=False, into=None):
    n, d = x.shape
    c = w.shape[1]
    t0, t1 = tiles if tiles is not None else (0, n // tm)
    assert not emit_w or t1 - t0 == 1
    in_specs = [
        pl.BlockSpec((tm, d), lambda i, j: (i + t0, 0)),
        pl.BlockSpec((1, d), lambda i, j: (0, 0)),
        pl.BlockSpec((d, tn), lambda i, j: (0, j)),
    ]
    args = [x, g.reshape(1, d), w]
    out_specs = [pl.BlockSpec((tm, tn), lambda i, j: (i + t0, j))]
    out_shape = [jax.ShapeDtypeStruct((n, c), F32)]
    if into is not None:
        in_specs.append(pl.BlockSpec(memory_space=pl.ANY))
        args.append(into)
    if emit_w:
        out_specs.append(pl.BlockSpec((d, tn), lambda i, j: (0, j)))
        out_shape.append(jax.ShapeDtypeStruct((d, c), BF16))
    res = pl.pallas_call(
        functools.partial(_norm_matmul_kernel, emit_w=emit_w, aliased=into is not None),
        grid=(t1 - t0, c // tn),
        in_specs=in_specs,
        out_specs=out_specs,
        out_shape=out_shape,
        scratch_shapes=[pltpu.VMEM((tm, d), BF16)],
        input_output_aliases={3: 0} if into is not None else {},
        compiler_params=_params(("arbitrary", "arbitrary")),
        name=name,
    )(*args)
    return res if emit_w else res[0]


def _rglru_kernel(z_ref, rc0_ref, h0_ref, cw_ref, cb_ref, wa_ref, ba_ref, wx_ref, bx_ref, lam_ref,
                  ya_ref, h1_ref, rc1_ref, xp_ref, a_ref, b_ref, hs_ref, hc_ref, *, nb, T):
    it = pl.program_id(1)
    nt = pl.num_programs(1)
    hist = RNN_CONV_W - 1

    @pl.when(it == 0)
    def _():
        xp_ref[:, HALO - hist:HALO, :] = rc0_ref[...]
        hc_ref[...] = h0_ref[...]

    xp_ref[:, HALO:HALO + T, :] = z_ref[...]
    cw = cw_ref[...]
    xr = cb_ref[...][None]
    for j in range(RNN_CONV_W):
        xr = xr + xp_ref[:, HALO - hist + j:HALO - hist + j + T, :] * cw[j:j + 1][None]
    xp_ref[:, HALO - hist:HALO, :] = xp_ref[:, HALO + T - hist:HALO + T, :]

    xr2 = xr.reshape(nb * T, RNN_WIDTH)
    xb = xr2.astype(BF16)
    r = _sigmoid(jnp.dot(xb, wa_ref[...], preferred_element_type=F32) + ba_ref[...])
    ig = _sigmoid(jnp.dot(xb, wx_ref[...], preferred_element_type=F32) + bx_ref[...])
    nl = -lam_ref[...]
    softplus = jnp.maximum(nl, 0.0) + jnp.log1p(jnp.exp(-jnp.abs(nl)))
    log_a = -LRU_C * r * softplus
    a = jnp.exp(log_a)
    a_ref[...] = a.reshape(nb, T, RNN_WIDTH)
    one_minus_a2 = -jnp.tanh(log_a) * (a * a + 1.0)
    b_ref[...] = (jnp.sqrt(one_minus_a2) * (ig * xr2)).reshape(nb, T, RNN_WIDTH)

    hs = [hc_ref[bi:bi + 1, :] for bi in range(nb)]
    for t in range(T):
        for bi in range(nb):
            hs[bi] = a_ref[bi, t:t + 1, :] * hs[bi] + b_ref[bi, t:t + 1, :]
            hs_ref[bi, t:t + 1, :] = hs[bi]
    for bi in range(nb):
        hc_ref[bi:bi + 1, :] = hs[bi]
    ya_ref[...] = hs_ref[...].astype(BF16)

    @pl.when(it == nt - 1)
    def _():
        h1_ref[...] = hc_ref[...]
        rc1_ref[...] = xp_ref[:, HALO - hist:HALO, :]


def _rglru(z3, rc0, h0, cw, cb, wa, ba, wx, bx, lam, nb, T):
    B, L, _ = z3.shape
    C = RNN_WIDTH
    hist = RNN_CONV_W - 1
    vec = lambda: pl.BlockSpec((1, C), lambda b, t: (0, 0))
    mat = lambda: pl.BlockSpec((C, C), lambda b, t: (0, 0))
    return pl.pallas_call(
        functools.partial(_rglru_kernel, nb=nb, T=T),
        grid=(B // nb, L // T),
        in_specs=[
            pl.BlockSpec((nb, T, C), lambda b, t: (b, t, COL_RNN)),
            pl.BlockSpec((nb, hist, C), lambda b, t: (b, 0, 0)),
            pl.BlockSpec((nb, C), lambda b, t: (b, 0)),
            pl.BlockSpec((RNN_CONV_W, C), lambda b, t: (0, 0)),
            vec(), mat(), vec(), mat(), vec(), vec(),
        ],
        out_specs=[
            pl.BlockSpec((nb, T, C), lambda b, t: (b, t, 0)),
            pl.BlockSpec((nb, C), lambda b, t: (b, 0)),
            pl.BlockSpec((nb, hist, C), lambda b, t: (b, 0, 0)),
        ],
        out_shape=[
            jax.ShapeDtypeStruct((B, L, C), BF16),
            jax.ShapeDtypeStruct((B, C), F32),
            jax.ShapeDtypeStruct((B, hist, C), F32),
        ],
        scratch_shapes=[
            pltpu.VMEM((nb, HALO + T, C), F32),
            pltpu.VMEM((nb, T, C), F32),
            pltpu.VMEM((nb, T, C), F32),
            pltpu.VMEM((nb, T, C), F32),
            pltpu.VMEM((nb, C), F32),
        ],
        compiler_params=_params(("arbitrary", "arbitrary")),
        name="rglru",
    )(z3, rc0, h0, cw, cb.reshape(1, C), wa, ba.reshape(1, C), wx, bx.reshape(1, C), lam.reshape(1, C))


def _dot_nt(a, b):
    return lax.dot_general(a, b, (((1,), (1,)), ((), ())), preferred_element_type=F32)


def _split3(x):
    hi = x.astype(BF16)
    r = x - hi.astype(F32)
    mid = r.astype(BF16)
    lo = (r - mid.astype(F32)).astype(BF16)
    return jnp.concatenate([hi, mid, lo], axis=0)


def _hg_sub_rows(cs):
    return HG_SUB_MAX if cs >= 4 * HG_SUB_MAX else SUBLANES


def _hgrn2_scores(qh, bc, ck, ck_row, st, cs):
    o = _dot_nt((qh * jnp.exp2(bc)).astype(BF16), st.astype(BF16))

    sub_rows = _hg_sub_rows(cs)
    nblk = cs // sub_rows
    a_rows = [None] * nblk
    half = cs // 2
    while half >= sub_rows:
        npair = cs // (2 * half)
        q_parts, k_parts = [], []
        for p in range(npair):
            lo, mid, hi = 2 * half * p, 2 * half * p + half, 2 * half * (p + 1)
            br = bc[mid - 1:mid]
            q_parts.append(qh[mid:hi] * jnp.exp2(bc[mid:hi] - br))
            k_parts += [jnp.exp2(br - ck[lo:mid]), jnp.zeros((half, HG_DK), F32)]
        off = _dot_nt(jnp.concatenate(q_parts, axis=0).astype(BF16),
                      jnp.concatenate(k_parts, axis=0).astype(BF16))
        if npair > 1:
            lg = half.bit_length() - 1
            rowp = lax.shift_right_logical(lax.broadcasted_iota(jnp.int32, off.shape, 0), lg)
            colp = lax.shift_right_logical(lax.broadcasted_iota(jnp.int32, off.shape, 1), lg + 1)
            off = jnp.where(rowp == colp, off, 0.0)
        for p in range(npair):
            for sub in range(half // sub_rows):
                blk = (2 * half * p + half) // sub_rows + sub
                piece = off[p * half + sub * sub_rows:p * half + (sub + 1) * sub_rows]
                a_rows[blk] = piece if a_rows[blk] is None else a_rows[blk] + piece
        half //= 2

    ones = jnp.ones((HG_DK, HG_DK), BF16)
    zero = jnp.zeros((SUBLANES, HG_DK), F32)
    groups = sub_rows // SUBLANES
    lane_sums = []
    for blk in range(nblk):
        r0 = blk * sub_rows
        bg = [bc[r0 + v * SUBLANES:r0 + (v + 1) * SUBLANES] for v in range(groups)]
        qg = [qh[r0 + v * SUBLANES:r0 + (v + 1) * SUBLANES] for v in range(groups)]
        pair = []
        for s in range(sub_rows):
            cks = ck_row(r0 + s)
            pair += [qg[v] * jnp.exp2(bg[v] - cks) if v >= s // SUBLANES else zero
                     for v in range(groups)]
        lane_sums.append(jnp.dot(jnp.concatenate(pair, axis=0).astype(BF16), ones,
                                 preferred_element_type=F32))
    return o, a_rows, lane_sums


def _hgrn2_output(scores, vh, ogh, gnh, bl, ck, st, cs):
    o, a_rows, lane_sums = scores
    sub_rows = _hg_sub_rows(cs)
    groups = sub_rows // SUBLANES
    lane = lax.broadcasted_iota(jnp.int32, (SUBLANES, HG_DK), 1)
    tloc = lax.broadcasted_iota(jnp.int32, (SUBLANES, HG_DK), 0)
    mask = [[(lane == s) & (tloc >= s - v * SUBLANES) if s > v * SUBLANES else (lane == s)
             for v in range(groups)] for s in range(sub_rows)]
    a_rows = list(a_rows)
    for blk, lane_sum in enumerate(lane_sums):
        a_g = [jnp.zeros((SUBLANES, HG_DK), F32)] * groups
        for s in range(sub_rows):
            for v in range(s // SUBLANES, groups):
                r = s * sub_rows + v * SUBLANES
                a_g[v] = jnp.where(mask[s][v], lane_sum[r:r + SUBLANES], a_g[v])
        a_d = jnp.concatenate(a_g, axis=0)
        if blk:
            a_d = pltpu.roll(a_d, blk * sub_rows, 1)
        a_d = a_d[:, :cs]
        a_rows[blk] = a_d if a_rows[blk] is None else a_rows[blk] + a_d
    a = jnp.concatenate(a_rows, axis=0)
    o = o + jnp.dot(a.astype(BF16), vh.astype(BF16), preferred_element_type=F32)
    kd = jnp.exp2(bl - ck).astype(BF16)
    st_new = st * jnp.exp2(bl) + jnp.dot(vh.T.astype(BF16), kd, preferred_element_type=F32)
    y = o * _rms_scale(o) * gnh * _sigmoid(ogh)
    return y, st_new


def _hgrn2_kernel(q_ref, f_ref, v_ref, og_ref, s0_ref, lbp_ref, gn_ref, yb_ref, s1_ref,
                  lb_ref, bc_ref, ck_ref, *st_refs, nb, T, cs, layer):
    it = pl.program_id(1)
    nt = pl.num_programs(1)
    row = lax.broadcasted_iota(jnp.int32, (cs, 3 * cs), 0)
    col = lax.broadcasted_iota(jnp.int32, (cs, 3 * cs), 1)
    tri3 = (row >= (col & (cs - 1))).astype(BF16)

    @pl.when(it == 0)
    def _():
        p = lbp_ref[...]
        e = jnp.exp(p - jnp.max(p, axis=0, keepdims=True))
        sm = e / jnp.sum(e, axis=0, keepdims=True)
        lb_ref[...] = jnp.sum(sm[:layer + 1], axis=0, keepdims=True)

        def init(b, c):
            for h in range(HG_HEADS):
                st_refs[h][b] = s0_ref[b, h].T
            return c
        lax.fori_loop(0, nb, init, 0)

    nch = T // cs

    def chunk(idx, c):
        b = idx // nch
        rows = pl.ds(pl.multiple_of((idx % nch) * cs, cs), cs)
        lb = lb_ref[...]
        f = lb + (1.0 - lb) * _sigmoid(f_ref[b, rows, :])
        bc = jnp.dot(tri3, _split3(jnp.log(f) * LOG2E), preferred_element_type=F32)
        bc_ref[...] = bc
        ck_ref[...] = bc - jnp.log(1.0 - f) * LOG2E

        def scores(h):
            hl = slice(h * HG_DK, (h + 1) * HG_DK)
            return _hgrn2_scores(q_ref[b, rows, hl], bc_ref[:, hl], ck_ref[:, hl],
                                 lambda r: ck_ref[r:r + 1, hl], st_refs[h][b], cs)

        def output(h, sc):
            hl = slice(h * HG_DK, (h + 1) * HG_DK)
            y, st_new = _hgrn2_output(sc, v_ref[b, rows, hl], og_ref[b, rows, hl], gn_ref[...],
                                      bc_ref[cs - 1:cs, hl], ck_ref[:, hl], st_refs[h][b], cs)
            yb_ref[b, rows, hl] = y.astype(BF16)
            st_refs[h][b] = st_new

        pending = {}
        for h in range(HG_HEADS + HG_SKEW):
            if h < HG_HEADS:
                pending[h] = scores(h)
            if h >= HG_SKEW:
                output(h - HG_SKEW, pending.pop(h - HG_SKEW))
        return c
    lax.fori_loop(0, nb * nch, chunk, 0)

    @pl.when(it == nt - 1)
    def _():
        def fin(b, c):
            for h in range(HG_HEADS):
                s1_ref[b, h] = st_refs[h][b].T
            return c
        lax.fori_loop(0, nb, fin, 0)


def _hgrn2(z3, s0, hg_lb, gn, layer, nb, T):
    B, L, _ = z3.shape
    C = HG_HEADS * HG_DK
    cs = min(HG_CHUNK, L)
    zcol = lambda col: pl.BlockSpec((nb, T, C), lambda b, t: (b, t, col))
    st_spec = lambda: pl.BlockSpec((nb, HG_HEADS, HG_DK, HG_DV), lambda b, t: (b, 0, 0, 0))
    nl = hg_lb.shape[0]
    return pl.pallas_call(
        functools.partial(_hgrn2_kernel, nb=nb, T=T, cs=cs, layer=layer),
        grid=(B // nb, L // T),
        in_specs=[
            zcol(COL_HQ), zcol(COL_HF), zcol(COL_HI), zcol(COL_HO),
            st_spec(),
            pl.BlockSpec((nl, C), lambda b, t: (0, 0)),
            pl.BlockSpec((1, HG_DV), lambda b, t: (0, 0)),
        ],
        out_specs=[pl.BlockSpec((nb, T, C), lambda b, t: (b, t, 0)), st_spec()],
        out_shape=[
            jax.ShapeDtypeStruct((B, L, C), BF16),
            jax.ShapeDtypeStruct((B, HG_HEADS, HG_DK, HG_DV), F32),
        ],
        scratch_shapes=[
            pltpu.VMEM((1, C), F32),
            pltpu.VMEM((cs, C), F32),
            pltpu.VMEM((cs, C), F32),
        ] + [pltpu.VMEM((nb, HG_DV, HG_DK), F32) for _ in range(HG_HEADS)],
        compiler_params=_params(("arbitrary", "arbitrary")),
        name="hgrn2",
    )(z3, z3, z3, z3, s0, hg_lb, gn.reshape(1, HG_DV))


def _softmax_rows(s):
    e = jnp.exp(s - jnp.max(s, axis=-1, keepdims=True))
    return e / jnp.sum(e, axis=-1, keepdims=True)


def _xattn_kernel(q_ref, k_ref, v_ref, o_ref):
    hls = [slice(h * XA_HD, (h + 1) * XA_HD) for h in range(XA_HEADS)]
    s = [_dot_nt(q_ref[0, :, hl].astype(BF16), k_ref[0, :, hl].astype(BF16)) * (XA_HD ** -0.5)
         for hl in hls]
    p = [_softmax_rows(sh).astype(BF16) for sh in s]
    for hl, ph in zip(hls, p):
        o = jnp.dot(ph, v_ref[0, :, hl].astype(BF16), preferred_element_type=F32)
        o_ref[0, :, hl] = o.astype(BF16)


def _xattn(z3, mk, mv, T):
    B, L, _ = z3.shape
    C = XA_HEADS * XA_HD
    mem = lambda: pl.BlockSpec((1, MEM_LEN, C), lambda b, t: (b, 0, 0))
    return pl.pallas_call(
        _xattn_kernel,
        grid=(B, L // T),
        in_specs=[pl.BlockSpec((1, T, C), lambda b, t: (b, t, COL_XQ)), mem(), mem()],
        out_specs=pl.BlockSpec((1, T, C), lambda b, t: (b, t, 0)),
        out_shape=jax.ShapeDtypeStruct((B, L, C), BF16),
        compiler_params=_params(("arbitrary", "arbitrary")),
        name="xattn",
    )(z3, mk, mv)


def _xattn_cached_kernel(q_ref, k_ref, v_ref, o_ref):
    kt = jnp.swapaxes(k_ref[...], 0, 1)
    vt = jnp.swapaxes(v_ref[...], 0, 1)
    hls = [slice(h * XA_HD, (h + 1) * XA_HD) for h in range(XA_HEADS)]
    s = [_dot_nt(q_ref[0, :, hl].astype(BF16), kt[h].astype(BF16)) * (XA_HD ** -0.5)
         for h, hl in enumerate(hls)]
    p = [_softmax_rows(sh).astype(BF16) for sh in s]
    for h, hl in enumerate(hls):
        o = jnp.dot(p[h], vt[h].astype(BF16), preferred_element_type=F32)
        o_ref[0, :, hl] = o.astype(BF16)


def _xattn_cached(z3, cache_k, cache_v, layer):
    B, L, _ = z3.shape
    C = XA_HEADS * XA_HD
    mem = lambda: pl.BlockSpec((None, None, MEM_LEN, XA_HEADS, XA_HD), lambda b: (layer, b, 0, 0, 0))
    return pl.pallas_call(
        _xattn_cached_kernel,
        grid=(B,),
        in_specs=[pl.BlockSpec((1, L, C), lambda b: (b, 0, COL_XQ)), mem(), mem()],
        out_specs=pl.BlockSpec((1, L, C), lambda b: (b, 0, 0)),
        out_shape=jax.ShapeDtypeStruct((B, L, C), BF16),
        compiler_params=_params(("arbitrary",)),
        name="xattn_cached",
    )(z3, cache_k, cache_v)


def _merge_kernel(x_ref, ya_ref, yb_ref, yc_ref, g0_ref, g1_ref, g2_ref, bg_ref, wb_ref, wo_ref,
                  pg_ref, o_ref):
    m = jnp.zeros(o_ref.shape, F32)
    for nb, (y_ref, g_ref) in enumerate(((ya_ref, g0_ref), (yb_ref, g1_ref), (yc_ref, g2_ref))):
        gate = _sigmoid(g_ref[...] + bg_ref[nb:nb + 1, :])
        m = m + gate * jnp.dot(y_ref[...], wb_ref[nb], preferred_element_type=F32)
    y = jnp.dot(m.astype(BF16), wo_ref[...], preferred_element_type=F32)
    o_ref[...] = x_ref[...] + y * _rms_scale(y) * pg_ref[...]


def _merge(x, ya, yb, yc, z, b_gate, wb, wo, pg, tm):
    n, d = x.shape
    c = BRANCH_WIDTH
    once = pl.Buffered(1)
    branch = lambda: pl.BlockSpec((tm, c), lambda i: (i, 0))
    gate = lambda nb: pl.BlockSpec((tm, d), lambda i: (i, GATE_COL0 + nb))
    return pl.pallas_call(
        _merge_kernel,
        grid=(n // tm,),
        in_specs=[
            pl.BlockSpec((tm, d), lambda i: (i, 0)),
            branch(), branch(), branch(),
            gate(0), gate(1), gate(2),
            pl.BlockSpec((N_BRANCH, d), lambda i: (0, 0)),
            pl.BlockSpec((N_BRANCH, c, d), lambda i: (0, 0, 0), pipeline_mode=once),
            pl.BlockSpec((d, d), lambda i: (0, 0), pipeline_mode=once),
            pl.BlockSpec((1, d), lambda i: (0, 0)),
        ],
        out_specs=pl.BlockSpec((tm, d), lambda i: (i, 0)),
        out_shape=jax.ShapeDtypeStruct((n, d), F32),
        compiler_params=_params(("arbitrary",)),
        name="merge",
    )(x, ya, yb, yc, z, z, z, b_gate, wb, wo, pg.reshape(1, d))


def _ffn_kernel(x_ref, g_ref, wu_ref, wv_ref, cw_ref, cb_ref, fc0_ref, wd_ref, pg_ref, *rest,
                nb, T, tf, emit_w, aliased):
    rest = rest[1:] if aliased else rest
    o_ref, fc1_ref = rest[:2]
    xn_ref, acc_ref, hal_ref, car_ref = rest[-4:]
    it = pl.program_id(1)
    j = pl.program_id(2)
    nj = pl.num_programs(2)
    hist = FFN_CONV_W - 1

    @pl.when(j == 0)
    def _():
        x = x_ref[...].reshape(nb * T, D_MODEL)
        xn_ref[...] = (x * _rms_scale(x) * g_ref[...]).astype(BF16)
        acc_ref[...] = jnp.zeros_like(acc_ref)

    @pl.when(it == 0)
    def _():
        car_ref[j] = fc0_ref[...]

    wu, wv, wd = (r[...].astype(BF16) for r in (wu_ref, wv_ref, wd_ref))
    if emit_w:
        for out_ref, wb in zip(rest[2:5], (wu, wv, wd)):
            out_ref[...] = wb
    xn = xn_ref[...]
    u = jnp.dot(xn, wu, preferred_element_type=F32)
    v = jnp.dot(xn, wv, preferred_element_type=F32)
    hal_ref[:, HALO - hist:HALO, :] = car_ref[j]
    hal_ref[:, HALO:HALO + T, :] = u.reshape(nb, T, tf)
    cw = cw_ref[...]
    uc = cb_ref[...][None]
    for jj in range(FFN_CONV_W):
        uc = uc + hal_ref[:, HALO - hist + jj:HALO - hist + jj + T, :] * cw[jj:jj + 1][None]
    tail = hal_ref[:, HALO + T - hist:HALO + T, :]
    car_ref[j] = tail
    act = (jax.nn.gelu(uc).reshape(nb * T, tf) * v).astype(BF16)
    acc_ref[...] += jnp.dot(act, wd, preferred_element_type=F32)

    @pl.when(j == nj - 1)
    def _():
        y = acc_ref[...]
        out = x_ref[...].reshape(nb * T, D_MODEL) + y * _rms_scale(y) * pg_ref[...]
        o_ref[...] = out.reshape(nb, T, D_MODEL)

    @pl.when((j == nj - 1) & (it == pl.num_programs(1) - 1))
    def _():
        for jj in range(FFN_DIM // tf):
            fc1_ref[:, :, jj * tf:(jj + 1) * tf] = car_ref[jj]


def _ffn(x3, g, wu, wv, v_off, cw, cb, fc0, w_down, pg, nb, T, tf, tiles=None, emit_w=False, into=None):
    B, L, d = x3.shape
    F = FFN_DIM
    nj = F // tf
    hist = FFN_CONV_W - 1
    t0, t1 = tiles if tiles is not None else (0, L // T)
    assert not emit_w or (t1 - t0 == 1 and B == nb)
    rows = lambda: pl.BlockSpec((nb, T, d), lambda b, t, j: (b, t + t0, 0))
    in_specs = [
        rows(),
        pl.BlockSpec((1, d), lambda b, t, j: (0, 0)),
        pl.BlockSpec((d, tf), lambda b, t, j: (0, j)),
        pl.BlockSpec((d, tf), lambda b, t, j: (0, v_off // tf + j)),
        pl.BlockSpec((FFN_CONV_W, tf), lambda b, t, j: (0, j)),
        pl.BlockSpec((1, tf), lambda b, t, j: (0, j)),
        pl.BlockSpec((nb, hist, tf), lambda b, t, j: (b, 0, j)),
        pl.BlockSpec((tf, d), lambda b, t, j: (j, 0)),
        pl.BlockSpec((1, d), lambda b, t, j: (0, 0)),
    ]
    args = [x3, g.reshape(1, d), wu, wv, cw, cb.reshape(1, F), fc0, w_down, pg.reshape(1, d)]
    out_specs = [rows(), pl.BlockSpec((nb, hist, F), lambda b, t, j: (b, 0, 0))]
    out_shape = [jax.ShapeDtypeStruct((B, L, d), F32), jax.ShapeDtypeStruct((B, hist, F), F32)]
    if into is not None:
        in_specs.append(pl.BlockSpec(memory_space=pl.ANY))
        args.append(into)
    if emit_w:
        out_specs += [pl.BlockSpec((d, tf), lambda b, t, j: (0, j)),
                      pl.BlockSpec((d, tf), lambda b, t, j: (0, j)),
                      pl.BlockSpec((tf, d), lambda b, t, j: (j, 0))]
        out_shape += [jax.ShapeDtypeStruct((d, F), BF16), jax.ShapeDtypeStruct((d, F), BF16),
                      jax.ShapeDtypeStruct((F, d), BF16)]
    res = pl.pallas_call(
        functools.partial(_ffn_kernel, nb=nb, T=T, tf=tf, emit_w=emit_w, aliased=into is not None),
        grid=(B // nb, t1 - t0, nj),
        in_specs=in_specs,
        out_specs=out_specs,
        out_shape=out_shape,
        scratch_shapes=[
            pltpu.VMEM((nb * T, d), BF16),
            pltpu.VMEM((nb * T, d), F32),
            pltpu.VMEM((nb, HALO + T, tf), F32),
            pltpu.VMEM((nj, nb, hist, tf), F32),
        ],
        input_output_aliases={9: 0} if into is not None else {},
        compiler_params=_params(("arbitrary", "arbitrary", "arbitrary")),
        name="ffn",
    )(*args)
    return res


def _block_diag(w):
    nblk, bi, bj = w.shape
    eye = jnp.eye(nblk, dtype=w.dtype)
    return jnp.einsum("hij,hg->higj", w, eye).reshape(nblk * bi, nblk * bj)


def _tiles(B, L):
    T = min(L, 256)
    return dict(
        proj_tm=min(B * L, 1024), proj_tn=1536,
        rnn_nb=B if B * T <= 512 else 512 // T, rnn_T=T,
        hg_nb=min(B, 4), hg_T=T,
        xa_T=min(L, 512),
        merge_tm=256,
        ffn_nb=B if B * T <= 512 else 512 // T, ffn_T=T, ffn_tf=512,
    )


def _trunk_layer(x, mk, mv, h0, rc0, s0, fc0, w, layer):
    B, L, d = x.shape
    n = B * L
    t = _tiles(B, L)
    x2 = x.reshape(n, d)
    made = {}
    tm, tn = t["proj_tm"], t["proj_tn"]
    if w["w_in"].dtype == F32:
        z, made["w_in"] = _norm_matmul(x2, w["pre_mix_norm"], w["w_in"], tm, tn // 2, "in_proj",
                                       tiles=(0, 1), emit_w=True)
        if n > tm:
            z = _norm_matmul(x2, w["pre_mix_norm"], made["w_in"], tm, tn, "in_proj",
                             tiles=(1, n // tm), into=z)
    else:
        z = _norm_matmul(x2, w["pre_mix_norm"], w["w_in"], tm, tn, "in_proj")
    z3 = z.reshape(B, L, IN_COLS)
    ya, h1, rc1 = _rglru(z3, rc0, h0, w["rnn_conv_w"], w["rnn_conv_b"], w["lru_wa"], w["lru_ba"],
                         w["lru_wx"], w["lru_bx"], w["lru_lambda"], t["rnn_nb"], t["rnn_T"])
    yb, s1 = _hgrn2(z3, s0, w["hg_lb"], w["hg_norm"], layer, t["hg_nb"], t["hg_T"])
    yc = _xattn(z3, mk, mv, t["xa_T"]) if mv is not None else _xattn_cached(z3, *mk, layer)
    c = BRANCH_WIDTH
    x1 = _merge(x2, ya.reshape(n, c), yb.reshape(n, c), yc.reshape(n, c), z, w["b_gate"],
                w["w_branch"], w["w_out"], w["post_mix_norm"], t["merge_tm"]).reshape(B, L, d)
    nb, T, tf = t["ffn_nb"], t["ffn_T"], t["ffn_tf"]
    ffn = functools.partial(_ffn, x1, w["pre_ffn_norm"], cw=w["ffn_conv_w"], cb=w["ffn_conv_b"],
                            pg=w["post_ffn_norm"], nb=nb, T=T)
    if "w_ffn_up" in w:
        xo, fc1, made["w_ffn_u"], made["w_ffn_v"], made["w_ffn_down"] = ffn(
            wu=w["w_ffn_up"], wv=w["w_ffn_up"], v_off=FFN_DIM, fc0=fc0, w_down=w["w_ffn_down"],
            tf=tf // 2, tiles=(0, 1), emit_w=True)
        if L > T:
            xo, fc1 = ffn(wu=made["w_ffn_u"], wv=made["w_ffn_v"], v_off=0, fc0=fc1,
                          w_down=made["w_ffn_down"], tf=tf, tiles=(1, L // T), into=xo)
    else:
        xo, fc1 = ffn(wu=w["w_ffn_u"], wv=w["w_ffn_v"], v_off=0, fc0=fc0, w_down=w["w_ffn_down"], tf=tf)
    return xo, h1, rc1, s1, fc1, made


def kernel(x_prompt, x_sample, cache_mem_k, cache_mem_v, state_rnn_h, state_rnn_conv, state_hg,
           state_ffn_conv, mem_prompt, pre_mix_norm, w_in, rnn_conv_w, rnn_conv_b, lru_wa, lru_ba,
           lru_wx, lru_bx, lru_lambda, hg_lb, hg_norm, mem_norm, w_mem_kv, w_branch, b_gate, w_out,
           post_mix_norm, pre_ffn_norm, w_ffn_up, ffn_conv_w, ffn_conv_b, w_ffn_down, post_ffn_norm):
    depth = w_in.shape[0]
    Bp = x_prompt.shape[0]
    Bs = x_sample.shape[0]
    xa_w = XA_HEADS * XA_HD
    yp, ys = x_prompt, x_sample
    outs = [[] for _ in range(10)]
    for l in range(depth):
        w = {
            "pre_mix_norm": pre_mix_norm[l], "w_in": w_in[l],
            "rnn_conv_w": rnn_conv_w[l], "rnn_conv_b": rnn_conv_b[l],
            "lru_wa": _block_diag(lru_wa[l]).astype(BF16), "lru_ba": lru_ba[l],
            "lru_wx": _block_diag(lru_wx[l]).astype(BF16), "lru_bx": lru_bx[l],
            "lru_lambda": lru_lambda[l], "hg_lb": hg_lb, "hg_norm": hg_norm[l],
            "w_branch": w_branch[l].astype(BF16), "b_gate": b_gate[l],
            "w_out": w_out[l].astype(BF16), "post_mix_norm": post_mix_norm[l],
            "pre_ffn_norm": pre_ffn_norm[l], "w_ffn_up": w_ffn_up[l],
            "ffn_conv_w": ffn_conv_w[l], "ffn_conv_b": ffn_conv_b[l],
            "w_ffn_down": w_ffn_down[l], "post_ffn_norm": post_ffn_norm[l],
        }
        mem2 = mem_prompt.reshape(Bp * MEM_LEN, D_MODEL)
        kv = _norm_matmul(mem2, mem_norm[l], w_mem_kv[l], Bp * MEM_LEN, 1024, "mem_kv")
        kv = kv.reshape(Bp, MEM_LEN, 2 * xa_w)
        mk_p, mv_p = kv[..., :xa_w], kv[..., xa_w:]
        yp, h_p, rc_p, s_p, fc_p, made = _trunk_layer(
            yp, mk_p, mv_p, jnp.zeros((Bp, RNN_WIDTH), F32),
            jnp.zeros((Bp, RNN_CONV_W - 1, RNN_WIDTH), F32),
            jnp.zeros((Bp, HG_HEADS, HG_DK, HG_DV), F32),
            jnp.zeros((Bp, FFN_CONV_W - 1, FFN_DIM), F32), w, l)
        w = {k: v for k, v in w.items() if k != "w_ffn_up"} | made
        ys, h_s, rc_s, s_s, fc_s, _ = _trunk_layer(
            ys, (cache_mem_k, cache_mem_v), None,
            state_rnn_h[l], state_rnn_conv[l], state_hg[l], state_ffn_conv[l], w, l)
        layer_out = (mk_p.reshape(Bp, MEM_LEN, XA_HEADS, XA_HD), mv_p.reshape(Bp, MEM_LEN, XA_HEADS, XA_HD),
                     h_p, rc_p, s_p, fc_p, h_s, rc_s, s_s, fc_s)
        for acc, val in zip(outs, layer_out):
            acc.append(val)
    return (yp, ys) + tuple(jnp.stack(o) for o in outs)
```

```python
import functools

import jax
import jax.numpy as jnp
from jax import lax
from jax.experimental import pallas as pl
from jax.experimental.pallas import tpu as pltpu

F32 = jnp.float32
BF16 = jnp.bfloat16

D_MODEL = 2048
RNN_WIDTH = 1024
RNN_BLOCKS = 16
RNN_CONV_W = 4
LRU_C = 8.0
HG_HEADS = 8
HG_DK = 128
HG_DV = 128
HG_CHUNK = 128
HG_SUB_MAX = 16
SUBLANES = 8
MXU_COLS = 256
HG_SKEW = 8
MEM_LEN = 256
XA_HEADS = 4
XA_HD = 256
BRANCH_WIDTH = 1024
N_BRANCH = 3
FFN_DIM = 5632
FFN_CONV_W = 3
FFN_OUT_ROWS = 16
EPS = 1e-6
LOG2E = 1.4426950408889634
IN_COLS = 6 * BRANCH_WIDTH + N_BRANCH * D_MODEL

COL_RNN, COL_HQ, COL_HF, COL_HI, COL_HO, COL_XQ = range(6)
GATE_COL0 = 3

HALO = 8
VMEM_LIMIT = 56 * 1024 * 1024


def _params(sem):
    return pltpu.CompilerParams(dimension_semantics=sem, vmem_limit_bytes=VMEM_LIMIT)


def _rms_scale(x):
    return lax.rsqrt(jnp.mean(x * x, axis=-1, keepdims=True) + EPS)


_sigmoid = jax.nn.sigmoid


def _sigmoid_tanh(x):
    return 0.5 * jnp.tanh(0.5 * x) + 0.5


def _norm_matmul_kernel(x_ref, g_ref, w_ref, *rest, emit_w, aliased):
    rest = rest[1:] if aliased else rest
    o_ref, xn_ref = rest[0], rest[-1]

    @pl.when(pl.program_id(1) == 0)
    def _():
        x = x_ref[...]
        xn_ref[...] = (x * _rms_scale(x) * g_ref[...]).astype(BF16)

    w = w_ref[...].astype(BF16)
    if emit_w:
        rest[1][...] = w
    o_ref[...] = jnp.dot(xn_ref[...], w, preferred_element_type=F32)


def _norm_matmul(x, g, w, tm, tn, name, tiles=None, emit_w=False, into=None):
    n, d = x.shape
    c = w.shape[1]
    t0, t1 = tiles if tiles is not None else (0, n // tm)
    assert not emit_w or t1 - t0 == 1
    in_specs = [
        pl.BlockSpec((tm, d), lambda i, j: (i + t0, 0)),
        pl.BlockSpec((1, d), lambda i, j: (0, 0)),
        pl.BlockSpec((d, tn), lambda i, j: (0, j)),
    ]
    args = [x, g.reshape(1, d), w]
    out_specs = [pl.BlockSpec((tm, tn), lambda i, j: (i + t0, j))]
    out_shape = [jax.ShapeDtypeStruct((n, c), F32)]
    if into is not None:
        in_specs.append(pl.BlockSpec(memory_space=pl.ANY))
        args.append(into)
    if emit_w:
        out_specs.append(pl.BlockSpec((d, tn), lambda i, j: (0, j)))
        out_shape.append(jax.ShapeDtypeStruct((d, c), BF16))
    res = pl.pallas_call(
        functools.partial(_norm_matmul_kernel, emit_w=emit_w, aliased=into is not None),
        grid=(t1 - t0, c // tn),
        in_specs=in_specs,
        out_specs=out_specs,
        out_shape=out_shape,
        scratch_shapes=[pltpu.VMEM((tm, d), BF16)],
        input_output_aliases={3: 0} if into is not None else {},
        compiler_params=_params(("arbitrary", "arbitrary")),
        name=name,
    )(*args)
    return res if emit_w else res[0]


def _rglru_kernel(z_ref, rc0_ref, h0_ref, cw_ref, cb_ref, wa_ref, ba_ref, wx_ref, bx_ref, lam_ref,
                  ya_ref, h1_ref, rc1_ref, xp_ref, a_ref, b_ref, hs_ref, hc_ref, *, nb, T):
    it = pl.program_id(1)
    nt = pl.num_programs(1)
    hist = RNN_CONV_W - 1

    @pl.when(it == 0)
    def _():
        xp_ref[:, HALO - hist:HALO, :] = rc0_ref[...]
        hc_ref[...] = h0_ref[...]

    xp_ref[:, HALO:HALO + T, :] = z_ref[...]
    cw = cw_ref[...]
    xr = cb_ref[...][None]
    for j in range(RNN_CONV_W):
        xr = xr + xp_ref[:, HALO - hist + j:HALO - hist + j + T, :] * cw[j:j + 1][None]
    xp_ref[:, HALO - hist:HALO, :] = xp_ref[:, HALO + T - hist:HALO + T, :]

    xr2 = xr.reshape(nb * T, RNN_WIDTH)
    xb = xr2.astype(BF16)
    def gate(w_ref, bias_ref):
        cols = [jnp.dot(xb[:, g * MXU_COLS:(g + 1) * MXU_COLS], w_ref[g], preferred_element_type=F32)
                for g in range(RNN_WIDTH // MXU_COLS)]
        return _sigmoid_tanh(jnp.concatenate(cols, axis=1) + bias_ref[...])

    r = gate(wa_ref, ba_ref)
    ig = gate(wx_ref, bx_ref)
    nl = -lam_ref[...]
    softplus = jnp.maximum(nl, 0.0) + jnp.log1p(jnp.exp(-jnp.abs(nl)))
    log_a = -LRU_C * r * softplus
    a = jnp.exp(log_a)
    a_ref[...] = a.reshape(nb, T, RNN_WIDTH)
    one_minus_a2 = -jnp.tanh(log_a) * (a * a + 1.0)
    b_ref[...] = (jnp.sqrt(one_minus_a2) * (ig * xr2)).reshape(nb, T, RNN_WIDTH)

    hs = [hc_ref[bi:bi + 1, :] for bi in range(nb)]
    for t in range(T):
        for bi in range(nb):
            hs[bi] = a_ref[bi, t:t + 1, :] * hs[bi] + b_ref[bi, t:t + 1, :]
            hs_ref[bi, t:t + 1, :] = hs[bi]
    for bi in range(nb):
        hc_ref[bi:bi + 1, :] = hs[bi]
    ya_ref[...] = hs_ref[...].astype(BF16)

    @pl.when(it == nt - 1)
    def _():
        h1_ref[...] = hc_ref[...]
        rc1_ref[...] = xp_ref[:, HALO - hist:HALO, :]


def _rglru(z3, rc0, h0, cw, cb, wa, ba, wx, bx, lam, nb, T):
    B, L, _ = z3.shape
    C = RNN_WIDTH
    hist = RNN_CONV_W - 1
    vec = lambda: pl.BlockSpec((1, C), lambda b, t: (0, 0))
    mat = lambda: pl.BlockSpec((C // MXU_COLS, MXU_COLS, MXU_COLS), lambda b, t: (0, 0, 0))
    return pl.pallas_call(
        functools.partial(_rglru_kernel, nb=nb, T=T),
        grid=(B // nb, L // T),
        in_specs=[
            pl.BlockSpec((nb, T, C), lambda b, t: (b, t, COL_RNN)),
            pl.BlockSpec((nb, hist, C), lambda b, t: (b, 0, 0)),
            pl.BlockSpec((nb, C), lambda b, t: (b, 0)),
            pl.BlockSpec((RNN_CONV_W, C), lambda b, t: (0, 0)),
            vec(), mat(), vec(), mat(), vec(), vec(),
        ],
        out_specs=[
            pl.BlockSpec((nb, T, C), lambda b, t: (b, t, 0)),
            pl.BlockSpec((nb, C), lambda b, t: (b, 0)),
            pl.BlockSpec((nb, hist, C), lambda b, t: (b, 0, 0)),
        ],
        out_shape=[
            jax.ShapeDtypeStruct((B, L, C), BF16),
            jax.ShapeDtypeStruct((B, C), F32),
            jax.ShapeDtypeStruct((B, hist, C), F32),
        ],
        scratch_shapes=[
            pltpu.VMEM((nb, HALO + T, C), F32),
            pltpu.VMEM((nb, T, C), F32),
            pltpu.VMEM((nb, T, C), F32),
            pltpu.VMEM((nb, T, C), F32),
            pltpu.VMEM((nb, C), F32),
        ],
        compiler_params=_params(("arbitrary", "arbitrary")),
        name="rglru",
    )(z3, rc0, h0, cw, cb.reshape(1, C), wa, ba.reshape(1, C), wx, bx.reshape(1, C), lam.reshape(1, C))


def _dot_nt(a, b):
    return lax.dot_general(a, b, (((1,), (1,)), ((), ())), preferred_element_type=F32)


def _split3(x):
    hi = x.astype(BF16)
    r = x - hi.astype(F32)
    mid = r.astype(BF16)
    lo = (r - mid.astype(F32)).astype(BF16)
    return jnp.concatenate([hi, mid, lo], axis=0)


def _hg_sub_rows(cs):
    return HG_SUB_MAX if cs >= 4 * HG_SUB_MAX else SUBLANES


def _hgrn2_scores(qh, bc, ck, ck_row, st, cs):
    o = _dot_nt((qh * jnp.exp2(bc)).astype(BF16), st.astype(BF16))

    sub_rows = _hg_sub_rows(cs)
    nblk = cs // sub_rows
    a_rows = [None] * nblk
    half = cs // 2
    while half >= sub_rows:
        npair = cs // (2 * half)
        q_parts, k_parts = [], []
        for p in range(npair):
            lo, mid, hi = 2 * half * p, 2 * half * p + half, 2 * half * (p + 1)
            br = bc[mid - 1:mid]
            q_parts.append(qh[mid:hi] * jnp.exp2(bc[mid:hi] - br))
            k_parts += [jnp.exp2(br - ck[lo:mid]), jnp.zeros((half, HG_DK), F32)]
        off = _dot_nt(jnp.concatenate(q_parts, axis=0).astype(BF16),
                      jnp.concatenate(k_parts, axis=0).astype(BF16))
        if npair > 1:
            lg = half.bit_length() - 1
            rowp = lax.shift_right_logical(lax.broadcasted_iota(jnp.int32, off.shape, 0), lg)
            colp = lax.shift_right_logical(lax.broadcasted_iota(jnp.int32, off.shape, 1), lg + 1)
            off = jnp.where(rowp == colp, off, 0.0)
        for p in range(npair):
            for sub in range(half // sub_rows):
                blk = (2 * half * p + half) // sub_rows + sub
                piece = off[p * half + sub * sub_rows:p * half + (sub + 1) * sub_rows]
                a_rows[blk] = piece if a_rows[blk] is None else a_rows[blk] + piece
        half //= 2

    ones = jnp.ones((HG_DK, HG_DK), BF16)
    zero = jnp.zeros((SUBLANES, HG_DK), F32)
    groups = sub_rows // SUBLANES
    lane_sums = []
    for blk in range(nblk):
        r0 = blk * sub_rows
        bg = [bc[r0 + v * SUBLANES:r0 + (v + 1) * SUBLANES] for v in range(groups)]
        qg = [qh[r0 + v * SUBLANES:r0 + (v + 1) * SUBLANES] for v in range(groups)]
        pair = []
        for s in range(sub_rows):
            cks = ck_row(r0 + s)
            pair += [qg[v] * jnp.exp2(bg[v] - cks) if v >= s // SUBLANES else zero
                     for v in range(groups)]
        lane_sums.append(jnp.dot(jnp.concatenate(pair, axis=0).astype(BF16), ones,
                                 preferred_element_type=F32))
    return o, a_rows, lane_sums


def _hgrn2_output(scores, vh, ogh, gnh, bl, ck, st, cs):
    o, a_rows, lane_sums = scores
    sub_rows = _hg_sub_rows(cs)
    groups = sub_rows // SUBLANES
    lane = lax.broadcasted_iota(jnp.int32, (SUBLANES, HG_DK), 1)
    tloc = lax.broadcasted_iota(jnp.int32, (SUBLANES, HG_DK), 0)
    mask = [[(lane == s) & (tloc >= s - v * SUBLANES) if s > v * SUBLANES else (lane == s)
             for v in range(groups)] for s in range(sub_rows)]
    a_rows = list(a_rows)
    for blk, lane_sum in enumerate(lane_sums):
        a_g = [jnp.zeros((SUBLANES, HG_DK), F32)] * groups
        for s in range(sub_rows):
            for v in range(s // SUBLANES, groups):
                r = s * sub_rows + v * SUBLANES
                a_g[v] = jnp.where(mask[s][v], lane_sum[r:r + SUBLANES], a_g[v])
        a_d = jnp.concatenate(a_g, axis=0)
        if blk:
            a_d = pltpu.roll(a_d, blk * sub_rows, 1)
        a_d = a_d[:, :cs]
        a_rows[blk] = a_d if a_rows[blk] is None else a_rows[blk] + a_d
    a = jnp.concatenate(a_rows, axis=0)
    o = o + jnp.dot(a.astype(BF16), vh.astype(BF16), preferred_element_type=F32)
    kd = jnp.exp2(bl - ck).astype(BF16)
    st_new = st * jnp.exp2(bl) + jnp.dot(vh.T.astype(BF16), kd, preferred_element_type=F32)
    y = o * _rms_scale(o) * gnh * _sigmoid(ogh)
    return y, st_new


def _hgrn2_kernel(q_ref, f_ref, v_ref, og_ref, s0_ref, lbp_ref, gn_ref, yb_ref, s1_ref,
                  lb_ref, bc_ref, ck_ref, *st_refs, nb, T, cs, layer):
    it = pl.program_id(1)
    nt = pl.num_programs(1)
    row = lax.broadcasted_iota(jnp.int32, (cs, 3 * cs), 0)
    col = lax.broadcasted_iota(jnp.int32, (cs, 3 * cs), 1)
    tri3 = (row >= (col & (cs - 1))).astype(BF16)

    @pl.when(it == 0)
    def _():
        p = lbp_ref[...]
        e = jnp.exp(p - jnp.max(p, axis=0, keepdims=True))
        sm = e / jnp.sum(e, axis=0, keepdims=True)
        lb_ref[...] = jnp.sum(sm[:layer + 1], axis=0, keepdims=True)

        def init(b, c):
            for h in range(HG_HEADS):
                st_refs[h][b] = s0_ref[b, h].T
            return c
        lax.fori_loop(0, nb, init, 0)

    nch = T // cs

    def chunk(idx, c):
        b = idx // nch
        rows = pl.ds(pl.multiple_of((idx % nch) * cs, cs), cs)
        lb = lb_ref[...]
        f = lb + (1.0 - lb) * _sigmoid(f_ref[b, rows, :])
        bc = jnp.dot(tri3, _split3(jnp.log(f) * LOG2E), preferred_element_type=F32)
        bc_ref[...] = bc
        ck_ref[...] = bc - jnp.log(1.0 - f) * LOG2E

        def scores(h):
            hl = slice(h * HG_DK, (h + 1) * HG_DK)
            return _hgrn2_scores(q_ref[b, rows, hl], bc_ref[:, hl], ck_ref[:, hl],
                                 lambda r: ck_ref[r:r + 1, hl], st_refs[h][b], cs)

        def output(h, sc):
            hl = slice(h * HG_DK, (h + 1) * HG_DK)
            y, st_new = _hgrn2_output(sc, v_ref[b, rows, hl], og_ref[b, rows, hl], gn_ref[...],
                                      bc_ref[cs - 1:cs, hl], ck_ref[:, hl], st_refs[h][b], cs)
            yb_ref[b, rows, hl] = y.astype(BF16)
            st_refs[h][b] = st_new

        pending = {}
        for h in range(HG_HEADS + HG_SKEW):
            if h < HG_HEADS:
                pending[h] = scores(h)
            if h >= HG_SKEW:
                output(h - HG_SKEW, pending.pop(h - HG_SKEW))
        return c
    lax.fori_loop(0, nb * nch, chunk, 0)

    @pl.when(it == nt - 1)
    def _():
        def fin(b, c):
            for h in range(HG_HEADS):
                s1_ref[b, h] = st_refs[h][b].T
            return c
        lax.fori_loop(0, nb, fin, 0)


def _hgrn2(z3, s0, hg_lb, gn, layer, nb, T):
    B, L, _ = z3.shape
    C = HG_HEADS * HG_DK
    cs = min(HG_CHUNK, L)
    zcol = lambda col: pl.BlockSpec((nb, T, C), lambda b, t: (b, t, col))
    st_spec = lambda: pl.BlockSpec((nb, HG_HEADS, HG_DK, HG_DV), lambda b, t: (b, 0, 0, 0))
    nl = hg_lb.shape[0]
    return pl.pallas_call(
        functools.partial(_hgrn2_kernel, nb=nb, T=T, cs=cs, layer=layer),
        grid=(B // nb, L // T),
        in_specs=[
            zcol(COL_HQ), zcol(COL_HF), zcol(COL_HI), zcol(COL_HO),
            st_spec(),
            pl.BlockSpec((nl, C), lambda b, t: (0, 0)),
            pl.BlockSpec((1, HG_DV), lambda b, t: (0, 0)),
        ],
        out_specs=[pl.BlockSpec((nb, T, C), lambda b, t: (b, t, 0)), st_spec()],
        out_shape=[
            jax.ShapeDtypeStruct((B, L, C), BF16),
            jax.ShapeDtypeStruct((B, HG_HEADS, HG_DK, HG_DV), F32),
        ],
        scratch_shapes=[
            pltpu.VMEM((1, C), F32),
            pltpu.VMEM((cs, C), F32),
            pltpu.VMEM((cs, C), F32),
        ] + [pltpu.VMEM((nb, HG_DV, HG_DK), F32) for _ in range(HG_HEADS)],
        compiler_params=_params(("arbitrary", "arbitrary")),
        name="hgrn2",
    )(z3, z3, z3, z3, s0, hg_lb, gn.reshape(1, HG_DV))


def _softmax_rows(s):
    e = jnp.exp(s - jnp.max(s, axis=-1, keepdims=True))
    return e / jnp.sum(e, axis=-1, keepdims=True)


def _xattn_kernel(q_ref, k_ref, v_ref, o_ref):
    hls = [slice(h * XA_HD, (h + 1) * XA_HD) for h in range(XA_HEADS)]
    s = [_dot_nt(q_ref[0, :, hl].astype(BF16), k_ref[0, :, hl].astype(BF16)) * (XA_HD ** -0.5)
         for hl in hls]
    p = [_softmax_rows(sh).astype(BF16) for sh in s]
    for hl, ph in zip(hls, p):
        o = jnp.dot(ph, v_ref[0, :, hl].astype(BF16), preferred_element_type=F32)
        o_ref[0, :, hl] = o.astype(BF16)


def _xattn(z3, mk, mv, T):
    B, L, _ = z3.shape
    C = XA_HEADS * XA_HD
    mem = lambda: pl.BlockSpec((1, MEM_LEN, C), lambda b, t: (b, 0, 0))
    return pl.pallas_call(
        _xattn_kernel,
        grid=(B, L // T),
        in_specs=[pl.BlockSpec((1, T, C), lambda b, t: (b, t, COL_XQ)), mem(), mem()],
        out_specs=pl.BlockSpec((1, T, C), lambda b, t: (b, t, 0)),
        out_shape=jax.ShapeDtypeStruct((B, L, C), BF16),
        compiler_params=_params(("arbitrary", "arbitrary")),
        name="xattn",
    )(z3, mk, mv)


def _xattn_cached_kernel(q_ref, k_ref, v_ref, o_ref):
    kt = jnp.swapaxes(k_ref[...], 0, 1)
    vt = jnp.swapaxes(v_ref[...], 0, 1)
    hls = [slice(h * XA_HD, (h + 1) * XA_HD) for h in range(XA_HEADS)]
    s = [_dot_nt(q_ref[0, :, hl].astype(BF16), kt[h].astype(BF16)) * (XA_HD ** -0.5)
         for h, hl in enumerate(hls)]
    p = [_softmax_rows(sh).astype(BF16) for sh in s]
    for h, hl in enumerate(hls):
        o = jnp.dot(p[h], vt[h].astype(BF16), preferred_element_type=F32)
        o_ref[0, :, hl] = o.astype(BF16)


def _xattn_cached(z3, cache_k, cache_v, layer):
    B, L, _ = z3.shape
    C = XA_HEADS * XA_HD
    mem = lambda: pl.BlockSpec((None, None, MEM_LEN, XA_HEADS, XA_HD), lambda b: (layer, b, 0, 0, 0))
    return pl.pallas_call(
        _xattn_cached_kernel,
        grid=(B,),
        in_specs=[pl.BlockSpec((1, L, C), lambda b: (b, 0, COL_XQ)), mem(), mem()],
        out_specs=pl.BlockSpec((1, L, C), lambda b: (b, 0, 0)),
        out_shape=jax.ShapeDtypeStruct((B, L, C), BF16),
        compiler_params=_params(("arbitrary",)),
        name="xattn_cached",
    )(z3, cache_k, cache_v)


def _merge_kernel(x_ref, ya_ref, yb_ref, yc_ref, g0_ref, g1_ref, g2_ref, bg_ref, wb_ref, wo_ref,
                  pg_ref, o_ref):
    m = jnp.zeros(o_ref.shape, F32)
    for nb, (y_ref, g_ref) in enumerate(((ya_ref, g0_ref), (yb_ref, g1_ref), (yc_ref, g2_ref))):
        gate = _sigmoid(g_ref[...] + bg_ref[nb:nb + 1, :])
        m = m + gate * jnp.dot(y_ref[...], wb_ref[nb], preferred_element_type=F32)
    y = jnp.dot(m.astype(BF16), wo_ref[...], preferred_element_type=F32)
    o_ref[...] = x_ref[...] + y * _rms_scale(y) * pg_ref[...]


def _merge(x, ya, yb, yc, z, b_gate, wb, wo, pg, tm):
    n, d = x.shape
    c = BRANCH_WIDTH
    once = pl.Buffered(1)
    branch = lambda: pl.BlockSpec((tm, c), lambda i: (i, 0))
    gate = lambda nb: pl.BlockSpec((tm, d), lambda i: (i, GATE_COL0 + nb))
    return pl.pallas_call(
        _merge_kernel,
        grid=(n // tm,),
        in_specs=[
            pl.BlockSpec((tm, d), lambda i: (i, 0)),
            branch(), branch(), branch(),
            gate(0), gate(1), gate(2),
            pl.BlockSpec((N_BRANCH, d), lambda i: (0, 0)),
            pl.BlockSpec((N_BRANCH, c, d), lambda i: (0, 0, 0), pipeline_mode=once),
            pl.BlockSpec((d, d), lambda i: (0, 0), pipeline_mode=once),
            pl.BlockSpec((1, d), lambda i: (0, 0)),
        ],
        out_specs=pl.BlockSpec((tm, d), lambda i: (i, 0)),
        out_shape=jax.ShapeDtypeStruct((n, d), F32),
        compiler_params=_params(("arbitrary",)),
        name="merge",
    )(x, ya, yb, yc, z, z, z, b_gate, wb, wo, pg.reshape(1, d))


def _ffn_kernel(x_ref, g_ref, wu_ref, wv_ref, cw_ref, cb_ref, fc0_ref, wd_ref, pg_ref, *rest,
                nb, T, tf, emit_w, aliased):
    rest = rest[1:] if aliased else rest
    o_ref, fc1_ref = rest[:2]
    xn_ref, acc_ref, hal_ref, car_ref = rest[-4:]
    it = pl.program_id(1)
    j = pl.program_id(2)
    nj = pl.num_programs(2)
    hist = FFN_CONV_W - 1

    @pl.when(j == 0)
    def _():
        x = x_ref[...].reshape(nb * T, D_MODEL)
        xn_ref[...] = (x * _rms_scale(x) * g_ref[...]).astype(BF16)
        acc_ref[...] = jnp.zeros_like(acc_ref)

    @pl.when(it == 0)
    def _():
        car_ref[j] = fc0_ref[...]

    wu, wv, wd = (r[...].astype(BF16) for r in (wu_ref, wv_ref, wd_ref))
    if emit_w:
        for out_ref, wb in zip(rest[2:5], (wu, wv, wd)):
            out_ref[...] = wb
    xn = xn_ref[...]
    u = jnp.dot(xn, wu, preferred_element_type=F32)
    v = jnp.dot(xn, wv, preferred_element_type=F32)
    hal_ref[:, HALO - hist:HALO, :] = car_ref[j]
    hal_ref[:, HALO:HALO + T, :] = u.reshape(nb, T, tf)
    cw = cw_ref[...]
    uc = cb_ref[...][None]
    for jj in range(FFN_CONV_W):
        uc = uc + hal_ref[:, HALO - hist + jj:HALO - hist + jj + T, :] * cw[jj:jj + 1][None]
    tail = hal_ref[:, HALO + T - hist:HALO + T, :]
    car_ref[j] = tail
    act = (jax.nn.gelu(uc).reshape(nb * T, tf) * v).astype(BF16)
    acc_ref[...] += jnp.dot(act, wd, preferred_element_type=F32)

    @pl.when(j == nj - 1)
    def _():
        pg = pg_ref[...]
        for b in range(nb):
            for r in range(0, T, FFN_OUT_ROWS):
                y = acc_ref[b * T + r:b * T + r + FFN_OUT_ROWS, :]
                o_ref[b, r:r + FFN_OUT_ROWS, :] = x_ref[b, r:r + FFN_OUT_ROWS, :] + y * _rms_scale(y) * pg

    @pl.when((j == nj - 1) & (it == pl.num_programs(1) - 1))
    def _():
        for jj in range(FFN_DIM // tf):
            fc1_ref[:, :, jj * tf:(jj + 1) * tf] = car_ref[jj]


def _ffn(x3, g, wu, wv, v_off, cw, cb, fc0, w_down, pg, nb, T, tf, tiles=None, emit_w=False, into=None):
    B, L, d = x3.shape
    F = FFN_DIM
    nj = F // tf
    hist = FFN_CONV_W - 1
    t0, t1 = tiles if tiles is not None else (0, L // T)
    assert not emit_w or (t1 - t0 == 1 and B == nb)
    rows = lambda: pl.BlockSpec((nb, T, d), lambda b, t, j: (b, t + t0, 0))
    in_specs = [
        rows(),
        pl.BlockSpec((1, d), lambda b, t, j: (0, 0)),
        pl.BlockSpec((d, tf), lambda b, t, j: (0, j)),
        pl.BlockSpec((d, tf), lambda b, t, j: (0, v_off // tf + j)),
        pl.BlockSpec((FFN_CONV_W, tf), lambda b, t, j: (0, j)),
        pl.BlockSpec((1, tf), lambda b, t, j: (0, j)),
        pl.BlockSpec((nb, hist, tf), lambda b, t, j: (b, 0, j)),
        pl.BlockSpec((tf, d), lambda b, t, j: (j, 0)),
        pl.BlockSpec((1, d), lambda b, t, j: (0, 0)),
    ]
    args = [x3, g.reshape(1, d), wu, wv, cw, cb.reshape(1, F), fc0, w_down, pg.reshape(1, d)]
    out_specs = [rows(), pl.BlockSpec((nb, hist, F), lambda b, t, j: (b, 0, 0))]
    out_shape = [jax.ShapeDtypeStruct((B, L, d), F32), jax.ShapeDtypeStruct((B, hist, F), F32)]
    if into is not None:
        in_specs.append(pl.BlockSpec(memory_space=pl.ANY))
        args.append(into)
    if emit_w:
        out_specs += [pl.BlockSpec((d, tf), lambda b, t, j: (0, j)),
                      pl.BlockSpec((d, tf), lambda b, t, j: (0, j)),
                      pl.BlockSpec((tf, d), lambda b, t, j: (j, 0))]
        out_shape += [jax.ShapeDtypeStruct((d, F), BF16), jax.ShapeDtypeStruct((d, F), BF16),
                      jax.ShapeDtypeStruct((F, d), BF16)]
    res = pl.pallas_call(
        functools.partial(_ffn_kernel, nb=nb, T=T, tf=tf, emit_w=emit_w, aliased=into is not None),
        grid=(B // nb, t1 - t0, nj),
        in_specs=in_specs,
        out_specs=out_specs,
        out_shape=out_shape,
        scratch_shapes=[
            pltpu.VMEM((nb * T, d), BF16),
            pltpu.VMEM((nb * T, d), F32),
            pltpu.VMEM((nb, HALO + T, tf), F32),
            pltpu.VMEM((nj, nb, hist, tf), F32),
        ],
        input_output_aliases={9: 0} if into is not None else {},
        compiler_params=_params(("arbitrary", "arbitrary", "arbitrary")),
        name="ffn",
    )(*args)
    return res


def _block_diag(w):
    nblk, bi, bj = w.shape
    per = MXU_COLS // bi
    eye = jnp.eye(per, dtype=w.dtype)
    grouped = jnp.einsum("ghij,hk->ghikj", w.reshape(nblk // per, per, bi, bj), eye)
    return grouped.reshape(nblk // per, per * bi, per * bj)


def _tiles(B, L):
    T = min(L, 256)
    return dict(
        proj_tm=min(B * L, 1024), proj_tn=1536,
        rnn_nb=B if B * T <= 512 else 512 // T, rnn_T=T,
        hg_nb=min(B, 4), hg_T=T,
        xa_T=min(L, 512),
        merge_tm=256,
        ffn_nb=B if B * T <= 512 else 512 // T, ffn_T=T, ffn_tf=512,
    )


def _trunk_layer(x, mk, mv, h0, rc0, s0, fc0, w, layer):
    B, L, d = x.shape
    n = B * L
    t = _tiles(B, L)
    x2 = x.reshape(n, d)
    made = {}
    tm, tn = t["proj_tm"], t["proj_tn"]
    if w["w_in"].dtype == F32:
        z, made["w_in"] = _norm_matmul(x2, w["pre_mix_norm"], w["w_in"], tm, tn // 2, "in_proj",
                                       tiles=(0, 1), emit_w=True)
        if n > tm:
            z = _norm_matmul(x2, w["pre_mix_norm"], made["w_in"], tm, tn, "in_proj",
                             tiles=(1, n // tm), into=z)
    else:
        z = _norm_matmul(x2, w["pre_mix_norm"], w["w_in"], tm, tn, "in_proj")
    z3 = z.reshape(B, L, IN_COLS)
    ya, h1, rc1 = _rglru(z3, rc0, h0, w["rnn_conv_w"], w["rnn_conv_b"], w["lru_wa"], w["lru_ba"],
                         w["lru_wx"], w["lru_bx"], w["lru_lambda"], t["rnn_nb"], t["rnn_T"])
    yb, s1 = _hgrn2(z3, s0, w["hg_lb"], w["hg_norm"], layer, t["hg_nb"], t["hg_T"])
    yc = _xattn(z3, mk, mv, t["xa_T"]) if mv is not None else _xattn_cached(z3, *mk, layer)
    c = BRANCH_WIDTH
    x1 = _merge(x2, ya.reshape(n, c), yb.reshape(n, c), yc.reshape(n, c), z, w["b_gate"],
                w["w_branch"], w["w_out"], w["post_mix_norm"], t["merge_tm"]).reshape(B, L, d)
    nb, T, tf = t["ffn_nb"], t["ffn_T"], t["ffn_tf"]
    ffn = functools.partial(_ffn, x1, w["pre_ffn_norm"], cw=w["ffn_conv_w"], cb=w["ffn_conv_b"],
                            pg=w["post_ffn_norm"], nb=nb, T=T)
    if "w_ffn_up" in w:
        xo, fc1, made["w_ffn_u"], made["w_ffn_v"], made["w_ffn_down"] = ffn(
            wu=w["w_ffn_up"], wv=w["w_ffn_up"], v_off=FFN_DIM, fc0=fc0, w_down=w["w_ffn_down"],
            tf=tf // 2, tiles=(0, 1), emit_w=True)
        if L > T:
            xo, fc1 = ffn(wu=made["w_ffn_u"], wv=made["w_ffn_v"], v_off=0, fc0=fc1,
                          w_down=made["w_ffn_down"], tf=tf, tiles=(1, L // T), into=xo)
    else:
        xo, fc1 = ffn(wu=w["w_ffn_u"], wv=w["w_ffn_v"], v_off=0, fc0=fc0, w_down=w["w_ffn_down"], tf=tf)
    return xo, h1, rc1, s1, fc1, made


def kernel(x_prompt, x_sample, cache_mem_k, cache_mem_v, state_rnn_h, state_rnn_conv, state_hg,
           state_ffn_conv, mem_prompt, pre_mix_norm, w_in, rnn_conv_w, rnn_conv_b, lru_wa, lru_ba,
           lru_wx, lru_bx, lru_lambda, hg_lb, hg_norm, mem_norm, w_mem_kv, w_branch, b_gate, w_out,
           post_mix_norm, pre_ffn_norm, w_ffn_up, ffn_conv_w, ffn_conv_b, w_ffn_down, post_ffn_norm):
    depth = w_in.shape[0]
    Bp = x_prompt.shape[0]
    Bs = x_sample.shape[0]
    xa_w = XA_HEADS * XA_HD
    yp, ys = x_prompt, x_sample
    outs = [[] for _ in range(10)]
    for l in range(depth):
        w = {
            "pre_mix_norm": pre_mix_norm[l], "w_in": w_in[l],
            "rnn_conv_w": rnn_conv_w[l], "rnn_conv_b": rnn_conv_b[l],
            "lru_wa": _block_diag(lru_wa[l]).astype(BF16), "lru_ba": lru_ba[l],
            "lru_wx": _block_diag(lru_wx[l]).astype(BF16), "lru_bx": lru_bx[l],
            "lru_lambda": lru_lambda[l], "hg_lb": hg_lb, "hg_norm": hg_norm[l],
            "w_branch": w_branch[l].astype(BF16), "b_gate": b_gate[l],
            "w_out": w_out[l].astype(BF16), "post_mix_norm": post_mix_norm[l],
            "pre_ffn_norm": pre_ffn_norm[l], "w_ffn_up": w_ffn_up[l],
            "ffn_conv_w": ffn_conv_w[l], "ffn_conv_b": ffn_conv_b[l],
            "w_ffn_down": w_ffn_down[l], "post_ffn_norm": post_ffn_norm[l],
        }
        mem2 = mem_prompt.reshape(Bp * MEM_LEN, D_MODEL)
        kv = _norm_matmul(mem2, mem_norm[l], w_mem_kv[l], Bp * MEM_LEN, 1024, "mem_kv")
        kv = kv.reshape(Bp, MEM_LEN, 2 * xa_w)
        mk_p, mv_p = kv[..., :xa_w], kv[..., xa_w:]
        yp, h_p, rc_p, s_p, fc_p, made = _trunk_layer(
            yp, mk_p, mv_p, jnp.zeros((Bp, RNN_WIDTH), F32),
            jnp.zeros((Bp, RNN_CONV_W - 1, RNN_WIDTH), F32),
            jnp.zeros((Bp, HG_HEADS, HG_DK, HG_DV), F32),
            jnp.zeros((Bp, FFN_CONV_W - 1, FFN_DIM), F32), w, l)
        w = {k: v for k, v in w.items() if k != "w_ffn_up"} | made
        ys, h_s, rc_s, s_s, fc_s, _ = _trunk_layer(
            ys, (cache_mem_k, cache_mem_v), None,
            state_rnn_h[l], state_rnn_conv[l], state_hg[l], state_ffn_conv[l], w, l)
        layer_out = (mk_p.reshape(Bp, MEM_LEN, XA_HEADS, XA_HD), mv_p.reshape(Bp, MEM_LEN, XA_HEADS, XA_HD),
                     h_p, rc_p, s_p, fc_p, h_s, rc_s, s_s, fc_s)
        for acc, val in zip(outs, layer_out):
            acc.append(val)
    return (yp, ys) + tuple(jnp.stack(o) for o in outs)
```

```python
import functools

import jax
import jax.numpy as jnp
from jax import lax
from jax.experimental import pallas as pl
from jax.experimental.pallas import tpu as pltpu

F32 = jnp.float32
BF16 = jnp.bfloat16

D_MODEL = 2048
RNN_WIDTH = 1024
RNN_BLOCKS = 16
RNN_CONV_W = 4
LRU_C = 8.0
HG_HEADS = 8
HG_DK = 128
HG_DV = 128
HG_CHUNK = 128
HG_SUB_MAX = 16
SUBLANES = 8
MXU_COLS = 256
HG_SKEW = 8
MEM_LEN = 256
XA_HEADS = 4
XA_HD = 256
BRANCH_WIDTH = 1024
N_BRANCH = 3
FFN_DIM = 5632
FFN_CONV_W = 3
FFN_OUT_ROWS = 16
EPS = 1e-6
LOG2E = 1.4426950408889634
IN_COLS = 6 * BRANCH_WIDTH + N_BRANCH * D_MODEL

COL_RNN, COL_HQ, COL_HF, COL_HI, COL_HO, COL_XQ = range(6)
GATE_COL0 = 3

HALO = 8
VMEM_LIMIT = 56 * 1024 * 1024


def _params(sem):
    return pltpu.CompilerParams(dimension_semantics=sem, vmem_limit_bytes=VMEM_LIMIT)


def _rms_scale(x):
    return lax.rsqrt(jnp.mean(x * x, axis=-1, keepdims=True) + EPS)


_sigmoid = jax.nn.sigmoid


def _sigmoid_tanh(x):
    return 0.5 * jnp.tanh(0.5 * x) + 0.5


def _with_casts(kernel, n_in, n_out, n_cast):
    def wrapped(*refs, **kw):
        ins, refs = refs[:n_in], refs[n_in:]
        cast_in, refs = refs[:n_cast], refs[n_cast:]
        outs, refs = refs[:n_out], refs[n_out:]
        cast_out, scratch = refs[:n_cast], refs[n_cast:]
        for src, dst in zip(cast_in, cast_out):
            dst[...] = src[...].astype(BF16)
        kernel(*ins, *outs, *scratch, **kw)
    return wrapped


def _cast_specs(casts, nsteps, step):
    in_specs, out_specs, out_shape = [], [], []
    for w in casts:
        rows, cols = w.shape[0] // nsteps, w.shape[1]
        assert rows * nsteps == w.shape[0] and rows % (2 * SUBLANES) == 0
        for specs in (in_specs, out_specs):
            specs.append(pl.BlockSpec((rows, cols), lambda *g, step=step: (step(*g), 0)))
        out_shape.append(jax.ShapeDtypeStruct(w.shape, BF16))
    return in_specs, out_specs, out_shape


def _norm_matmul_kernel(x_ref, g_ref, w_ref, *rest, emit_w, aliased):
    rest = rest[1:] if aliased else rest
    o_ref, xn_ref = rest[0], rest[-1]

    @pl.when(pl.program_id(1) == 0)
    def _():
        x = x_ref[...]
        xn_ref[...] = (x * _rms_scale(x) * g_ref[...]).astype(BF16)

    w = w_ref[...].astype(BF16)
    if emit_w:
        rest[1][...] = w
    o_ref[...] = jnp.dot(xn_ref[...], w, preferred_element_type=F32)


def _norm_matmul(x, g, w, tm, tn, name, tiles=None, emit_w=False, into=None):
    n, d = x.shape
    c = w.shape[1]
    t0, t1 = tiles if tiles is not None else (0, n // tm)
    assert not emit_w or t1 - t0 == 1
    in_specs = [
        pl.BlockSpec((tm, d), lambda i, j: (i + t0, 0)),
        pl.BlockSpec((1, d), lambda i, j: (0, 0)),
        pl.BlockSpec((d, tn), lambda i, j: (0, j)),
    ]
    args = [x, g.reshape(1, d), w]
    out_specs = [pl.BlockSpec((tm, tn), lambda i, j: (i + t0, j))]
    out_shape = [jax.ShapeDtypeStruct((n, c), F32)]
    if into is not None:
        in_specs.append(pl.BlockSpec(memory_space=pl.ANY))
        args.append(into)
    if emit_w:
        out_specs.append(pl.BlockSpec((d, tn), lambda i, j: (0, j)))
        out_shape.append(jax.ShapeDtypeStruct((d, c), BF16))
    res = pl.pallas_call(
        functools.partial(_norm_matmul_kernel, emit_w=emit_w, aliased=into is not None),
        grid=(t1 - t0, c // tn),
        in_specs=in_specs,
        out_specs=out_specs,
        out_shape=out_shape,
        scratch_shapes=[pltpu.VMEM((tm, d), BF16)],
        input_output_aliases={3: 0} if into is not None else {},
        compiler_params=_params(("arbitrary", "arbitrary")),
        name=name,
    )(*args)
    return res if emit_w else res[0]


def _rglru_kernel(z_ref, rc0_ref, h0_ref, cw_ref, cb_ref, wa_ref, ba_ref, wx_ref, bx_ref, lam_ref,
                  ya_ref, h1_ref, rc1_ref, xp_ref, a_ref, b_ref, hs_ref, hc_ref, *, nb, T):
    it = pl.program_id(1)
    nt = pl.num_programs(1)
    hist = RNN_CONV_W - 1

    @pl.when(it == 0)
    def _():
        xp_ref[:, HALO - hist:HALO, :] = rc0_ref[...]
        hc_ref[...] = h0_ref[...]

    xp_ref[:, HALO:HALO + T, :] = z_ref[...]
    cw = cw_ref[...]
    xr = cb_ref[...][None]
    for j in range(RNN_CONV_W):
        xr = xr + xp_ref[:, HALO - hist + j:HALO - hist + j + T, :] * cw[j:j + 1][None]
    xp_ref[:, HALO - hist:HALO, :] = xp_ref[:, HALO + T - hist:HALO + T, :]

    xr2 = xr.reshape(nb * T, RNN_WIDTH)
    xb = xr2.astype(BF16)
    def gate(w_ref, bias_ref):
        cols = [jnp.dot(xb[:, g * MXU_COLS:(g + 1) * MXU_COLS], w_ref[g], preferred_element_type=F32)
                for g in range(RNN_WIDTH // MXU_COLS)]
        return _sigmoid_tanh(jnp.concatenate(cols, axis=1) + bias_ref[...])

    r = gate(wa_ref, ba_ref)
    ig = gate(wx_ref, bx_ref)
    nl = -lam_ref[...]
    softplus = jnp.maximum(nl, 0.0) + jnp.log1p(jnp.exp(-jnp.abs(nl)))
    log_a = -LRU_C * r * softplus
    a = jnp.exp(log_a)
    a_ref[...] = a.reshape(nb, T, RNN_WIDTH)
    one_minus_a2 = -jnp.tanh(log_a) * (a * a + 1.0)
    b_ref[...] = (jnp.sqrt(one_minus_a2) * (ig * xr2)).reshape(nb, T, RNN_WIDTH)

    hs = [hc_ref[bi:bi + 1, :] for bi in range(nb)]
    for t in range(T):
        for bi in range(nb):
            hs[bi] = a_ref[bi, t:t + 1, :] * hs[bi] + b_ref[bi, t:t + 1, :]
            hs_ref[bi, t:t + 1, :] = hs[bi]
    for bi in range(nb):
        hc_ref[bi:bi + 1, :] = hs[bi]
    ya_ref[...] = hs_ref[...].astype(BF16)

    @pl.when(it == nt - 1)
    def _():
        h1_ref[...] = hc_ref[...]
        rc1_ref[...] = xp_ref[:, HALO - hist:HALO, :]


def _rglru(z3, rc0, h0, cw, cb, wa, ba, wx, bx, lam, nb, T, casts=()):
    B, L, _ = z3.shape
    C = RNN_WIDTH
    hist = RNN_CONV_W - 1
    vec = lambda: pl.BlockSpec((1, C), lambda b, t: (0, 0))
    mat = lambda: pl.BlockSpec((C // MXU_COLS, MXU_COLS, MXU_COLS), lambda b, t: (0, 0, 0))
    nt = L // T
    c_in, c_out, c_shape = _cast_specs(casts, (B // nb) * nt, lambda b, t: b * nt + t)
    return pl.pallas_call(
        functools.partial(_with_casts(_rglru_kernel, 10, 3, len(casts)), nb=nb, T=T),
        grid=(B // nb, nt),
        in_specs=[
            pl.BlockSpec((nb, T, C), lambda b, t: (b, t, COL_RNN)),
            pl.BlockSpec((nb, hist, C), lambda b, t: (b, 0, 0)),
            pl.BlockSpec((nb, C), lambda b, t: (b, 0)),
            pl.BlockSpec((RNN_CONV_W, C), lambda b, t: (0, 0)),
            vec(), mat(), vec(), mat(), vec(), vec(),
        ] + c_in,
        out_specs=[
            pl.BlockSpec((nb, T, C), lambda b, t: (b, t, 0)),
            pl.BlockSpec((nb, C), lambda b, t: (b, 0)),
            pl.BlockSpec((nb, hist, C), lambda b, t: (b, 0, 0)),
        ] + c_out,
        out_shape=[
            jax.ShapeDtypeStruct((B, L, C), BF16),
            jax.ShapeDtypeStruct((B, C), F32),
            jax.ShapeDtypeStruct((B, hist, C), F32),
        ] + c_shape,
        scratch_shapes=[
            pltpu.VMEM((nb, HALO + T, C), F32),
            pltpu.VMEM((nb, T, C), F32),
            pltpu.VMEM((nb, T, C), F32),
            pltpu.VMEM((nb, T, C), F32),
            pltpu.VMEM((nb, C), F32),
        ],
        compiler_params=_params(("arbitrary", "arbitrary")),
        name="rglru",
    )(z3, rc0, h0, cw, cb.reshape(1, C), wa, ba.reshape(1, C), wx, bx.reshape(1, C), lam.reshape(1, C),
      *casts)


def _dot_nt(a, b):
    return lax.dot_general(a, b, (((1,), (1,)), ((), ())), preferred_element_type=F32)


def _split3(x):
    hi = x.astype(BF16)
    r = x - hi.astype(F32)
    mid = r.astype(BF16)
    lo = (r - mid.astype(F32)).astype(BF16)
    return jnp.concatenate([hi, mid, lo], axis=0)


def _hg_sub_rows(cs):
    return HG_SUB_MAX if cs >= 4 * HG_SUB_MAX else SUBLANES


def _hgrn2_scores(qh, bc, ck, ck_row, st, cs):
    o = _dot_nt((qh * jnp.exp2(bc)).astype(BF16), st.astype(BF16))

    sub_rows = _hg_sub_rows(cs)
    nblk = cs // sub_rows
    a_rows = [None] * nblk
    half = cs // 2
    while half >= sub_rows:
        npair = cs // (2 * half)
        q_parts, k_parts = [], []
        for p in range(npair):
            lo, mid, hi = 2 * half * p, 2 * half * p + half, 2 * half * (p + 1)
            br = bc[mid - 1:mid]
            q_parts.append(qh[mid:hi] * jnp.exp2(bc[mid:hi] - br))
            k_parts += [jnp.exp2(br - ck[lo:mid]), jnp.zeros((half, HG_DK), F32)]
        off = _dot_nt(jnp.concatenate(q_parts, axis=0).astype(BF16),
                      jnp.concatenate(k_parts, axis=0).astype(BF16))
        if npair > 1:
            lg = half.bit_length() - 1
            rowp = lax.shift_right_logical(lax.broadcasted_iota(jnp.int32, off.shape, 0), lg)
            colp = lax.shift_right_logical(lax.broadcasted_iota(jnp.int32, off.shape, 1), lg + 1)
            off = jnp.where(rowp == colp, off, 0.0)
        for p in range(npair):
            for sub in range(half // sub_rows):
                blk = (2 * half * p + half) // sub_rows + sub
                piece = off[p * half + sub * sub_rows:p * half + (sub + 1) * sub_rows]
                a_rows[blk] = piece if a_rows[blk] is None else a_rows[blk] + piece
        half //= 2

    ones = jnp.ones((HG_DK, HG_DK), BF16)
    zero = jnp.zeros((SUBLANES, HG_DK), F32)
    groups = sub_rows // SUBLANES
    lane_sums = []
    for blk in range(nblk):
        r0 = blk * sub_rows
        bg = [bc[r0 + v * SUBLANES:r0 + (v + 1) * SUBLANES] for v in range(groups)]
        qg = [qh[r0 + v * SUBLANES:r0 + (v + 1) * SUBLANES] for v in range(groups)]
        pair = []
        for s in range(sub_rows):
            cks = ck_row(r0 + s)
            pair += [qg[v] * jnp.exp2(bg[v] - cks) if v >= s // SUBLANES else zero
                     for v in range(groups)]
        lane_sums.append(jnp.dot(jnp.concatenate(pair, axis=0).astype(BF16), ones,
                                 preferred_element_type=F32))
    return o, a_rows, lane_sums


def _hgrn2_output(scores, vh, ogh, gnh, bl, ck, st, cs):
    o, a_rows, lane_sums = scores
    sub_rows = _hg_sub_rows(cs)
    groups = sub_rows // SUBLANES
    lane = lax.broadcasted_iota(jnp.int32, (SUBLANES, HG_DK), 1)
    tloc = lax.broadcasted_iota(jnp.int32, (SUBLANES, HG_DK), 0)
    mask = [[(lane == s) & (tloc >= s - v * SUBLANES) if s > v * SUBLANES else (lane == s)
             for v in range(groups)] for s in range(sub_rows)]
    a_rows = list(a_rows)
    for blk, lane_sum in enumerate(lane_sums):
        a_g = [jnp.zeros((SUBLANES, HG_DK), F32)] * groups
        for s in range(sub_rows):
            for v in range(s // SUBLANES, groups):
                r = s * sub_rows + v * SUBLANES
                a_g[v] = jnp.where(mask[s][v], lane_sum[r:r + SUBLANES], a_g[v])
        a_d = jnp.concatenate(a_g, axis=0)
        if blk:
            a_d = pltpu.roll(a_d, blk * sub_rows, 1)
        a_d = a_d[:, :cs]
        a_rows[blk] = a_d if a_rows[blk] is None else a_rows[blk] + a_d
    a = jnp.concatenate(a_rows, axis=0)
    o = o + jnp.dot(a.astype(BF16), vh.astype(BF16), preferred_element_type=F32)
    kd = jnp.exp2(bl - ck).astype(BF16)
    st_new = st * jnp.exp2(bl) + jnp.dot(vh.T.astype(BF16), kd, preferred_element_type=F32)
    y = o * _rms_scale(o) * gnh * _sigmoid(ogh)
    return y, st_new


def _hgrn2_kernel(q_ref, f_ref, v_ref, og_ref, s0_ref, lbp_ref, gn_ref, yb_ref, s1_ref,
                  lb_ref, bc_ref, ck_ref, *st_refs, nb, T, cs, layer):
    it = pl.program_id(1)
    nt = pl.num_programs(1)
    row = lax.broadcasted_iota(jnp.int32, (cs, 3 * cs), 0)
    col = lax.broadcasted_iota(jnp.int32, (cs, 3 * cs), 1)
    tri3 = (row >= (col & (cs - 1))).astype(BF16)

    @pl.when(it == 0)
    def _():
        p = lbp_ref[...]
        e = jnp.exp(p - jnp.max(p, axis=0, keepdims=True))
        sm = e / jnp.sum(e, axis=0, keepdims=True)
        lb_ref[...] = jnp.sum(sm[:layer + 1], axis=0, keepdims=True)

        def init(b, c):
            for h in range(HG_HEADS):
                st_refs[h][b] = s0_ref[b, h].T
            return c
        lax.fori_loop(0, nb, init, 0)

    nch = T // cs

    def chunk(idx, c):
        b = idx // nch
        rows = pl.ds(pl.multiple_of((idx % nch) * cs, cs), cs)
        lb = lb_ref[...]
        f = lb + (1.0 - lb) * _sigmoid(f_ref[b, rows, :])
        bc = jnp.dot(tri3, _split3(jnp.log(f) * LOG2E), preferred_element_type=F32)
        bc_ref[...] = bc
        ck_ref[...] = bc - jnp.log(1.0 - f) * LOG2E

        def scores(h):
            hl = slice(h * HG_DK, (h + 1) * HG_DK)
            return _hgrn2_scores(q_ref[b, rows, hl], bc_ref[:, hl], ck_ref[:, hl],
                                 lambda r: ck_ref[r:r + 1, hl], st_refs[h][b], cs)

        def output(h, sc):
            hl = slice(h * HG_DK, (h + 1) * HG_DK)
            y, st_new = _hgrn2_output(sc, v_ref[b, rows, hl], og_ref[b, rows, hl], gn_ref[...],
                                      bc_ref[cs - 1:cs, hl], ck_ref[:, hl], st_refs[h][b], cs)
            yb_ref[b, rows, hl] = y.astype(BF16)
            st_refs[h][b] = st_new

        pending = {}
        for h in range(HG_HEADS + HG_SKEW):
            if h < HG_HEADS:
                pending[h] = scores(h)
            if h >= HG_SKEW:
                output(h - HG_SKEW, pending.pop(h - HG_SKEW))
        return c
    lax.fori_loop(0, nb * nch, chunk, 0)

    @pl.when(it == nt - 1)
    def _():
        def fin(b, c):
            for h in range(HG_HEADS):
                s1_ref[b, h] = st_refs[h][b].T
            return c
        lax.fori_loop(0, nb, fin, 0)


def _hgrn2(z3, s0, hg_lb, gn, layer, nb, T, casts=()):
    B, L, _ = z3.shape
    C = HG_HEADS * HG_DK
    cs = min(HG_CHUNK, L)
    zcol = lambda col: pl.BlockSpec((nb, T, C), lambda b, t: (b, t, col))
    st_spec = lambda: pl.BlockSpec((nb, HG_HEADS, HG_DK, HG_DV), lambda b, t: (b, 0, 0, 0))
    nl = hg_lb.shape[0]
    nt = L // T
    c_in, c_out, c_shape = _cast_specs(casts, (B // nb) * nt, lambda b, t: b * nt + t)
    return pl.pallas_call(
        functools.partial(_with_casts(_hgrn2_kernel, 7, 2, len(casts)), nb=nb, T=T, cs=cs, layer=layer),
        grid=(B // nb, nt),
        in_specs=[
            zcol(COL_HQ), zcol(COL_HF), zcol(COL_HI), zcol(COL_HO),
            st_spec(),
            pl.BlockSpec((nl, C), lambda b, t: (0, 0)),
            pl.BlockSpec((1, HG_DV), lambda b, t: (0, 0)),
        ] + c_in,
        out_specs=[pl.BlockSpec((nb, T, C), lambda b, t: (b, t, 0)), st_spec()] + c_out,
        out_shape=[
            jax.ShapeDtypeStruct((B, L, C), BF16),
            jax.ShapeDtypeStruct((B, HG_HEADS, HG_DK, HG_DV), F32),
        ] + c_shape,
        scratch_shapes=[
            pltpu.VMEM((1, C), F32),
            pltpu.VMEM((cs, C), F32),
            pltpu.VMEM((cs, C), F32),
        ] + [pltpu.VMEM((nb, HG_DV, HG_DK), F32) for _ in range(HG_HEADS)],
        compiler_params=_params(("arbitrary", "arbitrary")),
        name="hgrn2",
    )(z3, z3, z3, z3, s0, hg_lb, gn.reshape(1, HG_DV), *casts)


def _softmax_rows(s):
    e = jnp.exp(s - jnp.max(s, axis=-1, keepdims=True))
    return e / jnp.sum(e, axis=-1, keepdims=True)


def _xattn_kernel(q_ref, k_ref, v_ref, o_ref):
    hls = [slice(h * XA_HD, (h + 1) * XA_HD) for h in range(XA_HEADS)]
    s = [_dot_nt(q_ref[0, :, hl].astype(BF16), k_ref[0, :, hl].astype(BF16)) * (XA_HD ** -0.5)
         for hl in hls]
    p = [_softmax_rows(sh).astype(BF16) for sh in s]
    for hl, ph in zip(hls, p):
        o = jnp.dot(ph, v_ref[0, :, hl].astype(BF16), preferred_element_type=F32)
        o_ref[0, :, hl] = o.astype(BF16)


def _xattn(z3, mk, mv, T):
    B, L, _ = z3.shape
    C = XA_HEADS * XA_HD
    mem = lambda: pl.BlockSpec((1, MEM_LEN, C), lambda b, t: (b, 0, 0))
    return pl.pallas_call(
        _xattn_kernel,
        grid=(B, L // T),
        in_specs=[pl.BlockSpec((1, T, C), lambda b, t: (b, t, COL_XQ)), mem(), mem()],
        out_specs=pl.BlockSpec((1, T, C), lambda b, t: (b, t, 0)),
        out_shape=jax.ShapeDtypeStruct((B, L, C), BF16),
        compiler_params=_params(("arbitrary", "arbitrary")),
        name="xattn",
    )(z3, mk, mv)


def _xattn_cached_kernel(q_ref, k_ref, v_ref, o_ref):
    kt = jnp.swapaxes(k_ref[...], 0, 1)
    vt = jnp.swapaxes(v_ref[...], 0, 1)
    hls = [slice(h * XA_HD, (h + 1) * XA_HD) for h in range(XA_HEADS)]
    s = [_dot_nt(q_ref[0, :, hl].astype(BF16), kt[h].astype(BF16)) * (XA_HD ** -0.5)
         for h, hl in enumerate(hls)]
    p = [_softmax_rows(sh).astype(BF16) for sh in s]
    for h, hl in enumerate(hls):
        o = jnp.dot(p[h], vt[h].astype(BF16), preferred_element_type=F32)
        o_ref[0, :, hl] = o.astype(BF16)


def _xattn_cached(z3, cache_k, cache_v, layer):
    B, L, _ = z3.shape
    C = XA_HEADS * XA_HD
    mem = lambda: pl.BlockSpec((None, None, MEM_LEN, XA_HEADS, XA_HD), lambda b: (layer, b, 0, 0, 0))
    return pl.pallas_call(
        _xattn_cached_kernel,
        grid=(B,),
        in_specs=[pl.BlockSpec((1, L, C), lambda b: (b, 0, COL_XQ)), mem(), mem()],
        out_specs=pl.BlockSpec((1, L, C), lambda b: (b, 0, 0)),
        out_shape=jax.ShapeDtypeStruct((B, L, C), BF16),
        compiler_params=_params(("arbitrary",)),
        name="xattn_cached",
    )(z3, cache_k, cache_v)


def _merge_kernel(x_ref, ya_ref, yb_ref, yc_ref, g0_ref, g1_ref, g2_ref, bg_ref, wb_ref, wo_ref,
                  pg_ref, o_ref):
    m = jnp.zeros(o_ref.shape, F32)
    for nb, (y_ref, g_ref) in enumerate(((ya_ref, g0_ref), (yb_ref, g1_ref), (yc_ref, g2_ref))):
        gate = _sigmoid(g_ref[...] + bg_ref[nb:nb + 1, :])
        m = m + gate * jnp.dot(y_ref[...], wb_ref[nb], preferred_element_type=F32)
    y = jnp.dot(m.astype(BF16), wo_ref[...], preferred_element_type=F32)
    o_ref[...] = x_ref[...] + y * _rms_scale(y) * pg_ref[...]


def _merge(x, ya, yb, yc, z, b_gate, wb, wo, pg, tm):
    n, d = x.shape
    c = BRANCH_WIDTH
    once = pl.Buffered(1)
    branch = lambda: pl.BlockSpec((tm, c), lambda i: (i, 0))
    gate = lambda nb: pl.BlockSpec((tm, d), lambda i: (i, GATE_COL0 + nb))
    return pl.pallas_call(
        _merge_kernel,
        grid=(n // tm,),
        in_specs=[
            pl.BlockSpec((tm, d), lambda i: (i, 0)),
            branch(), branch(), branch(),
            gate(0), gate(1), gate(2),
            pl.BlockSpec((N_BRANCH, d), lambda i: (0, 0)),
            pl.BlockSpec((N_BRANCH, c, d), lambda i: (0, 0, 0), pipeline_mode=once),
            pl.BlockSpec((d, d), lambda i: (0, 0), pipeline_mode=once),
            pl.BlockSpec((1, d), lambda i: (0, 0)),
        ],
        out_specs=pl.BlockSpec((tm, d), lambda i: (i, 0)),
        out_shape=jax.ShapeDtypeStruct((n, d), F32),
        compiler_params=_params(("arbitrary",)),
        name="merge",
    )(x, ya, yb, yc, z, z, z, b_gate, wb, wo, pg.reshape(1, d))


def _ffn_kernel(x_ref, g_ref, wu_ref, wv_ref, cw_ref, cb_ref, fc0_ref, wd_ref, pg_ref,
                o_ref, fc1_ref, xn_ref, acc_ref, hal_ref, car_ref, *, nb, T, tf):
    it = pl.program_id(1)
    j = pl.program_id(2)
    nj = pl.num_programs(2)
    hist = FFN_CONV_W - 1

    @pl.when(j == 0)
    def _():
        x = x_ref[...].reshape(nb * T, D_MODEL)
        xn_ref[...] = (x * _rms_scale(x) * g_ref[...]).astype(BF16)
        acc_ref[...] = jnp.zeros_like(acc_ref)

    @pl.when(it == 0)
    def _():
        car_ref[j] = fc0_ref[...]

    xn = xn_ref[...]
    u = jnp.dot(xn, wu_ref[...], preferred_element_type=F32)
    v = jnp.dot(xn, wv_ref[...], preferred_element_type=F32)
    hal_ref[:, HALO - hist:HALO, :] = car_ref[j]
    hal_ref[:, HALO:HALO + T, :] = u.reshape(nb, T, tf)
    cw = cw_ref[...]
    uc = cb_ref[...][None]
    for jj in range(FFN_CONV_W):
        uc = uc + hal_ref[:, HALO - hist + jj:HALO - hist + jj + T, :] * cw[jj:jj + 1][None]
    tail = hal_ref[:, HALO + T - hist:HALO + T, :]
    car_ref[j] = tail
    act = (jax.nn.gelu(uc).reshape(nb * T, tf) * v).astype(BF16)
    acc_ref[...] += jnp.dot(act, wd_ref[...], preferred_element_type=F32)

    @pl.when(j == nj - 1)
    def _():
        pg = pg_ref[...]
        for b in range(nb):
            for r in range(0, T, FFN_OUT_ROWS):
                y = acc_ref[b * T + r:b * T + r + FFN_OUT_ROWS, :]
                o_ref[b, r:r + FFN_OUT_ROWS, :] = x_ref[b, r:r + FFN_OUT_ROWS, :] + y * _rms_scale(y) * pg

    @pl.when((j == nj - 1) & (it == pl.num_programs(1) - 1))
    def _():
        for jj in range(FFN_DIM // tf):
            fc1_ref[:, :, jj * tf:(jj + 1) * tf] = car_ref[jj]


def _ffn(x3, g, w_up, cw, cb, fc0, w_down, pg, nb, T, tf):
    B, L, d = x3.shape
    F = FFN_DIM
    nj = F // tf
    hist = FFN_CONV_W - 1
    rows = lambda: pl.BlockSpec((nb, T, d), lambda b, t, j: (b, t, 0))
    return pl.pallas_call(
        functools.partial(_ffn_kernel, nb=nb, T=T, tf=tf),
        grid=(B // nb, L // T, nj),
        in_specs=[
            rows(),
            pl.BlockSpec((1, d), lambda b, t, j: (0, 0)),
            pl.BlockSpec((d, tf), lambda b, t, j: (0, j)),
            pl.BlockSpec((d, tf), lambda b, t, j: (0, nj + j)),
            pl.BlockSpec((FFN_CONV_W, tf), lambda b, t, j: (0, j)),
            pl.BlockSpec((1, tf), lambda b, t, j: (0, j)),
            pl.BlockSpec((nb, hist, tf), lambda b, t, j: (b, 0, j)),
            pl.BlockSpec((tf, d), lambda b, t, j: (j, 0)),
            pl.BlockSpec((1, d), lambda b, t, j: (0, 0)),
        ],
        out_specs=[rows(), pl.BlockSpec((nb, hist, F), lambda b, t, j: (b, 0, 0))],
        out_shape=[jax.ShapeDtypeStruct((B, L, d), F32), jax.ShapeDtypeStruct((B, hist, F), F32)],
        scratch_shapes=[
            pltpu.VMEM((nb * T, d), BF16),
            pltpu.VMEM((nb * T, d), F32),
            pltpu.VMEM((nb, HALO + T, tf), F32),
            pltpu.VMEM((nj, nb, hist, tf), F32),
        ],
        compiler_params=_params(("arbitrary", "arbitrary", "arbitrary")),
        name="ffn",
    )(x3, g.reshape(1, d), w_up, w_up, cw, cb.reshape(1, F), fc0, w_down, pg.reshape(1, d))


def _block_diag(w):
    nblk, bi, bj = w.shape
    per = MXU_COLS // bi
    eye = jnp.eye(per, dtype=w.dtype)
    grouped = jnp.einsum("ghij,hk->ghikj", w.reshape(nblk // per, per, bi, bj), eye)
    return grouped.reshape(nblk // per, per * bi, per * bj)


def _tiles(B, L):
    T = min(L, 256)
    return dict(
        proj_tm=min(B * L, 1024), proj_tn=1536,
        rnn_nb=B if B * T <= 512 else 512 // T, rnn_T=T,
        hg_nb=min(B, 4), hg_T=T,
        xa_T=min(L, 512),
        merge_tm=256,
        ffn_nb=B if B * T <= 512 else 512 // T, ffn_T=T, ffn_tf=512,
    )


def _trunk_layer(x, mk, mv, h0, rc0, s0, fc0, w, layer):
    B, L, d = x.shape
    n = B * L
    t = _tiles(B, L)
    x2 = x.reshape(n, d)
    made = {}
    tm, tn = t["proj_tm"], t["proj_tn"]
    if w["w_in"].dtype == F32:
        z, made["w_in"] = _norm_matmul(x2, w["pre_mix_norm"], w["w_in"], tm, tn // 2, "in_proj",
                                       tiles=(0, 1), emit_w=True)
        if n > tm:
            z = _norm_matmul(x2, w["pre_mix_norm"], made["w_in"], tm, tn, "in_proj",
                             tiles=(1, n // tm), into=z)
    else:
        z = _norm_matmul(x2, w["pre_mix_norm"], w["w_in"], tm, tn, "in_proj")
    z3 = z.reshape(B, L, IN_COLS)
    c = BRANCH_WIDTH
    fresh = w["w_out"].dtype == F32
    ya, h1, rc1, *cast = _rglru(
        z3, rc0, h0, w["rnn_conv_w"], w["rnn_conv_b"], w["lru_wa"], w["lru_ba"], w["lru_wx"],
        w["lru_bx"], w["lru_lambda"], t["rnn_nb"], t["rnn_T"],
        casts=(w["w_branch"].reshape(N_BRANCH * c, d), w["w_out"]) if fresh else ())
    if fresh:
        made["w_branch"], made["w_out"] = cast[0].reshape(N_BRANCH, c, d), cast[1]
    yb, s1, *cast = _hgrn2(z3, s0, w["hg_lb"], w["hg_norm"], layer, t["hg_nb"], t["hg_T"],
                           casts=(w["w_ffn_up"], w["w_ffn_down"]) if fresh else ())
    if fresh:
        made["w_ffn_up"], made["w_ffn_down"] = cast
    w = {**w, **made}
    yc = _xattn(z3, mk, mv, t["xa_T"]) if mv is not None else _xattn_cached(z3, *mk, layer)
    x1 = _merge(x2, ya.reshape(n, c), yb.reshape(n, c), yc.reshape(n, c), z, w["b_gate"],
                w["w_branch"], w["w_out"], w["post_mix_norm"], t["merge_tm"]).reshape(B, L, d)
    xo, fc1 = _ffn(x1, w["pre_ffn_norm"], w["w_ffn_up"], w["ffn_conv_w"], w["ffn_conv_b"], fc0,
                   w["w_ffn_down"], w["post_ffn_norm"], t["ffn_nb"], t["ffn_T"], t["ffn_tf"])
    return xo, h1, rc1, s1, fc1, made


def kernel(x_prompt, x_sample, cache_mem_k, cache_mem_v, state_rnn_h, state_rnn_conv, state_hg,
           state_ffn_conv, mem_prompt, pre_mix_norm, w_in, rnn_conv_w, rnn_conv_b, lru_wa, lru_ba,
           lru_wx, lru_bx, lru_lambda, hg_lb, hg_norm, mem_norm, w_mem_kv, w_branch, b_gate, w_out,
           post_mix_norm, pre_ffn_norm, w_ffn_up, ffn_conv_w, ffn_conv_b, w_ffn_down, post_ffn_norm):
    depth = w_in.shape[0]
    Bp = x_prompt.shape[0]
    Bs = x_sample.shape[0]
    xa_w = XA_HEADS * XA_HD
    yp, ys = x_prompt, x_sample
    outs = [[] for _ in range(10)]
    for l in range(depth):
        w = {
            "pre_mix_norm": pre_mix_norm[l], "w_in": w_in[l],
            "rnn_conv_w": rnn_conv_w[l], "rnn_conv_b": rnn_conv_b[l],
            "lru_wa": _block_diag(lru_wa[l]).astype(BF16), "lru_ba": lru_ba[l],
            "lru_wx": _block_diag(lru_wx[l]).astype(BF16), "lru_bx": lru_bx[l],
            "lru_lambda": lru_lambda[l], "hg_lb": hg_lb, "hg_norm": hg_norm[l],
            "w_branch": w_branch[l], "b_gate": b_gate[l],
            "w_out": w_out[l], "post_mix_norm": post_mix_norm[l],
            "pre_ffn_norm": pre_ffn_norm[l], "w_ffn_up": w_ffn_up[l],
            "ffn_conv_w": ffn_conv_w[l], "ffn_conv_b": ffn_conv_b[l],
            "w_ffn_down": w_ffn_down[l], "post_ffn_norm": post_ffn_norm[l],
        }
        mem2 = mem_prompt.reshape(Bp * MEM_LEN, D_MODEL)
        kv = _norm_matmul(mem2, mem_norm[l], w_mem_kv[l], Bp * MEM_LEN, 1024, "mem_kv")
        kv = kv.reshape(Bp, MEM_LEN, 2 * xa_w)
        mk_p, mv_p = kv[..., :xa_w], kv[..., xa_w:]
        yp, h_p, rc_p, s_p, fc_p, made = _trunk_layer(
            yp, mk_p, mv_p, jnp.zeros((Bp, RNN_WIDTH), F32),
            jnp.zeros((Bp, RNN_CONV_W - 1, RNN_WIDTH), F32),
            jnp.zeros((Bp, HG_HEADS, HG_DK, HG_DV), F32),
            jnp.zeros((Bp, FFN_CONV_W - 1, FFN_DIM), F32), w, l)
        w = {**w, **made}
        ys, h_s, rc_s, s_s, fc_s, _ = _trunk_layer(
            ys, (cache_mem_k, cache_mem_v), None,
            state_rnn_h[l], state_rnn_conv[l], state_hg[l], state_ffn_conv[l], w, l)
        layer_out = (mk_p.reshape(Bp, MEM_LEN, XA_HEADS, XA_HD), mv_p.reshape(Bp, MEM_LEN, XA_HEADS, XA_HD),
                     h_p, rc_p, s_p, fc_p, h_s, rc_s, s_s, fc_s)
        for acc, val in zip(outs, layer_out):
            acc.append(val)
    return (yp, ys) + tuple(jnp.stack(o) for o in outs)
```

```python
import functools

import jax
import jax.numpy as jnp
from jax import lax
from jax.experimental import pallas as pl
from jax.experimental.pallas import tpu as pltpu

F32 = jnp.float32
BF16 = jnp.bfloat16

D_MODEL = 2048
RNN_WIDTH = 1024
RNN_BLOCKS = 16
RNN_CONV_W = 4
LRU_C = 8.0
HG_HEADS = 8
HG_DK = 128
HG_DV = 128
HG_CHUNK = 128
HG_SUB_MAX = 16
SUBLANES = 8
MXU_COLS = 256
HG_SKEW = 8
MEM_LEN = 256
XA_HEADS = 4
XA_HD = 256
BRANCH_WIDTH = 1024
N_BRANCH = 3
FFN_DIM = 5632
FFN_CONV_W = 3
FFN_OUT_ROWS = 16
EPS = 1e-6
LOG2E = 1.4426950408889634
IN_COLS = 6 * BRANCH_WIDTH + N_BRANCH * D_MODEL

COL_RNN, COL_HQ, COL_HF, COL_HI, COL_HO, COL_XQ = range(6)
GATE_COL0 = 3

HALO = 8
VMEM_LIMIT = 56 * 1024 * 1024


def _params(sem):
    return pltpu.CompilerParams(dimension_semantics=sem, vmem_limit_bytes=VMEM_LIMIT)


def _rms_scale(x):
    return lax.rsqrt(jnp.mean(x * x, axis=-1, keepdims=True) + EPS)


_sigmoid = jax.nn.sigmoid


def _sigmoid_tanh(x):
    return 0.5 * jnp.tanh(0.5 * x) + 0.5


def _with_casts(kernel, n_in, n_out, n_cast):
    def wrapped(*refs, **kw):
        ins, refs = refs[:n_in], refs[n_in:]
        cast_in, refs = refs[:n_cast], refs[n_cast:]
        outs, refs = refs[:n_out], refs[n_out:]
        cast_out, scratch = refs[:n_cast], refs[n_cast:]
        for src, dst in zip(cast_in, cast_out):
            dst[...] = src[...].astype(BF16)
        kernel(*ins, *outs, *scratch, **kw)
    return wrapped


def _cast_specs(casts, nsteps, step):
    in_specs, out_specs, out_shape = [], [], []
    for w in casts:
        rows, cols = w.shape[0] // nsteps, w.shape[1]
        assert rows * nsteps == w.shape[0] and rows % (2 * SUBLANES) == 0
        for specs in (in_specs, out_specs):
            specs.append(pl.BlockSpec((rows, cols), lambda *g, step=step: (step(*g), 0)))
        out_shape.append(jax.ShapeDtypeStruct(w.shape, BF16))
    return in_specs, out_specs, out_shape


def _norm_matmul_kernel(x_ref, g_ref, w_ref, *rest, aliased):
    o_ref, xn_ref = rest[1:] if aliased else rest

    @pl.when(pl.program_id(1) == 0)
    def _():
        x = x_ref[...]
        xn_ref[...] = (x * _rms_scale(x) * g_ref[...]).astype(BF16)

    o_ref[...] = jnp.dot(xn_ref[...], w_ref[...].astype(BF16), preferred_element_type=F32)


def _norm_matmul(x, g, w, tm, tn, name, tiles=None, into=None):
    n, d = x.shape
    c = w.shape[1]
    t0, t1 = tiles if tiles is not None else (0, n // tm)
    in_specs = [
        pl.BlockSpec((tm, d), lambda i, j: (i + t0, 0)),
        pl.BlockSpec((1, d), lambda i, j: (0, 0)),
        pl.BlockSpec((d, tn), lambda i, j: (0, j)),
    ]
    args = [x, g.reshape(1, d), w]
    if into is not None:
        in_specs.append(pl.BlockSpec(memory_space=pl.ANY))
        args.append(into)
    return pl.pallas_call(
        functools.partial(_norm_matmul_kernel, aliased=into is not None),
        grid=(t1 - t0, c // tn),
        in_specs=in_specs,
        out_specs=pl.BlockSpec((tm, tn), lambda i, j: (i + t0, j)),
        out_shape=jax.ShapeDtypeStruct((n, c), F32),
        scratch_shapes=[pltpu.VMEM((tm, d), BF16)],
        input_output_aliases={3: 0} if into is not None else {},
        compiler_params=_params(("arbitrary", "arbitrary")),
        name=name,
    )(*args)


def _norm_matmul_cast_kernel(x_ref, g_ref, w_ref, o_ref, wb_ref, xn_ref):
    j, r = pl.program_id(0), pl.program_id(1)

    @pl.when(j == 0)
    def _():
        x = x_ref[...]
        xn_ref[r] = (x * _rms_scale(x) * g_ref[...]).astype(BF16)

    @pl.when(r == 0)
    def _():
        wb_ref[...] = w_ref[...].astype(BF16)

    o_ref[...] = jnp.dot(xn_ref[r], wb_ref[...], preferred_element_type=F32)


def _norm_matmul_cast(x, g, w, tm, tn, nrow, name):
    n, d = x.shape
    c = w.shape[1]
    return pl.pallas_call(
        _norm_matmul_cast_kernel,
        grid=(c // tn, nrow),
        in_specs=[
            pl.BlockSpec((tm, d), lambda j, r: (jnp.where(j == 0, r, nrow - 1), 0)),
            pl.BlockSpec((1, d), lambda j, r: (0, 0)),
            pl.BlockSpec((d, tn), lambda j, r: (0, j)),
        ],
        out_specs=[pl.BlockSpec((tm, tn), lambda j, r: (r, j)),
                   pl.BlockSpec((d, tn), lambda j, r: (0, j))],
        out_shape=[jax.ShapeDtypeStruct((n, c), F32), jax.ShapeDtypeStruct((d, c), BF16)],
        scratch_shapes=[pltpu.VMEM((nrow, tm, d), BF16)],
        compiler_params=_params(("arbitrary", "arbitrary")),
        name=name,
    )(x, g.reshape(1, d), w)


def _rglru_kernel(z_ref, rc0_ref, h0_ref, cw_ref, cb_ref, wa_ref, ba_ref, wx_ref, bx_ref, lam_ref,
                  ya_ref, h1_ref, rc1_ref, xp_ref, a_ref, b_ref, hs_ref, hc_ref, *, nb, T):
    it = pl.program_id(1)
    nt = pl.num_programs(1)
    hist = RNN_CONV_W - 1

    @pl.when(it == 0)
    def _():
        xp_ref[:, HALO - hist:HALO, :] = rc0_ref[...]
        hc_ref[...] = h0_ref[...]

    xp_ref[:, HALO:HALO + T, :] = z_ref[...]
    cw = cw_ref[...]
    xr = cb_ref[...][None]
    for j in range(RNN_CONV_W):
        xr = xr + xp_ref[:, HALO - hist + j:HALO - hist + j + T, :] * cw[j:j + 1][None]
    xp_ref[:, HALO - hist:HALO, :] = xp_ref[:, HALO + T - hist:HALO + T, :]

    xr2 = xr.reshape(nb * T, RNN_WIDTH)
    xb = xr2.astype(BF16)
    def gate(w_ref, bias_ref):
        cols = [jnp.dot(xb[:, g * MXU_COLS:(g + 1) * MXU_COLS], w_ref[g], preferred_element_type=F32)
                for g in range(RNN_WIDTH // MXU_COLS)]
        return _sigmoid_tanh(jnp.concatenate(cols, axis=1) + bias_ref[...])

    r = gate(wa_ref, ba_ref)
    ig = gate(wx_ref, bx_ref)
    nl = -lam_ref[...]
    softplus = jnp.maximum(nl, 0.0) + jnp.log1p(jnp.exp(-jnp.abs(nl)))
    log_a = -LRU_C * r * softplus
    a = jnp.exp(log_a)
    a_ref[...] = a.reshape(nb, T, RNN_WIDTH)
    one_minus_a2 = -jnp.tanh(log_a) * (a * a + 1.0)
    b_ref[...] = (jnp.sqrt(one_minus_a2) * (ig * xr2)).reshape(nb, T, RNN_WIDTH)

    hs = [hc_ref[bi:bi + 1, :] for bi in range(nb)]
    for t in range(T):
        for bi in range(nb):
            hs[bi] = a_ref[bi, t:t + 1, :] * hs[bi] + b_ref[bi, t:t + 1, :]
            hs_ref[bi, t:t + 1, :] = hs[bi]
    for bi in range(nb):
        hc_ref[bi:bi + 1, :] = hs[bi]
    ya_ref[...] = hs_ref[...].astype(BF16)

    @pl.when(it == nt - 1)
    def _():
        h1_ref[...] = hc_ref[...]
        rc1_ref[...] = xp_ref[:, HALO - hist:HALO, :]


def _rglru(z3, rc0, h0, cw, cb, wa, ba, wx, bx, lam, nb, T, casts=()):
    B, L, _ = z3.shape
    C = RNN_WIDTH
    hist = RNN_CONV_W - 1
    vec = lambda: pl.BlockSpec((1, C), lambda b, t: (0, 0))
    mat = lambda: pl.BlockSpec((C // MXU_COLS, MXU_COLS, MXU_COLS), lambda b, t: (0, 0, 0))
    nt = L // T
    c_in, c_out, c_shape = _cast_specs(casts, (B // nb) * nt, lambda b, t: b * nt + t)
    return pl.pallas_call(
        functools.partial(_with_casts(_rglru_kernel, 10, 3, len(casts)), nb=nb, T=T),
        grid=(B // nb, nt),
        in_specs=[
            pl.BlockSpec((nb, T, C), lambda b, t: (b, t, COL_RNN)),
            pl.BlockSpec((nb, hist, C), lambda b, t: (b, 0, 0)),
            pl.BlockSpec((nb, C), lambda b, t: (b, 0)),
            pl.BlockSpec((RNN_CONV_W, C), lambda b, t: (0, 0)),
            vec(), mat(), vec(), mat(), vec(), vec(),
        ] + c_in,
        out_specs=[
            pl.BlockSpec((nb, T, C), lambda b, t: (b, t, 0)),
            pl.BlockSpec((nb, C), lambda b, t: (b, 0)),
            pl.BlockSpec((nb, hist, C), lambda b, t: (b, 0, 0)),
        ] + c_out,
        out_shape=[
            jax.ShapeDtypeStruct((B, L, C), BF16),
            jax.ShapeDtypeStruct((B, C), F32),
            jax.ShapeDtypeStruct((B, hist, C), F32),
        ] + c_shape,
        scratch_shapes=[
            pltpu.VMEM((nb, HALO + T, C), F32),
            pltpu.VMEM((nb, T, C), F32),
            pltpu.VMEM((nb, T, C), F32),
            pltpu.VMEM((nb, T, C), F32),
            pltpu.VMEM((nb, C), F32),
        ],
        compiler_params=_params(("arbitrary", "arbitrary")),
        name="rglru",
    )(z3, rc0, h0, cw, cb.reshape(1, C), wa, ba.reshape(1, C), wx, bx.reshape(1, C), lam.reshape(1, C),
      *casts)


def _dot_nt(a, b):
    return lax.dot_general(a, b, (((1,), (1,)), ((), ())), preferred_element_type=F32)


def _split3(x):
    hi = x.astype(BF16)
    r = x - hi.astype(F32)
    mid = r.astype(BF16)
    lo = (r - mid.astype(F32)).astype(BF16)
    return jnp.concatenate([hi, mid, lo], axis=0)


def _hg_sub_rows(cs):
    return HG_SUB_MAX if cs >= 4 * HG_SUB_MAX else SUBLANES


def _hgrn2_scores(qh, bc, ck, ck_row, st, cs):
    o = _dot_nt((qh * jnp.exp2(bc)).astype(BF16), st.astype(BF16))

    sub_rows = _hg_sub_rows(cs)
    nblk = cs // sub_rows
    a_rows = [None] * nblk
    half = cs // 2
    while half >= sub_rows:
        npair = cs // (2 * half)
        q_parts, k_parts = [], []
        for p in range(npair):
            lo, mid, hi = 2 * half * p, 2 * half * p + half, 2 * half * (p + 1)
            br = bc[mid - 1:mid]
            q_parts.append(qh[mid:hi] * jnp.exp2(bc[mid:hi] - br))
            k_parts += [jnp.exp2(br - ck[lo:mid]), jnp.zeros((half, HG_DK), F32)]
        off = _dot_nt(jnp.concatenate(q_parts, axis=0).astype(BF16),
                      jnp.concatenate(k_parts, axis=0).astype(BF16))
        if npair > 1:
            lg = half.bit_length() - 1
            rowp = lax.shift_right_logical(lax.broadcasted_iota(jnp.int32, off.shape, 0), lg)
            colp = lax.shift_right_logical(lax.broadcasted_iota(jnp.int32, off.shape, 1), lg + 1)
            off = jnp.where(rowp == colp, off, 0.0)
        for p in range(npair):
            for sub in range(half // sub_rows):
                blk = (2 * half * p + half) // sub_rows + sub
                piece = off[p * half + sub * sub_rows:p * half + (sub + 1) * sub_rows]
                a_rows[blk] = piece if a_rows[blk] is None else a_rows[blk] + piece
        half //= 2

    ones = jnp.ones((HG_DK, HG_DK), BF16)
    zero = jnp.zeros((SUBLANES, HG_DK), F32)
    groups = sub_rows // SUBLANES
    lane_sums = []
    for blk in range(nblk):
        r0 = blk * sub_rows
        bg = [bc[r0 + v * SUBLANES:r0 + (v + 1) * SUBLANES] for v in range(groups)]
        qg = [qh[r0 + v * SUBLANES:r0 + (v + 1) * SUBLANES] for v in range(groups)]
        pair = []
        for s in range(sub_rows):
            cks = ck_row(r0 + s)
            pair += [qg[v] * jnp.exp2(bg[v] - cks) if v >= s // SUBLANES else zero
                     for v in range(groups)]
        lane_sums.append(jnp.dot(jnp.concatenate(pair, axis=0).astype(BF16), ones,
                                 preferred_element_type=F32))
    return o, a_rows, lane_sums


def _hgrn2_output(scores, vh, ogh, gnh, bl, ck, st, cs):
    o, a_rows, lane_sums = scores
    sub_rows = _hg_sub_rows(cs)
    groups = sub_rows // SUBLANES
    lane = lax.broadcasted_iota(jnp.int32, (SUBLANES, HG_DK), 1)
    tloc = lax.broadcasted_iota(jnp.int32, (SUBLANES, HG_DK), 0)
    mask = [[(lane == s) & (tloc >= s - v * SUBLANES) if s > v * SUBLANES else (lane == s)
             for v in range(groups)] for s in range(sub_rows)]
    a_rows = list(a_rows)
    for blk, lane_sum in enumerate(lane_sums):
        a_g = [jnp.zeros((SUBLANES, HG_DK), F32)] * groups
        for s in range(sub_rows):
            for v in range(s // SUBLANES, groups):
                r = s * sub_rows + v * SUBLANES
                a_g[v] = jnp.where(mask[s][v], lane_sum[r:r + SUBLANES], a_g[v])
        a_d = jnp.concatenate(a_g, axis=0)
        if blk:
            a_d = pltpu.roll(a_d, blk * sub_rows, 1)
        a_d = a_d[:, :cs]
        a_rows[blk] = a_d if a_rows[blk] is None else a_rows[blk] + a_d
    a = jnp.concatenate(a_rows, axis=0)
    o = o + jnp.dot(a.astype(BF16), vh.astype(BF16), preferred_element_type=F32)
    kd = jnp.exp2(bl - ck).astype(BF16)
    st_new = st * jnp.exp2(bl) + jnp.dot(vh.T.astype(BF16), kd, preferred_element_type=F32)
    y = o * _rms_scale(o) * gnh * _sigmoid(ogh)
    return y, st_new


def _hgrn2_kernel(q_ref, f_ref, v_ref, og_ref, s0_ref, lbp_ref, gn_ref, yb_ref, s1_ref,
                  lb_ref, bc_ref, ck_ref, *st_refs, nb, T, cs, layer):
    it = pl.program_id(1)
    nt = pl.num_programs(1)
    row = lax.broadcasted_iota(jnp.int32, (cs, 3 * cs), 0)
    col = lax.broadcasted_iota(jnp.int32, (cs, 3 * cs), 1)
    tri3 = (row >= (col & (cs - 1))).astype(BF16)

    @pl.when(it == 0)
    def _():
        p = lbp_ref[...]
        e = jnp.exp(p - jnp.max(p, axis=0, keepdims=True))
        sm = e / jnp.sum(e, axis=0, keepdims=True)
        lb_ref[...] = jnp.sum(sm[:layer + 1], axis=0, keepdims=True)

        def init(b, c):
            for h in range(HG_HEADS):
                st_refs[h][b] = s0_ref[b, h].T
            return c
        lax.fori_loop(0, nb, init, 0)

    nch = T // cs

    def chunk(idx, c):
        b = idx // nch
        rows = pl.ds(pl.multiple_of((idx % nch) * cs, cs), cs)
        lb = lb_ref[...]
        f = lb + (1.0 - lb) * _sigmoid(f_ref[b, rows, :])
        bc = jnp.dot(tri3, _split3(jnp.log(f) * LOG2E), preferred_element_type=F32)
        bc_ref[...] = bc
        ck_ref[...] = bc - jnp.log(1.0 - f) * LOG2E

        def scores(h):
            hl = slice(h * HG_DK, (h + 1) * HG_DK)
            return _hgrn2_scores(q_ref[b, rows, hl], bc_ref[:, hl], ck_ref[:, hl],
                                 lambda r: ck_ref[r:r + 1, hl], st_refs[h][b], cs)

        def output(h, sc):
            hl = slice(h * HG_DK, (h + 1) * HG_DK)
            y, st_new = _hgrn2_output(sc, v_ref[b, rows, hl], og_ref[b, rows, hl], gn_ref[...],
                                      bc_ref[cs - 1:cs, hl], ck_ref[:, hl], st_refs[h][b], cs)
            yb_ref[b, rows, hl] = y.astype(BF16)
            st_refs[h][b] = st_new

        pending = {}
        for h in range(HG_HEADS + HG_SKEW):
            if h < HG_HEADS:
                pending[h] = scores(h)
            if h >= HG_SKEW:
                output(h - HG_SKEW, pending.pop(h - HG_SKEW))
        return c
    lax.fori_loop(0, nb * nch, chunk, 0)

    @pl.when(it == nt - 1)
    def _():
        def fin(b, c):
            for h in range(HG_HEADS):
                s1_ref[b, h] = st_refs[h][b].T
            return c
        lax.fori_loop(0, nb, fin, 0)


def _hgrn2(z3, s0, hg_lb, gn, layer, nb, T, casts=()):
    B, L, _ = z3.shape
    C = HG_HEADS * HG_DK
    cs = min(HG_CHUNK, L)
    zcol = lambda col: pl.BlockSpec((nb, T, C), lambda b, t: (b, t, col))
    st_spec = lambda: pl.BlockSpec((nb, HG_HEADS, HG_DK, HG_DV), lambda b, t: (b, 0, 0, 0))
    nl = hg_lb.shape[0]
    nt = L // T
    c_in, c_out, c_shape = _cast_specs(casts, (B // nb) * nt, lambda b, t: b * nt + t)
    return pl.pallas_call(
        functools.partial(_with_casts(_hgrn2_kernel, 7, 2, len(casts)), nb=nb, T=T, cs=cs, layer=layer),
        grid=(B // nb, nt),
        in_specs=[
            zcol(COL_HQ), zcol(COL_HF), zcol(COL_HI), zcol(COL_HO),
            st_spec(),
            pl.BlockSpec((nl, C), lambda b, t: (0, 0)),
            pl.BlockSpec((1, HG_DV), lambda b, t: (0, 0)),
        ] + c_in,
        out_specs=[pl.BlockSpec((nb, T, C), lambda b, t: (b, t, 0)), st_spec()] + c_out,
        out_shape=[
            jax.ShapeDtypeStruct((B, L, C), BF16),
            jax.ShapeDtypeStruct((B, HG_HEADS, HG_DK, HG_DV), F32),
        ] + c_shape,
        scratch_shapes=[
            pltpu.VMEM((1, C), F32),
            pltpu.VMEM((cs, C), F32),
            pltpu.VMEM((cs, C), F32),
        ] + [pltpu.VMEM((nb, HG_DV, HG_DK), F32) for _ in range(HG_HEADS)],
        compiler_params=_params(("arbitrary", "arbitrary")),
        name="hgrn2",
    )(z3, z3, z3, z3, s0, hg_lb, gn.reshape(1, HG_DV), *casts)


def _softmax_rows(s):
    e = jnp.exp(s - jnp.max(s, axis=-1, keepdims=True))
    return e / jnp.sum(e, axis=-1, keepdims=True)


def _xattn_kernel(q_ref, k_ref, v_ref, o_ref):
    hls = [slice(h * XA_HD, (h + 1) * XA_HD) for h in range(XA_HEADS)]
    s = [_dot_nt(q_ref[0, :, hl].astype(BF16), k_ref[0, :, hl].astype(BF16)) * (XA_HD ** -0.5)
         for hl in hls]
    p = [_softmax_rows(sh).astype(BF16) for sh in s]
    for hl, ph in zip(hls, p):
        o = jnp.dot(ph, v_ref[0, :, hl].astype(BF16), preferred_element_type=F32)
        o_ref[0, :, hl] = o.astype(BF16)


def _xattn(z3, mk, mv, T):
    B, L, _ = z3.shape
    C = XA_HEADS * XA_HD
    mem = lambda: pl.BlockSpec((1, MEM_LEN, C), lambda b, t: (b, 0, 0))
    return pl.pallas_call(
        _xattn_kernel,
        grid=(B, L // T),
        in_specs=[pl.BlockSpec((1, T, C), lambda b, t: (b, t, COL_XQ)), mem(), mem()],
        out_specs=pl.BlockSpec((1, T, C), lambda b, t: (b, t, 0)),
        out_shape=jax.ShapeDtypeStruct((B, L, C), BF16),
        compiler_params=_params(("arbitrary", "arbitrary")),
        name="xattn",
    )(z3, mk, mv)


def _xattn_cached_kernel(q_ref, k_ref, v_ref, o_ref):
    kt = jnp.swapaxes(k_ref[...], 0, 1)
    vt = jnp.swapaxes(v_ref[...], 0, 1)
    hls = [slice(h * XA_HD, (h + 1) * XA_HD) for h in range(XA_HEADS)]
    s = [_dot_nt(q_ref[0, :, hl].astype(BF16), kt[h].astype(BF16)) * (XA_HD ** -0.5)
         for h, hl in enumerate(hls)]
    p = [_softmax_rows(sh).astype(BF16) for sh in s]
    for h, hl in enumerate(hls):
        o = jnp.dot(p[h], vt[h].astype(BF16), preferred_element_type=F32)
        o_ref[0, :, hl] = o.astype(BF16)


def _xattn_cached(z3, cache_k, cache_v, layer):
    B, L, _ = z3.shape
    C = XA_HEADS * XA_HD
    mem = lambda: pl.BlockSpec((None, None, MEM_LEN, XA_HEADS, XA_HD), lambda b: (layer, b, 0, 0, 0))
    return pl.pallas_call(
        _xattn_cached_kernel,
        grid=(B,),
        in_specs=[pl.BlockSpec((1, L, C), lambda b: (b, 0, COL_XQ)), mem(), mem()],
        out_specs=pl.BlockSpec((1, L, C), lambda b: (b, 0, 0)),
        out_shape=jax.ShapeDtypeStruct((B, L, C), BF16),
        compiler_params=_params(("arbitrary",)),
        name="xattn_cached",
    )(z3, cache_k, cache_v)


def _merge_kernel(x_ref, ya_ref, yb_ref, yc_ref, g0_ref, g1_ref, g2_ref, bg_ref, wb_ref, wo_ref,
                  pg_ref, o_ref):
    m = jnp.zeros(o_ref.shape, F32)
    for nb, (y_ref, g_ref) in enumerate(((ya_ref, g0_ref), (yb_ref, g1_ref), (yc_ref, g2_ref))):
        gate = _sigmoid(g_ref[...] + bg_ref[nb:nb + 1, :])
        m = m + gate * jnp.dot(y_ref[...], wb_ref[nb], preferred_element_type=F32)
    y = jnp.dot(m.astype(BF16), wo_ref[...], preferred_element_type=F32)
    o_ref[...] = x_ref[...] + y * _rms_scale(y) * pg_ref[...]


def _merge(x, ya, yb, yc, z, b_gate, wb, wo, pg, tm):
    n, d = x.shape
    c = BRANCH_WIDTH
    once = pl.Buffered(1)
    branch = lambda: pl.BlockSpec((tm, c), lambda i: (i, 0))
    gate = lambda nb: pl.BlockSpec((tm, d), lambda i: (i, GATE_COL0 + nb))
    return pl.pallas_call(
        _merge_kernel,
        grid=(n // tm,),
        in_specs=[
            pl.BlockSpec((tm, d), lambda i: (i, 0)),
            branch(), branch(), branch(),
            gate(0), gate(1), gate(2),
            pl.BlockSpec((N_BRANCH, d), lambda i: (0, 0)),
            pl.BlockSpec((N_BRANCH, c, d), lambda i: (0, 0, 0), pipeline_mode=once),
            pl.BlockSpec((d, d), lambda i: (0, 0), pipeline_mode=once),
            pl.BlockSpec((1, d), lambda i: (0, 0)),
        ],
        out_specs=pl.BlockSpec((tm, d), lambda i: (i, 0)),
        out_shape=jax.ShapeDtypeStruct((n, d), F32),
        compiler_params=_params(("arbitrary",)),
        name="merge",
    )(x, ya, yb, yc, z, z, z, b_gate, wb, wo, pg.reshape(1, d))


def _ffn_kernel(x_ref, g_ref, wu_ref, wv_ref, cw_ref, cb_ref, fc0_ref, wd_ref, pg_ref,
                o_ref, fc1_ref, xn_ref, acc_ref, hal_ref, car_ref, *, nb, T, tf):
    it = pl.program_id(1)
    j = pl.program_id(2)
    nj = pl.num_programs(2)
    hist = FFN_CONV_W - 1

    @pl.when(j == 0)
    def _():
        x = x_ref[...].reshape(nb * T, D_MODEL)
        xn_ref[...] = (x * _rms_scale(x) * g_ref[...]).astype(BF16)
        acc_ref[...] = jnp.zeros_like(acc_ref)

    @pl.when(it == 0)
    def _():
        car_ref[j] = fc0_ref[...]

    xn = xn_ref[...]
    u = jnp.dot(xn, wu_ref[...], preferred_element_type=F32)
    v = jnp.dot(xn, wv_ref[...], preferred_element_type=F32)
    hal_ref[:, HALO - hist:HALO, :] = car_ref[j]
    hal_ref[:, HALO:HALO + T, :] = u.reshape(nb, T, tf)
    cw = cw_ref[...]
    uc = cb_ref[...][None]
    for jj in range(FFN_CONV_W):
        uc = uc + hal_ref[:, HALO - hist + jj:HALO - hist + jj + T, :] * cw[jj:jj + 1][None]
    tail = hal_ref[:, HALO + T - hist:HALO + T, :]
    car_ref[j] = tail
    act = (jax.nn.gelu(uc).reshape(nb * T, tf) * v).astype(BF16)
    acc_ref[...] += jnp.dot(act, wd_ref[...], preferred_element_type=F32)

    @pl.when(j == nj - 1)
    def _():
        pg = pg_ref[...]
        for b in range(nb):
            for r in range(0, T, FFN_OUT_ROWS):
                y = acc_ref[b * T + r:b * T + r + FFN_OUT_ROWS, :]
                o_ref[b, r:r + FFN_OUT_ROWS, :] = x_ref[b, r:r + FFN_OUT_ROWS, :] + y * _rms_scale(y) * pg

    @pl.when((j == nj - 1) & (it == pl.num_programs(1) - 1))
    def _():
        for jj in range(FFN_DIM // tf):
            fc1_ref[:, :, jj * tf:(jj + 1) * tf] = car_ref[jj]


def _ffn(x3, g, w_up, cw, cb, fc0, w_down, pg, nb, T, tf):
    B, L, d = x3.shape
    F = FFN_DIM
    nj = F // tf
    hist = FFN_CONV_W - 1
    rows = lambda: pl.BlockSpec((nb, T, d), lambda b, t, j: (b, t, 0))
    return pl.pallas_call(
        functools.partial(_ffn_kernel, nb=nb, T=T, tf=tf),
        grid=(B // nb, L // T, nj),
        in_specs=[
            rows(),
            pl.BlockSpec((1, d), lambda b, t, j: (0, 0)),
            pl.BlockSpec((d, tf), lambda b, t, j: (0, j)),
            pl.BlockSpec((d, tf), lambda b, t, j: (0, nj + j)),
            pl.BlockSpec((FFN_CONV_W, tf), lambda b, t, j: (0, j)),
            pl.BlockSpec((1, tf), lambda b, t, j: (0, j)),
            pl.BlockSpec((nb, hist, tf), lambda b, t, j: (b, 0, j)),
            pl.BlockSpec((tf, d), lambda b, t, j: (j, 0)),
            pl.BlockSpec((1, d), lambda b, t, j: (0, 0)),
        ],
        out_specs=[rows(), pl.BlockSpec((nb, hist, F), lambda b, t, j: (b, 0, 0))],
        out_shape=[jax.ShapeDtypeStruct((B, L, d), F32), jax.ShapeDtypeStruct((B, hist, F), F32)],
        scratch_shapes=[
            pltpu.VMEM((nb * T, d), BF16),
            pltpu.VMEM((nb * T, d), F32),
            pltpu.VMEM((nb, HALO + T, tf), F32),
            pltpu.VMEM((nj, nb, hist, tf), F32),
        ],
        compiler_params=_params(("arbitrary", "arbitrary", "arbitrary")),
        name="ffn",
    )(x3, g.reshape(1, d), w_up, w_up, cw, cb.reshape(1, F), fc0, w_down, pg.reshape(1, d))


def _block_diag(w):
    nblk, bi, bj = w.shape
    per = MXU_COLS // bi
    eye = jnp.eye(per, dtype=w.dtype)
    grouped = jnp.einsum("ghij,hk->ghikj", w.reshape(nblk // per, per, bi, bj), eye)
    return grouped.reshape(nblk // per, per * bi, per * bj)


def _tiles(B, L):
    T = min(L, 256)
    return dict(
        proj_tm=min(B * L, 1024), proj_tn=1536,
        rnn_nb=B if B * T <= 512 else 512 // T, rnn_T=T,
        hg_nb=min(B, 4), hg_T=T,
        xa_T=min(L, 512),
        merge_tm=256,
        ffn_nb=B if B * T <= 512 else 512 // T, ffn_T=T, ffn_tf=512,
    )


def _trunk_layer(x, mk, mv, h0, rc0, s0, fc0, w, layer):
    B, L, d = x.shape
    n = B * L
    t = _tiles(B, L)
    x2 = x.reshape(n, d)
    made = {}
    tm, tn = t["proj_tm"], t["proj_tn"]
    if w["w_in"].dtype == F32:
        first = min(2, n // tm)
        z, made["w_in"] = _norm_matmul_cast(x2, w["pre_mix_norm"], w["w_in"], tm, tn // 2, first,
                                            "in_proj")
        if n > first * tm:
            z = _norm_matmul(x2, w["pre_mix_norm"], made["w_in"], tm, tn, "in_proj",
                             tiles=(first, n // tm), into=z)
    else:
        z = _norm_matmul(x2, w["pre_mix_norm"], w["w_in"], tm, tn, "in_proj")
    z3 = z.reshape(B, L, IN_COLS)
    c = BRANCH_WIDTH
    fresh = w["w_out"].dtype == F32
    ya, h1, rc1, *cast = _rglru(
        z3, rc0, h0, w["rnn_conv_w"], w["rnn_conv_b"], w["lru_wa"], w["lru_ba"], w["lru_wx"],
        w["lru_bx"], w["lru_lambda"], t["rnn_nb"], t["rnn_T"],
        casts=(w["w_branch"].reshape(N_BRANCH * c, d), w["w_out"]) if fresh else ())
    if fresh:
        made["w_branch"], made["w_out"] = cast[0].reshape(N_BRANCH, c, d), cast[1]
    yb, s1, *cast = _hgrn2(z3, s0, w["hg_lb"], w["hg_norm"], layer, t["hg_nb"], t["hg_T"],
                           casts=(w["w_ffn_up"], w["w_ffn_down"]) if fresh else ())
    if fresh:
        made["w_ffn_up"], made["w_ffn_down"] = cast
    w = {**w, **made}
    yc = _xattn(z3, mk, mv, t["xa_T"]) if mv is not None else _xattn_cached(z3, *mk, layer)
    x1 = _merge(x2, ya.reshape(n, c), yb.reshape(n, c), yc.reshape(n, c), z, w["b_gate"],
                w["w_branch"], w["w_out"], w["post_mix_norm"], t["merge_tm"]).reshape(B, L, d)
    xo, fc1 = _ffn(x1, w["pre_ffn_norm"], w["w_ffn_up"], w["ffn_conv_w"], w["ffn_conv_b"], fc0,
                   w["w_ffn_down"], w["post_ffn_norm"], t["ffn_nb"], t["ffn_T"], t["ffn_tf"])
    return xo, h1, rc1, s1, fc1, made


def kernel(x_prompt, x_sample, cache_mem_k, cache_mem_v, state_rnn_h, state_rnn_conv, state_hg,
           state_ffn_conv, mem_prompt, pre_mix_norm, w_in, rnn_conv_w, rnn_conv_b, lru_wa, lru_ba,
           lru_wx, lru_bx, lru_lambda, hg_lb, hg_norm, mem_norm, w_mem_kv, w_branch, b_gate, w_out,
           post_mix_norm, pre_ffn_norm, w_ffn_up, ffn_conv_w, ffn_conv_b, w_ffn_down, post_ffn_norm):
    depth = w_in.shape[0]
    Bp = x_prompt.shape[0]
    Bs = x_sample.shape[0]
    xa_w = XA_HEADS * XA_HD
    yp, ys = x_prompt, x_sample
    outs = [[] for _ in range(10)]
    for l in range(depth):
        w = {
            "pre_mix_norm": pre_mix_norm[l], "w_in": w_in[l],
            "rnn_conv_w": rnn_conv_w[l], "rnn_conv_b": rnn_conv_b[l],
            "lru_wa": _block_diag(lru_wa[l]).astype(BF16), "lru_ba": lru_ba[l],
            "lru_wx": _block_diag(lru_wx[l]).astype(BF16), "lru_bx": lru_bx[l],
            "lru_lambda": lru_lambda[l], "hg_lb": hg_lb, "hg_norm": hg_norm[l],
            "w_branch": w_branch[l], "b_gate": b_gate[l],
            "w_out": w_out[l], "post_mix_norm": post_mix_norm[l],
            "pre_ffn_norm": pre_ffn_norm[l], "w_ffn_up": w_ffn_up[l],
            "ffn_conv_w": ffn_conv_w[l], "ffn_conv_b": ffn_conv_b[l],
            "w_ffn_down": w_ffn_down[l], "post_ffn_norm": post_ffn_norm[l],
        }
        mem2 = mem_prompt.reshape(Bp * MEM_LEN, D_MODEL)
        kv = _norm_matmul(mem2, mem_norm[l], w_mem_kv[l], Bp * MEM_LEN, 1024, "mem_kv")
        kv = kv.reshape(Bp, MEM_LEN, 2 * xa_w)
        mk_p, mv_p = kv[..., :xa_w], kv[..., xa_w:]
        yp, h_p, rc_p, s_p, fc_p, made = _trunk_layer(
            yp, mk_p, mv_p, jnp.zeros((Bp, RNN_WIDTH), F32),
            jnp.zeros((Bp, RNN_CONV_W - 1, RNN_WIDTH), F32),
            jnp.zeros((Bp, HG_HEADS, HG_DK, HG_DV), F32),
            jnp.zeros((Bp, FFN_CONV_W - 1, FFN_DIM), F32), w, l)
        w = {**w, **made}
        ys, h_s, rc_s, s_s, fc_s, _ = _trunk_layer(
            ys, (cache_mem_k, cache_mem_v), None,
            state_rnn_h[l], state_rnn_conv[l], state_hg[l], state_ffn_conv[l], w, l)
        layer_out = (mk_p.reshape(Bp, MEM_LEN, XA_HEADS, XA_HD), mv_p.reshape(Bp, MEM_LEN, XA_HEADS, XA_HD),
                     h_p, rc_p, s_p, fc_p, h_s, rc_s, s_s, fc_s)
        for acc, val in zip(outs, layer_out):
            acc.append(val)
    return (yp, ys) + tuple(jnp.stack(o) for o in outs)
```

```python
import functools

import jax
import jax.numpy as jnp
from jax import lax
from jax.experimental import pallas as pl
from jax.experimental.pallas import tpu as pltpu

F32 = jnp.float32
BF16 = jnp.bfloat16

D_MODEL = 2048
RNN_WIDTH = 1024
RNN_BLOCKS = 16
RNN_CONV_W = 4
LRU_C = 8.0
HG_HEADS = 8
HG_DK = 128
HG_DV = 128
HG_CHUNK = 128
HG_SUB_MAX = 16
SUBLANES = 8
MXU_COLS = 256
HG_SKEW = 8
MEM_LEN = 256
XA_HEADS = 4
XA_HD = 256
BRANCH_WIDTH = 1024
N_BRANCH = 3
FFN_DIM = 5632
FFN_CONV_W = 3
FFN_OUT_ROWS = 16
EPS = 1e-6
LOG2E = 1.4426950408889634
IN_COLS = 6 * BRANCH_WIDTH + N_BRANCH * D_MODEL

COL_RNN, COL_HQ, COL_HF, COL_HI, COL_HO, COL_XQ = range(6)
GATE_COL0 = 3

HALO = 8
VMEM_LIMIT = 56 * 1024 * 1024


def _params(sem):
    return pltpu.CompilerParams(dimension_semantics=sem, vmem_limit_bytes=VMEM_LIMIT)


def _rms_scale(x):
    return lax.rsqrt(jnp.mean(x * x, axis=-1, keepdims=True) + EPS)


_sigmoid = jax.nn.sigmoid


def _sigmoid_tanh(x):
    return 0.5 * jnp.tanh(0.5 * x) + 0.5


def _with_casts(kernel, n_in, n_out, n_cast):
    def wrapped(*refs, **kw):
        ins, refs = refs[:n_in], refs[n_in:]
        cast_in, refs = refs[:n_cast], refs[n_cast:]
        outs, refs = refs[:n_out], refs[n_out:]
        cast_out, scratch = refs[:n_cast], refs[n_cast:]
        for src, dst in zip(cast_in, cast_out):
            dst[...] = src[...].astype(BF16)
        kernel(*ins, *outs, *scratch, **kw)
    return wrapped


def _cast_specs(casts, nsteps, step):
    in_specs, out_specs, out_shape = [], [], []
    for w in casts:
        rows, cols = w.shape[0] // nsteps, w.shape[1]
        assert rows * nsteps == w.shape[0] and rows % (2 * SUBLANES) == 0
        for specs in (in_specs, out_specs):
            specs.append(pl.BlockSpec((rows, cols), lambda *g, step=step: (step(*g), 0)))
        out_shape.append(jax.ShapeDtypeStruct(w.shape, BF16))
    return in_specs, out_specs, out_shape


def _norm_matmul_kernel(x_ref, g_ref, w_ref, *rest, emit_w, aliased):
    rest = rest[1:] if aliased else rest
    o_ref, xn_ref = rest[0], rest[-1]

    @pl.when(pl.program_id(1) == 0)
    def _():
        x = x_ref[...]
        xn_ref[...] = (x * _rms_scale(x) * g_ref[...]).astype(BF16)

    w = w_ref[...].astype(BF16)
    if emit_w:
        rest[1][...] = w
    o_ref[...] = jnp.dot(xn_ref[...], w, preferred_element_type=F32)


def _norm_matmul(x, g, w, tm, tn, name, tiles=None, emit_w=False, into=None):
    n, d = x.shape
    c = w.shape[1]
    t0, t1 = tiles if tiles is not None else (0, n // tm)
    assert not emit_w or t1 - t0 == 1
    in_specs = [
        pl.BlockSpec((tm, d), lambda i, j: (i + t0, 0)),
        pl.BlockSpec((1, d), lambda i, j: (0, 0)),
        pl.BlockSpec((d, tn), lambda i, j: (0, j)),
    ]
    args = [x, g.reshape(1, d), w]
    out_specs = [pl.BlockSpec((tm, tn), lambda i, j: (i + t0, j))]
    out_shape = [jax.ShapeDtypeStruct((n, c), F32)]
    if into is not None:
        in_specs.append(pl.BlockSpec(memory_space=pl.ANY))
        args.append(into)
    if emit_w:
        out_specs.append(pl.BlockSpec((d, tn), lambda i, j: (0, j)))
        out_shape.append(jax.ShapeDtypeStruct((d, c), BF16))
    res = pl.pallas_call(
        functools.partial(_norm_matmul_kernel, emit_w=emit_w, aliased=into is not None),
        grid=(t1 - t0, c // tn),
        in_specs=in_specs,
        out_specs=out_specs,
        out_shape=out_shape,
        scratch_shapes=[pltpu.VMEM((tm, d), BF16)],
        input_output_aliases={3: 0} if into is not None else {},
        compiler_params=_params(("arbitrary", "arbitrary")),
        name=name,
    )(*args)
    return res if emit_w else res[0]


def _rglru_kernel(z_ref, rc0_ref, h0_ref, cw_ref, cb_ref, wa_ref, ba_ref, wx_ref, bx_ref, lam_ref,
                  ya_ref, h1_ref, rc1_ref, xp_ref, a_ref, b_ref, hs_ref, hc_ref, *, nb, T):
    it = pl.program_id(1)
    nt = pl.num_programs(1)
    hist = RNN_CONV_W - 1

    @pl.when(it == 0)
    def _():
        xp_ref[:, HALO - hist:HALO, :] = rc0_ref[...]
        hc_ref[...] = h0_ref[...]

    xp_ref[:, HALO:HALO + T, :] = z_ref[...]
    cw = cw_ref[...]
    xr = cb_ref[...][None]
    for j in range(RNN_CONV_W):
        xr = xr + xp_ref[:, HALO - hist + j:HALO - hist + j + T, :] * cw[j:j + 1][None]
    xp_ref[:, HALO - hist:HALO, :] = xp_ref[:, HALO + T - hist:HALO + T, :]

    xr2 = xr.reshape(nb * T, RNN_WIDTH)
    xb = xr2.astype(BF16)
    def gate(w_ref, bias_ref):
        cols = [jnp.dot(xb[:, g * MXU_COLS:(g + 1) * MXU_COLS], w_ref[g], preferred_element_type=F32)
                for g in range(RNN_WIDTH // MXU_COLS)]
        return _sigmoid_tanh(jnp.concatenate(cols, axis=1) + bias_ref[...])

    r = gate(wa_ref, ba_ref)
    ig = gate(wx_ref, bx_ref)
    nl = -lam_ref[...]
    softplus = jnp.maximum(nl, 0.0) + jnp.log1p(jnp.exp(-jnp.abs(nl)))
    log_a = -LRU_C * r * softplus
    a = jnp.exp(log_a)
    a_ref[...] = a.reshape(nb, T, RNN_WIDTH)
    one_minus_a2 = -jnp.tanh(log_a) * (a * a + 1.0)
    b_ref[...] = (jnp.sqrt(one_minus_a2) * (ig * xr2)).reshape(nb, T, RNN_WIDTH)

    hs = [hc_ref[bi:bi + 1, :] for bi in range(nb)]
    for t in range(T):
        for bi in range(nb):
            hs[bi] = a_ref[bi, t:t + 1, :] * hs[bi] + b_ref[bi, t:t + 1, :]
            hs_ref[bi, t:t + 1, :] = hs[bi]
    for bi in range(nb):
        hc_ref[bi:bi + 1, :] = hs[bi]
    ya_ref[...] = hs_ref[...].astype(BF16)

    @pl.when(it == nt - 1)
    def _():
        h1_ref[...] = hc_ref[...]
        rc1_ref[...] = xp_ref[:, HALO - hist:HALO, :]


def _rglru(z3, rc0, h0, cw, cb, wa, ba, wx, bx, lam, nb, T, casts=()):
    B, L, _ = z3.shape
    C = RNN_WIDTH
    hist = RNN_CONV_W - 1
    vec = lambda: pl.BlockSpec((1, C), lambda b, t: (0, 0))
    mat = lambda: pl.BlockSpec((C // MXU_COLS, MXU_COLS, MXU_COLS), lambda b, t: (0, 0, 0))
    nt = L // T
    c_in, c_out, c_shape = _cast_specs(casts, (B // nb) * nt, lambda b, t: b * nt + t)
    return pl.pallas_call(
        functools.partial(_with_casts(_rglru_kernel, 10, 3, len(casts)), nb=nb, T=T),
        grid=(B // nb, nt),
        in_specs=[
            pl.BlockSpec((nb, T, C), lambda b, t: (b, t, COL_RNN)),
            pl.BlockSpec((nb, hist, C), lambda b, t: (b, 0, 0)),
            pl.BlockSpec((nb, C), lambda b, t: (b, 0)),
            pl.BlockSpec((RNN_CONV_W, C), lambda b, t: (0, 0)),
            vec(), mat(), vec(), mat(), vec(), vec(),
        ] + c_in,
        out_specs=[
            pl.BlockSpec((nb, T, C), lambda b, t: (b, t, 0)),
            pl.BlockSpec((nb, C), lambda b, t: (b, 0)),
            pl.BlockSpec((nb, hist, C), lambda b, t: (b, 0, 0)),
        ] + c_out,
        out_shape=[
            jax.ShapeDtypeStruct((B, L, C), BF16),
            jax.ShapeDtypeStruct((B, C), F32),
            jax.ShapeDtypeStruct((B, hist, C), F32),
        ] + c_shape,
        scratch_shapes=[
            pltpu.VMEM((nb, HALO + T, C), F32),
            pltpu.VMEM((nb, T, C), F32),
            pltpu.VMEM((nb, T, C), F32),
            pltpu.VMEM((nb, T, C), F32),
            pltpu.VMEM((nb, C), F32),
        ],
        compiler_params=_params(("arbitrary", "arbitrary")),
        name="rglru",
    )(z3, rc0, h0, cw, cb.reshape(1, C), wa, ba.reshape(1, C), wx, bx.reshape(1, C), lam.reshape(1, C),
      *casts)


def _dot_nt(a, b):
    return lax.dot_general(a, b, (((1,), (1,)), ((), ())), preferred_element_type=F32)


def _split3(x):
    hi = x.astype(BF16)
    r = x - hi.astype(F32)
    mid = r.astype(BF16)
    lo = (r - mid.astype(F32)).astype(BF16)
    return jnp.concatenate([hi, mid, lo], axis=0)


def _hg_sub_rows(cs):
    return HG_SUB_MAX if cs >= 4 * HG_SUB_MAX else SUBLANES


def _hgrn2_scores(qh, bc, ck, ck_row, st, cs):
    o = _dot_nt((qh * jnp.exp2(bc)).astype(BF16), st.astype(BF16))

    sub_rows = _hg_sub_rows(cs)
    nblk = cs // sub_rows
    a_rows = [None] * nblk
    half = cs // 2
    while half >= sub_rows:
        npair = cs // (2 * half)
        q_parts, k_parts = [], []
        for p in range(npair):
            lo, mid, hi = 2 * half * p, 2 * half * p + half, 2 * half * (p + 1)
            br = bc[mid - 1:mid]
            q_parts.append(qh[mid:hi] * jnp.exp2(bc[mid:hi] - br))
            k_parts += [jnp.exp2(br - ck[lo:mid]), jnp.zeros((half, HG_DK), F32)]
        off = _dot_nt(jnp.concatenate(q_parts, axis=0).astype(BF16),
                      jnp.concatenate(k_parts, axis=0).astype(BF16))
        if npair > 1:
            lg = half.bit_length() - 1
            rowp = lax.shift_right_logical(lax.broadcasted_iota(jnp.int32, off.shape, 0), lg)
            colp = lax.shift_right_logical(lax.broadcasted_iota(jnp.int32, off.shape, 1), lg + 1)
            off = jnp.where(rowp == colp, off, 0.0)
        for p in range(npair):
            for sub in range(half // sub_rows):
                blk = (2 * half * p + half) // sub_rows + sub
                piece = off[p * half + sub * sub_rows:p * half + (sub + 1) * sub_rows]
                a_rows[blk] = piece if a_rows[blk] is None else a_rows[blk] + piece
        half //= 2

    ones = jnp.ones((HG_DK, HG_DK), BF16)
    zero = jnp.zeros((SUBLANES, HG_DK), F32)
    groups = sub_rows // SUBLANES
    lane_sums = []
    for blk in range(nblk):
        r0 = blk * sub_rows
        bg = [bc[r0 + v * SUBLANES:r0 + (v + 1) * SUBLANES] for v in range(groups)]
        qg = [qh[r0 + v * SUBLANES:r0 + (v + 1) * SUBLANES] for v in range(groups)]
        pair = []
        for s in range(sub_rows):
            cks = ck_row(r0 + s)
            pair += [qg[v] * jnp.exp2(bg[v] - cks) if v >= s // SUBLANES else zero
                     for v in range(groups)]
        lane_sums.append(jnp.dot(jnp.concatenate(pair, axis=0).astype(BF16), ones,
                                 preferred_element_type=F32))
    return o, a_rows, lane_sums


def _hgrn2_output(scores, vh, ogh, gnh, bl, ck, st, cs):
    o, a_rows, lane_sums = scores
    sub_rows = _hg_sub_rows(cs)
    groups = sub_rows // SUBLANES
    lane = lax.broadcasted_iota(jnp.int32, (SUBLANES, HG_DK), 1)
    tloc = lax.broadcasted_iota(jnp.int32, (SUBLANES, HG_DK), 0)
    mask = [[(lane == s) & (tloc >= s - v * SUBLANES) if s > v * SUBLANES else (lane == s)
             for v in range(groups)] for s in range(sub_rows)]
    a_rows = list(a_rows)
    for blk, lane_sum in enumerate(lane_sums):
        a_g = [jnp.zeros((SUBLANES, HG_DK), F32)] * groups
        for s in range(sub_rows):
            for v in range(s // SUBLANES, groups):
                r = s * sub_rows + v * SUBLANES
                a_g[v] = jnp.where(mask[s][v], lane_sum[r:r + SUBLANES], a_g[v])
        a_d = jnp.concatenate(a_g, axis=0)
        if blk:
            a_d = pltpu.roll(a_d, blk * sub_rows, 1)
        a_d = a_d[:, :cs]
        a_rows[blk] = a_d if a_rows[blk] is None else a_rows[blk] + a_d
    a = jnp.concatenate(a_rows, axis=0)
    o = o + jnp.dot(a.astype(BF16), vh.astype(BF16), preferred_element_type=F32)
    kd = jnp.exp2(bl - ck).astype(BF16)
    st_new = st * jnp.exp2(bl) + jnp.dot(vh.T.astype(BF16), kd, preferred_element_type=F32)
    y = o * _rms_scale(o) * gnh * _sigmoid(ogh)
    return y, st_new


def _hgrn2_kernel(q_ref, f_ref, v_ref, og_ref, s0_ref, lbp_ref, gn_ref, yb_ref, s1_ref,
                  lb_ref, bc_ref, ck_ref, *st_refs, nb, T, cs, layer):
    it = pl.program_id(1)
    nt = pl.num_programs(1)
    row = lax.broadcasted_iota(jnp.int32, (cs, 3 * cs), 0)
    col = lax.broadcasted_iota(jnp.int32, (cs, 3 * cs), 1)
    tri3 = (row >= (col & (cs - 1))).astype(BF16)

    @pl.when(it == 0)
    def _():
        p = lbp_ref[...]
        e = jnp.exp(p - jnp.max(p, axis=0, keepdims=True))
        sm = e / jnp.sum(e, axis=0, keepdims=True)
        lb_ref[...] = jnp.sum(sm[:layer + 1], axis=0, keepdims=True)

        def init(b, c):
            for h in range(HG_HEADS):
                st_refs[h][b] = s0_ref[b, h].T
            return c
        lax.fori_loop(0, nb, init, 0)

    nch = T // cs
    par = 2 if nb % 2 == 0 else 1

    def chunk(idx, c):
        rows = pl.ds(pl.multiple_of((idx % nch) * cs, cs), cs)
        lb = lb_ref[...]
        units = []
        for k in range(par):
            b = (idx // nch) * par + k
            f = lb + (1.0 - lb) * _sigmoid(f_ref[b, rows, :])
            bc = jnp.dot(tri3, _split3(jnp.log(f) * LOG2E), preferred_element_type=F32)
            bc_ref[k] = bc
            ck_ref[k] = bc - jnp.log(1.0 - f) * LOG2E
            units += [(k, b, h) for h in range(HG_HEADS)]

        def scores(k, b, h):
            hl = slice(h * HG_DK, (h + 1) * HG_DK)
            return _hgrn2_scores(q_ref[b, rows, hl], bc_ref[k, :, hl], ck_ref[k, :, hl],
                                 lambda r: ck_ref[k, r:r + 1, hl], st_refs[h][b], cs)

        def output(k, b, h, sc):
            hl = slice(h * HG_DK, (h + 1) * HG_DK)
            y, st_new = _hgrn2_output(sc, v_ref[b, rows, hl], og_ref[b, rows, hl], gn_ref[...],
                                      bc_ref[k, cs - 1:cs, hl], ck_ref[k, :, hl], st_refs[h][b], cs)
            yb_ref[b, rows, hl] = y.astype(BF16)
            st_refs[h][b] = st_new

        pending = {}
        for n in range(len(units) + HG_SKEW):
            if n < len(units):
                pending[n] = scores(*units[n])
            if n >= HG_SKEW:
                output(*units[n - HG_SKEW], pending.pop(n - HG_SKEW))
        return c
    lax.fori_loop(0, (nb // par) * nch, chunk, 0)

    @pl.when(it == nt - 1)
    def _():
        def fin(b, c):
            for h in range(HG_HEADS):
                s1_ref[b, h] = st_refs[h][b].T
            return c
        lax.fori_loop(0, nb, fin, 0)


def _hgrn2(z3, s0, hg_lb, gn, layer, nb, T, casts=()):
    B, L, _ = z3.shape
    C = HG_HEADS * HG_DK
    cs = min(HG_CHUNK, L)
    zcol = lambda col: pl.BlockSpec((nb, T, C), lambda b, t: (b, t, col))
    st_spec = lambda: pl.BlockSpec((nb, HG_HEADS, HG_DK, HG_DV), lambda b, t: (b, 0, 0, 0))
    nl = hg_lb.shape[0]
    nt = L // T
    c_in, c_out, c_shape = _cast_specs(casts, (B // nb) * nt, lambda b, t: b * nt + t)
    return pl.pallas_call(
        functools.partial(_with_casts(_hgrn2_kernel, 7, 2, len(casts)), nb=nb, T=T, cs=cs, layer=layer),
        grid=(B // nb, nt),
        in_specs=[
            zcol(COL_HQ), zcol(COL_HF), zcol(COL_HI), zcol(COL_HO),
            st_spec(),
            pl.BlockSpec((nl, C), lambda b, t: (0, 0)),
            pl.BlockSpec((1, HG_DV), lambda b, t: (0, 0)),
        ] + c_in,
        out_specs=[pl.BlockSpec((nb, T, C), lambda b, t: (b, t, 0)), st_spec()] + c_out,
        out_shape=[
            jax.ShapeDtypeStruct((B, L, C), BF16),
            jax.ShapeDtypeStruct((B, HG_HEADS, HG_DK, HG_DV), F32),
        ] + c_shape,
        scratch_shapes=[
            pltpu.VMEM((1, C), F32),
            pltpu.VMEM((2, cs, C), F32),
            pltpu.VMEM((2, cs, C), F32),
        ] + [pltpu.VMEM((nb, HG_DV, HG_DK), F32) for _ in range(HG_HEADS)],
        compiler_params=_params(("arbitrary", "arbitrary")),
        name="hgrn2",
    )(z3, z3, z3, z3, s0, hg_lb, gn.reshape(1, HG_DV), *casts)


def _softmax_rows(s):
    e = jnp.exp(s - jnp.max(s, axis=-1, keepdims=True))
    return e / jnp.sum(e, axis=-1, keepdims=True)


def _xattn_kernel(q_ref, k_ref, v_ref, o_ref):
    hls = [slice(h * XA_HD, (h + 1) * XA_HD) for h in range(XA_HEADS)]
    s = [_dot_nt(q_ref[0, :, hl].astype(BF16), k_ref[0, :, hl].astype(BF16)) * (XA_HD ** -0.5)
         for hl in hls]
    p = [_softmax_rows(sh).astype(BF16) for sh in s]
    for hl, ph in zip(hls, p):
        o = jnp.dot(ph, v_ref[0, :, hl].astype(BF16), preferred_element_type=F32)
        o_ref[0, :, hl] = o.astype(BF16)


def _xattn(z3, mk, mv, T):
    B, L, _ = z3.shape
    C = XA_HEADS * XA_HD
    mem = lambda: pl.BlockSpec((1, MEM_LEN, C), lambda b, t: (b, 0, 0))
    return pl.pallas_call(
        _xattn_kernel,
        grid=(B, L // T),
        in_specs=[pl.BlockSpec((1, T, C), lambda b, t: (b, t, COL_XQ)), mem(), mem()],
        out_specs=pl.BlockSpec((1, T, C), lambda b, t: (b, t, 0)),
        out_shape=jax.ShapeDtypeStruct((B, L, C), BF16),
        compiler_params=_params(("arbitrary", "arbitrary")),
        name="xattn",
    )(z3, mk, mv)


def _xattn_cached_kernel(q_ref, k_ref, v_ref, o_ref):
    kt = jnp.swapaxes(k_ref[...], 0, 1)
    vt = jnp.swapaxes(v_ref[...], 0, 1)
    hls = [slice(h * XA_HD, (h + 1) * XA_HD) for h in range(XA_HEADS)]
    s = [_dot_nt(q_ref[0, :, hl].astype(BF16), kt[h].astype(BF16)) * (XA_HD ** -0.5)
         for h, hl in enumerate(hls)]
    p = [_softmax_rows(sh).astype(BF16) for sh in s]
    for h, hl in enumerate(hls):
        o = jnp.dot(p[h], vt[h].astype(BF16), preferred_element_type=F32)
        o_ref[0, :, hl] = o.astype(BF16)


def _xattn_cached(z3, cache_k, cache_v, layer):
    B, L, _ = z3.shape
    C = XA_HEADS * XA_HD
    mem = lambda: pl.BlockSpec((None, None, MEM_LEN, XA_HEADS, XA_HD), lambda b: (layer, b, 0, 0, 0))
    return pl.pallas_call(
        _xattn_cached_kernel,
        grid=(B,),
        in_specs=[pl.BlockSpec((1, L, C), lambda b: (b, 0, COL_XQ)), mem(), mem()],
        out_specs=pl.BlockSpec((1, L, C), lambda b: (b, 0, 0)),
        out_shape=jax.ShapeDtypeStruct((B, L, C), BF16),
        compiler_params=_params(("arbitrary",)),
        name="xattn_cached",
    )(z3, cache_k, cache_v)


def _merge_kernel(x_ref, ya_ref, yb_ref, yc_ref, g0_ref, g1_ref, g2_ref, bg_ref, wb_ref, wo_ref,
                  pg_ref, o_ref):
    m = jnp.zeros(o_ref.shape, F32)
    for nb, (y_ref, g_ref) in enumerate(((ya_ref, g0_ref), (yb_ref, g1_ref), (yc_ref, g2_ref))):
        gate = _sigmoid(g_ref[...] + bg_ref[nb:nb + 1, :])
        m = m + gate * jnp.dot(y_ref[...], wb_ref[nb], preferred_element_type=F32)
    y = jnp.dot(m.astype(BF16), wo_ref[...], preferred_element_type=F32)
    o_ref[...] = x_ref[...] + y * _rms_scale(y) * pg_ref[...]


def _merge(x, ya, yb, yc, z, b_gate, wb, wo, pg, tm):
    n, d = x.shape
    c = BRANCH_WIDTH
    once = pl.Buffered(1)
    branch = lambda: pl.BlockSpec((tm, c), lambda i: (i, 0))
    gate = lambda nb: pl.BlockSpec((tm, d), lambda i: (i, GATE_COL0 + nb))
    return pl.pallas_call(
        _merge_kernel,
        grid=(n // tm,),
        in_specs=[
            pl.BlockSpec((tm, d), lambda i: (i, 0)),
            branch(), branch(), branch(),
            gate(0), gate(1), gate(2),
            pl.BlockSpec((N_BRANCH, d), lambda i: (0, 0)),
            pl.BlockSpec((N_BRANCH, c, d), lambda i: (0, 0, 0), pipeline_mode=once),
            pl.BlockSpec((d, d), lambda i: (0, 0), pipeline_mode=once),
            pl.BlockSpec((1, d), lambda i: (0, 0)),
        ],
        out_specs=pl.BlockSpec((tm, d), lambda i: (i, 0)),
        out_shape=jax.ShapeDtypeStruct((n, d), F32),
        compiler_params=_params(("arbitrary",)),
        name="merge",
    )(x, ya, yb, yc, z, z, z, b_gate, wb, wo, pg.reshape(1, d))


def _ffn_kernel(x_ref, g_ref, wu_ref, wv_ref, cw_ref, cb_ref, fc0_ref, wd_ref, pg_ref,
                o_ref, fc1_ref, xn_ref, acc_ref, hal_ref, car_ref, *, nb, T, tf):
    it = pl.program_id(1)
    j = pl.program_id(2)
    nj = pl.num_programs(2)
    hist = FFN_CONV_W - 1

    @pl.when(j == 0)
    def _():
        x = x_ref[...].reshape(nb * T, D_MODEL)
        xn_ref[...] = (x * _rms_scale(x) * g_ref[...]).astype(BF16)
        acc_ref[...] = jnp.zeros_like(acc_ref)

    @pl.when(it == 0)
    def _():
        car_ref[j] = fc0_ref[...]

    xn = xn_ref[...]
    u = jnp.dot(xn, wu_ref[...], preferred_element_type=F32)
    v = jnp.dot(xn, wv_ref[...], preferred_element_type=F32)
    hal_ref[:, HALO - hist:HALO, :] = car_ref[j]
    hal_ref[:, HALO:HALO + T, :] = u.reshape(nb, T, tf)
    cw = cw_ref[...]
    uc = cb_ref[...][None]
    for jj in range(FFN_CONV_W):
        uc = uc + hal_ref[:, HALO - hist + jj:HALO - hist + jj + T, :] * cw[jj:jj + 1][None]
    tail = hal_ref[:, HALO + T - hist:HALO + T, :]
    car_ref[j] = tail
    act = (jax.nn.gelu(uc).reshape(nb * T, tf) * v).astype(BF16)
    acc_ref[...] += jnp.dot(act, wd_ref[...], preferred_element_type=F32)

    @pl.when(j == nj - 1)
    def _():
        pg = pg_ref[...]
        for b in range(nb):
            for r in range(0, T, FFN_OUT_ROWS):
                y = acc_ref[b * T + r:b * T + r + FFN_OUT_ROWS, :]
                o_ref[b, r:r + FFN_OUT_ROWS, :] = x_ref[b, r:r + FFN_OUT_ROWS, :] + y * _rms_scale(y) * pg

    @pl.when((j == nj - 1) & (it == pl.num_programs(1) - 1))
    def _():
        for jj in range(FFN_DIM // tf):
            fc1_ref[:, :, jj * tf:(jj + 1) * tf] = car_ref[jj]


def _ffn(x3, g, w_up, cw, cb, fc0, w_down, pg, nb, T, tf):
    B, L, d = x3.shape
    F = FFN_DIM
    nj = F // tf
    hist = FFN_CONV_W - 1
    rows = lambda: pl.BlockSpec((nb, T, d), lambda b, t, j: (b, t, 0))
    return pl.pallas_call(
        functools.partial(_ffn_kernel, nb=nb, T=T, tf=tf),
        grid=(B // nb, L // T, nj),
        in_specs=[
            rows(),
            pl.BlockSpec((1, d), lambda b, t, j: (0, 0)),
            pl.BlockSpec((d, tf), lambda b, t, j: (0, j)),
            pl.BlockSpec((d, tf), lambda b, t, j: (0, nj + j)),
            pl.BlockSpec((FFN_CONV_W, tf), lambda b, t, j: (0, j)),
            pl.BlockSpec((1, tf), lambda b, t, j: (0, j)),
            pl.BlockSpec((nb, hist, tf), lambda b, t, j: (b, 0, j)),
            pl.BlockSpec((tf, d), lambda b, t, j: (j, 0)),
            pl.BlockSpec((1, d), lambda b, t, j: (0, 0)),
        ],
        out_specs=[rows(), pl.BlockSpec((nb, hist, F), lambda b, t, j: (b, 0, 0))],
        out_shape=[jax.ShapeDtypeStruct((B, L, d), F32), jax.ShapeDtypeStruct((B, hist, F), F32)],
        scratch_shapes=[
            pltpu.VMEM((nb * T, d), BF16),
            pltpu.VMEM((nb * T, d), F32),
            pltpu.VMEM((nb, HALO + T, tf), F32),
            pltpu.VMEM((nj, nb, hist, tf), F32),
        ],
        compiler_params=_params(("arbitrary", "arbitrary", "arbitrary")),
        name="ffn",
    )(x3, g.reshape(1, d), w_up, w_up, cw, cb.reshape(1, F), fc0, w_down, pg.reshape(1, d))


def _block_diag(w):
    nblk, bi, bj = w.shape
    per = MXU_COLS // bi
    eye = jnp.eye(per, dtype=w.dtype)
    grouped = jnp.einsum("ghij,hk->ghikj", w.reshape(nblk // per, per, bi, bj), eye)
    return grouped.reshape(nblk // per, per * bi, per * bj)


def _tiles(B, L):
    T = min(L, 256)
    return dict(
        proj_tm=min(B * L, 1024), proj_tn=2048,
        rnn_nb=B if B * T <= 512 else 512 // T, rnn_T=T,
        hg_nb=min(B, 4), hg_T=T,
        xa_T=min(L, 512),
        merge_tm=256,
        ffn_nb=B if B * T <= 512 else 512 // T, ffn_T=T, ffn_tf=512,
    )


def _trunk_layer(x, mk, mv, h0, rc0, s0, fc0, w, layer):
    B, L, d = x.shape
    n = B * L
    t = _tiles(B, L)
    x2 = x.reshape(n, d)
    made = {}
    tm, tn = t["proj_tm"], t["proj_tn"]
    if w["w_in"].dtype == F32:
        z, made["w_in"] = _norm_matmul(x2, w["pre_mix_norm"], w["w_in"], tm, tn // 2, "in_proj",
                                       tiles=(0, 1), emit_w=True)
        if n > tm:
            z = _norm_matmul(x2, w["pre_mix_norm"], made["w_in"], tm, tn, "in_proj",
                             tiles=(1, n // tm), into=z)
    else:
        z = _norm_matmul(x2, w["pre_mix_norm"], w["w_in"], tm, tn, "in_proj")
    z3 = z.reshape(B, L, IN_COLS)
    c = BRANCH_WIDTH
    fresh = w["w_out"].dtype == F32
    ya, h1, rc1, *cast = _rglru(
        z3, rc0, h0, w["rnn_conv_w"], w["rnn_conv_b"], w["lru_wa"], w["lru_ba"], w["lru_wx"],
        w["lru_bx"], w["lru_lambda"], t["rnn_nb"], t["rnn_T"],
        casts=(w["w_branch"].reshape(N_BRANCH * c, d), w["w_out"]) if fresh else ())
    if fresh:
        made["w_branch"], made["w_out"] = cast[0].reshape(N_BRANCH, c, d), cast[1]
    yb, s1, *cast = _hgrn2(z3, s0, w["hg_lb"], w["hg_norm"], layer, t["hg_nb"], t["hg_T"],
                           casts=(w["w_ffn_up"], w["w_ffn_down"]) if fresh else ())
    if fresh:
        made["w_ffn_up"], made["w_ffn_down"] = cast
    w = {**w, **made}
    yc = _xattn(z3, mk, mv, t["xa_T"]) if mv is not None else _xattn_cached(z3, *mk, layer)
    x1 = _merge(x2, ya.reshape(n, c), yb.reshape(n, c), yc.reshape(n, c), z, w["b_gate"],
                w["w_branch"], w["w_out"], w["post_mix_norm"], t["merge_tm"]).reshape(B, L, d)
    xo, fc1 = _ffn(x1, w["pre_ffn_norm"], w["w_ffn_up"], w["ffn_conv_w"], w["ffn_conv_b"], fc0,
                   w["w_ffn_down"], w["post_ffn_norm"], t["ffn_nb"], t["ffn_T"], t["ffn_tf"])
    return xo, h1, rc1, s1, fc1, made


def kernel(x_prompt, x_sample, cache_mem_k, cache_mem_v, state_rnn_h, state_rnn_conv, state_hg,
           state_ffn_conv, mem_prompt, pre_mix_norm, w_in, rnn_conv_w, rnn_conv_b, lru_wa, lru_ba,
           lru_wx, lru_bx, lru_lambda, hg_lb, hg_norm, mem_norm, w_mem_kv, w_branch, b_gate, w_out,
           post_mix_norm, pre_ffn_norm, w_ffn_up, ffn_conv_w, ffn_conv_b, w_ffn_down, post_ffn_norm):
    depth = w_in.shape[0]
    Bp = x_prompt.shape[0]
    Bs = x_sample.shape[0]
    xa_w = XA_HEADS * XA_HD
    yp, ys = x_prompt, x_sample
    outs = [[] for _ in range(10)]
    for l in range(depth):
        w = {
            "pre_mix_norm": pre_mix_norm[l], "w_in": w_in[l],
            "rnn_conv_w": rnn_conv_w[l], "rnn_conv_b": rnn_conv_b[l],
            "lru_wa": _block_diag(lru_wa[l]).astype(BF16), "lru_ba": lru_ba[l],
            "lru_wx": _block_diag(lru_wx[l]).astype(BF16), "lru_bx": lru_bx[l],
            "lru_lambda": lru_lambda[l], "hg_lb": hg_lb, "hg_norm": hg_norm[l],
            "w_branch": w_branch[l], "b_gate": b_gate[l],
            "w_out": w_out[l], "post_mix_norm": post_mix_norm[l],
            "pre_ffn_norm": pre_ffn_norm[l], "w_ffn_up": w_ffn_up[l],
            "ffn_conv_w": ffn_conv_w[l], "ffn_conv_b": ffn_conv_b[l],
            "w_ffn_down": w_ffn_down[l], "post_ffn_norm": post_ffn_norm[l],
        }
        mem2 = mem_prompt.reshape(Bp * MEM_LEN, D_MODEL)
        kv = _norm_matmul(mem2, mem_norm[l], w_mem_kv[l], Bp * MEM_LEN, 1024, "mem_kv")
        kv = kv.reshape(Bp, MEM_LEN, 2 * xa_w)
        mk_p, mv_p = kv[..., :xa_w], kv[..., xa_w:]
        yp, h_p, rc_p, s_p, fc_p, made = _trunk_layer(
            yp, mk_p, mv_p, jnp.zeros((Bp, RNN_WIDTH), F32),
            jnp.zeros((Bp, RNN_CONV_W - 1, RNN_WIDTH), F32),
            jnp.zeros((Bp, HG_HEADS, HG_DK, HG_DV), F32),
            jnp.zeros((Bp, FFN_CONV_W - 1, FFN_DIM), F32), w, l)
        w = {**w, **made}
        ys, h_s, rc_s, s_s, fc_s, _ = _trunk_layer(
            ys, (cache_mem_k, cache_mem_v), None,
            state_rnn_h[l], state_rnn_conv[l], state_hg[l], state_ffn_conv[l], w, l)
        layer_out = (mk_p.reshape(Bp, MEM_LEN, XA_HEADS, XA_HD), mv_p.reshape(Bp, MEM_LEN, XA_HEADS, XA_HD),
                     h_p, rc_p, s_p, fc_p, h_s, rc_s, s_s, fc_s)
        for acc, val in zip(outs, layer_out):
            acc.append(val)
    return (yp, ys) + tuple(jnp.stack(o) for o in outs)
```

```python
import functools

import jax
import jax.numpy as jnp
from jax import lax
from jax.experimental import pallas as pl
from jax.experimental.pallas import tpu as pltpu

F32 = jnp.float32
BF16 = jnp.bfloat16

D_MODEL = 2048
RNN_WIDTH = 1024
RNN_BLOCKS = 16
RNN_CONV_W = 4
LRU_C = 8.0
HG_HEADS = 8
HG_DK = 128
HG_DV = 128
HG_CHUNK = 128
HG_SUB_MAX = 16
SUBLANES = 8
MXU_COLS = 256
HG_SKEW = 8
MEM_LEN = 256
XA_HEADS = 4
XA_HD = 256
BRANCH_WIDTH = 1024
N_BRANCH = 3
FFN_DIM = 5632
FFN_CONV_W = 3
FFN_OUT_ROWS = 8
EPS = 1e-6
LOG2E = 1.4426950408889634
IN_COLS = 6 * BRANCH_WIDTH + N_BRANCH * D_MODEL

COL_RNN, COL_HQ, COL_HF, COL_HI, COL_HO, COL_XQ = range(6)
GATE_COL0 = 3

HALO = 8
VMEM_LIMIT = 56 * 1024 * 1024


def _params(sem):
    return pltpu.CompilerParams(dimension_semantics=sem, vmem_limit_bytes=VMEM_LIMIT)


def _rms_scale(x):
    return lax.rsqrt(jnp.mean(x * x, axis=-1, keepdims=True) + EPS)


_sigmoid = jax.nn.sigmoid


def _sigmoid_tanh(x):
    return 0.5 * jnp.tanh(0.5 * x) + 0.5


def _with_casts(kernel, n_in, n_out, n_cast):
    def wrapped(*refs, **kw):
        ins, refs = refs[:n_in], refs[n_in:]
        cast_in, refs = refs[:n_cast], refs[n_cast:]
        outs, refs = refs[:n_out], refs[n_out:]
        cast_out, scratch = refs[:n_cast], refs[n_cast:]
        for src, dst in zip(cast_in, cast_out):
            dst[...] = src[...].astype(BF16)
        kernel(*ins, *outs, *scratch, **kw)
    return wrapped


def _cast_specs(casts, nsteps, step):
    in_specs, out_specs, out_shape = [], [], []
    for w in casts:
        rows, cols = w.shape[0] // nsteps, w.shape[1]
        assert rows * nsteps == w.shape[0] and rows % (2 * SUBLANES) == 0
        for specs in (in_specs, out_specs):
            specs.append(pl.BlockSpec((rows, cols), lambda *g, step=step: (step(*g), 0)))
        out_shape.append(jax.ShapeDtypeStruct(w.shape, BF16))
    return in_specs, out_specs, out_shape


def _norm_matmul_kernel(x_ref, g_ref, w_ref, *rest, emit_w, aliased):
    rest = rest[1:] if aliased else rest
    o_ref, xn_ref = rest[0], rest[-1]

    @pl.when(pl.program_id(1) == 0)
    def _():
        x = x_ref[...]
        xn_ref[...] = (x * _rms_scale(x) * g_ref[...]).astype(BF16)

    w = w_ref[...].astype(BF16)
    if emit_w:
        rest[1][...] = w
    o_ref[...] = jnp.dot(xn_ref[...], w, preferred_element_type=F32)


def _norm_matmul(x, g, w, tm, tn, name, tiles=None, emit_w=False, into=None):
    n, d = x.shape
    c = w.shape[1]
    t0, t1 = tiles if tiles is not None else (0, n // tm)
    assert not emit_w or t1 - t0 == 1
    in_specs = [
        pl.BlockSpec((tm, d), lambda i, j: (i + t0, 0)),
        pl.BlockSpec((1, d), lambda i, j: (0, 0)),
        pl.BlockSpec((d, tn), lambda i, j: (0, j)),
    ]
    args = [x, g.reshape(1, d), w]
    out_specs = [pl.BlockSpec((tm, tn), lambda i, j: (i + t0, j))]
    out_shape = [jax.ShapeDtypeStruct((n, c), F32)]
    if into is not None:
        in_specs.append(pl.BlockSpec(memory_space=pl.ANY))
        args.append(into)
    if emit_w:
        out_specs.append(pl.BlockSpec((d, tn), lambda i, j: (0, j)))
        out_shape.append(jax.ShapeDtypeStruct((d, c), BF16))
    res = pl.pallas_call(
        functools.partial(_norm_matmul_kernel, emit_w=emit_w, aliased=into is not None),
        grid=(t1 - t0, c // tn),
        in_specs=in_specs,
        out_specs=out_specs,
        out_shape=out_shape,
        scratch_shapes=[pltpu.VMEM((tm, d), BF16)],
        input_output_aliases={3: 0} if into is not None else {},
        compiler_params=_params(("arbitrary", "arbitrary")),
        name=name,
    )(*args)
    return res if emit_w else res[0]


def _rglru_kernel(z_ref, rc0_ref, h0_ref, cw_ref, cb_ref, wa_ref, ba_ref, wx_ref, bx_ref, lam_ref,
                  ya_ref, h1_ref, rc1_ref, xp_ref, a_ref, b_ref, hs_ref, hc_ref, *, nb, T):
    it = pl.program_id(1)
    nt = pl.num_programs(1)
    hist = RNN_CONV_W - 1

    @pl.when(it == 0)
    def _():
        xp_ref[:, HALO - hist:HALO, :] = rc0_ref[...]
        hc_ref[...] = h0_ref[...]

    xp_ref[:, HALO:HALO + T, :] = z_ref[...]
    cw = cw_ref[...]
    xr = cb_ref[...][None]
    for j in range(RNN_CONV_W):
        xr = xr + xp_ref[:, HALO - hist + j:HALO - hist + j + T, :] * cw[j:j + 1][None]
    xp_ref[:, HALO - hist:HALO, :] = xp_ref[:, HALO + T - hist:HALO + T, :]

    xr2 = xr.reshape(nb * T, RNN_WIDTH)
    xb = xr2.astype(BF16)
    def gate(w_ref, bias_ref):
        cols = [jnp.dot(xb[:, g * MXU_COLS:(g + 1) * MXU_COLS], w_ref[g], preferred_element_type=F32)
                for g in range(RNN_WIDTH // MXU_COLS)]
        return _sigmoid_tanh(jnp.concatenate(cols, axis=1) + bias_ref[...])

    r = gate(wa_ref, ba_ref)
    ig = gate(wx_ref, bx_ref)
    nl = -lam_ref[...]
    softplus = jnp.maximum(nl, 0.0) + jnp.log1p(jnp.exp(-jnp.abs(nl)))
    log_a = -LRU_C * r * softplus
    a = jnp.exp(log_a)
    a_ref[...] = a.reshape(nb, T, RNN_WIDTH)
    one_minus_a2 = -jnp.tanh(log_a) * (a * a + 1.0)
    b_ref[...] = (jnp.sqrt(one_minus_a2) * (ig * xr2)).reshape(nb, T, RNN_WIDTH)

    hs = [hc_ref[bi:bi + 1, :] for bi in range(nb)]
    for t in range(T):
        for bi in range(nb):
            hs[bi] = a_ref[bi, t:t + 1, :] * hs[bi] + b_ref[bi, t:t + 1, :]
            hs_ref[bi, t:t + 1, :] = hs[bi]
    for bi in range(nb):
        hc_ref[bi:bi + 1, :] = hs[bi]
    ya_ref[...] = hs_ref[...].astype(BF16)

    @pl.when(it == nt - 1)
    def _():
        h1_ref[...] = hc_ref[...]
        rc1_ref[...] = xp_ref[:, HALO - hist:HALO, :]


def _rglru(z3, rc0, h0, cw, cb, wa, ba, wx, bx, lam, nb, T, casts=()):
    B, L, _ = z3.shape
    C = RNN_WIDTH
    hist = RNN_CONV_W - 1
    vec = lambda: pl.BlockSpec((1, C), lambda b, t: (0, 0))
    mat = lambda: pl.BlockSpec((C // MXU_COLS, MXU_COLS, MXU_COLS), lambda b, t: (0, 0, 0))
    nt = L // T
    c_in, c_out, c_shape = _cast_specs(casts, (B // nb) * nt, lambda b, t: b * nt + t)
    return pl.pallas_call(
        functools.partial(_with_casts(_rglru_kernel, 10, 3, len(casts)), nb=nb, T=T),
        grid=(B // nb, nt),
        in_specs=[
            pl.BlockSpec((nb, T, C), lambda b, t: (b, t, COL_RNN)),
            pl.BlockSpec((nb, hist, C), lambda b, t: (b, 0, 0)),
            pl.BlockSpec((nb, C), lambda b, t: (b, 0)),
            pl.BlockSpec((RNN_CONV_W, C), lambda b, t: (0, 0)),
            vec(), mat(), vec(), mat(), vec(), vec(),
        ] + c_in,
        out_specs=[
            pl.BlockSpec((nb, T, C), lambda b, t: (b, t, 0)),
            pl.BlockSpec((nb, C), lambda b, t: (b, 0)),
            pl.BlockSpec((nb, hist, C), lambda b, t: (b, 0, 0)),
        ] + c_out,
        out_shape=[
            jax.ShapeDtypeStruct((B, L, C), BF16),
            jax.ShapeDtypeStruct((B, C), F32),
            jax.ShapeDtypeStruct((B, hist, C), F32),
        ] + c_shape,
        scratch_shapes=[
            pltpu.VMEM((nb, HALO + T, C), F32),
            pltpu.VMEM((nb, T, C), F32),
            pltpu.VMEM((nb, T, C), F32),
            pltpu.VMEM((nb, T, C), F32),
            pltpu.VMEM((nb, C), F32),
        ],
        compiler_params=_params(("arbitrary", "arbitrary")),
        name="rglru",
    )(z3, rc0, h0, cw, cb.reshape(1, C), wa, ba.reshape(1, C), wx, bx.reshape(1, C), lam.reshape(1, C),
      *casts)


def _dot_nt(a, b):
    return lax.dot_general(a, b, (((1,), (1,)), ((), ())), preferred_element_type=F32)


def _split3(x):
    hi = x.astype(BF16)
    r = x - hi.astype(F32)
    mid = r.astype(BF16)
    lo = (r - mid.astype(F32)).astype(BF16)
    return jnp.concatenate([hi, mid, lo], axis=0)


def _hg_sub_rows(cs):
    return HG_SUB_MAX if cs >= 4 * HG_SUB_MAX else SUBLANES


def _hgrn2_scores(qh, bc, ck, ck_row, st, cs):
    o = _dot_nt((qh * jnp.exp2(bc)).astype(BF16), st.astype(BF16))

    sub_rows = _hg_sub_rows(cs)
    nblk = cs // sub_rows
    a_rows = [None] * nblk
    half = cs // 2
    while half >= sub_rows:
        npair = cs // (2 * half)
        q_parts, k_parts = [], []
        for p in range(npair):
            lo, mid, hi = 2 * half * p, 2 * half * p + half, 2 * half * (p + 1)
            br = bc[mid - 1:mid]
            q_parts.append(qh[mid:hi] * jnp.exp2(bc[mid:hi] - br))
            k_parts += [jnp.exp2(br - ck[lo:mid]), jnp.zeros((half, HG_DK), F32)]
        off = _dot_nt(jnp.concatenate(q_parts, axis=0).astype(BF16),
                      jnp.concatenate(k_parts, axis=0).astype(BF16))
        if npair > 1:
            lg = half.bit_length() - 1
            rowp = lax.shift_right_logical(lax.broadcasted_iota(jnp.int32, off.shape, 0), lg)
            colp = lax.shift_right_logical(lax.broadcasted_iota(jnp.int32, off.shape, 1), lg + 1)
            off = jnp.where(rowp == colp, off, 0.0)
        for p in range(npair):
            for sub in range(half // sub_rows):
                blk = (2 * half * p + half) // sub_rows + sub
                piece = off[p * half + sub * sub_rows:p * half + (sub + 1) * sub_rows]
                a_rows[blk] = piece if a_rows[blk] is None else a_rows[blk] + piece
        half //= 2

    ones = jnp.ones((HG_DK, HG_DK), BF16)
    zero = jnp.zeros((SUBLANES, HG_DK), F32)
    groups = sub_rows // SUBLANES
    lane_sums = []
    for blk in range(nblk):
        r0 = blk * sub_rows
        bg = [bc[r0 + v * SUBLANES:r0 + (v + 1) * SUBLANES] for v in range(groups)]
        qg = [qh[r0 + v * SUBLANES:r0 + (v + 1) * SUBLANES] for v in range(groups)]
        pair = []
        for s in range(sub_rows):
            cks = ck_row(r0 + s)
            pair += [qg[v] * jnp.exp2(bg[v] - cks) if v >= s // SUBLANES else zero
                     for v in range(groups)]
        lane_sums.append(jnp.dot(jnp.concatenate(pair, axis=0).astype(BF16), ones,
                                 preferred_element_type=F32))
    return o, a_rows, lane_sums


def _hgrn2_output(scores, vh, ogh, gnh, bl, ck, st, cs):
    o, a_rows, lane_sums = scores
    sub_rows = _hg_sub_rows(cs)
    groups = sub_rows // SUBLANES
    lane = lax.broadcasted_iota(jnp.int32, (SUBLANES, HG_DK), 1)
    tloc = lax.broadcasted_iota(jnp.int32, (SUBLANES, HG_DK), 0)
    mask = [[(lane == s) & (tloc >= s - v * SUBLANES) if s > v * SUBLANES else (lane == s)
             for v in range(groups)] for s in range(sub_rows)]
    a_rows = list(a_rows)
    for blk, lane_sum in enumerate(lane_sums):
        a_g = [jnp.zeros((SUBLANES, HG_DK), F32)] * groups
        for s in range(sub_rows):
            for v in range(s // SUBLANES, groups):
                r = s * sub_rows + v * SUBLANES
                a_g[v] = jnp.where(mask[s][v], lane_sum[r:r + SUBLANES], a_g[v])
        a_d = jnp.concatenate(a_g, axis=0)
        if blk:
            a_d = pltpu.roll(a_d, blk * sub_rows, 1)
        a_d = a_d[:, :cs]
        a_rows[blk] = a_d if a_rows[blk] is None else a_rows[blk] + a_d
    a = jnp.concatenate(a_rows, axis=0)
    o = o + jnp.dot(a.astype(BF16), vh.astype(BF16), preferred_element_type=F32)
    kd = jnp.exp2(bl - ck).astype(BF16)
    st_new = st * jnp.exp2(bl) + jnp.dot(vh.T.astype(BF16), kd, preferred_element_type=F32)
    y = o * _rms_scale(o) * gnh * _sigmoid(ogh)
    return y, st_new


def _hgrn2_kernel(q_ref, f_ref, v_ref, og_ref, s0_ref, lbp_ref, gn_ref, yb_ref, s1_ref,
                  lb_ref, bc_ref, ck_ref, *st_refs, nb, T, cs, layer):
    it = pl.program_id(1)
    nt = pl.num_programs(1)
    row = lax.broadcasted_iota(jnp.int32, (cs, 3 * cs), 0)
    col = lax.broadcasted_iota(jnp.int32, (cs, 3 * cs), 1)
    tri3 = (row >= (col & (cs - 1))).astype(BF16)

    @pl.when(it == 0)
    def _():
        p = lbp_ref[...]
        e = jnp.exp(p - jnp.max(p, axis=0, keepdims=True))
        sm = e / jnp.sum(e, axis=0, keepdims=True)
        lb_ref[...] = jnp.sum(sm[:layer + 1], axis=0, keepdims=True)

        def init(b, c):
            for h in range(HG_HEADS):
                st_refs[h][b] = s0_ref[b, h].T
            return c
        lax.fori_loop(0, nb, init, 0)

    nch = T // cs
    par = 2 if nb % 2 == 0 else 1

    def chunk(idx, c):
        rows = pl.ds(pl.multiple_of((idx % nch) * cs, cs), cs)
        lb = lb_ref[...]
        units = []
        for k in range(par):
            b = (idx // nch) * par + k
            f = lb + (1.0 - lb) * _sigmoid(f_ref[b, rows, :])
            bc = jnp.dot(tri3, _split3(jnp.log(f) * LOG2E), preferred_element_type=F32)
            bc_ref[k] = bc
            ck_ref[k] = bc - jnp.log(1.0 - f) * LOG2E
            units += [(k, b, h) for h in range(HG_HEADS)]

        def scores(k, b, h):
            hl = slice(h * HG_DK, (h + 1) * HG_DK)
            return _hgrn2_scores(q_ref[b, rows, hl], bc_ref[k, :, hl], ck_ref[k, :, hl],
                                 lambda r: ck_ref[k, r:r + 1, hl], st_refs[h][b], cs)

        def output(k, b, h, sc):
            hl = slice(h * HG_DK, (h + 1) * HG_DK)
            y, st_new = _hgrn2_output(sc, v_ref[b, rows, hl], og_ref[b, rows, hl], gn_ref[...],
                                      bc_ref[k, cs - 1:cs, hl], ck_ref[k, :, hl], st_refs[h][b], cs)
            yb_ref[b, rows, hl] = y.astype(BF16)
            st_refs[h][b] = st_new

        pending = {}
        for n in range(len(units) + HG_SKEW):
            if n < len(units):
                pending[n] = scores(*units[n])
            if n >= HG_SKEW:
                output(*units[n - HG_SKEW], pending.pop(n - HG_SKEW))
        return c
    lax.fori_loop(0, (nb // par) * nch, chunk, 0)

    @pl.when(it == nt - 1)
    def _():
        def fin(b, c):
            for h in range(HG_HEADS):
                s1_ref[b, h] = st_refs[h][b].T
            return c
        lax.fori_loop(0, nb, fin, 0)


def _hgrn2(z3, s0, hg_lb, gn, layer, nb, T, casts=()):
    B, L, _ = z3.shape
    C = HG_HEADS * HG_DK
    cs = min(HG_CHUNK, L)
    zcol = lambda col: pl.BlockSpec((nb, T, C), lambda b, t: (b, t, col))
    st_spec = lambda: pl.BlockSpec((nb, HG_HEADS, HG_DK, HG_DV), lambda b, t: (b, 0, 0, 0))
    nl = hg_lb.shape[0]
    nt = L // T
    c_in, c_out, c_shape = _cast_specs(casts, (B // nb) * nt, lambda b, t: b * nt + t)
    return pl.pallas_call(
        functools.partial(_with_casts(_hgrn2_kernel, 7, 2, len(casts)), nb=nb, T=T, cs=cs, layer=layer),
        grid=(B // nb, nt),
        in_specs=[
            zcol(COL_HQ), zcol(COL_HF), zcol(COL_HI), zcol(COL_HO),
            st_spec(),
            pl.BlockSpec((nl, C), lambda b, t: (0, 0)),
            pl.BlockSpec((1, HG_DV), lambda b, t: (0, 0)),
        ] + c_in,
        out_specs=[pl.BlockSpec((nb, T, C), lambda b, t: (b, t, 0)), st_spec()] + c_out,
        out_shape=[
            jax.ShapeDtypeStruct((B, L, C), BF16),
            jax.ShapeDtypeStruct((B, HG_HEADS, HG_DK, HG_DV), F32),
        ] + c_shape,
        scratch_shapes=[
            pltpu.VMEM((1, C), F32),
            pltpu.VMEM((2, cs, C), F32),
            pltpu.VMEM((2, cs, C), F32),
        ] + [pltpu.VMEM((nb, HG_DV, HG_DK), F32) for _ in range(HG_HEADS)],
        compiler_params=_params(("arbitrary", "arbitrary")),
        name="hgrn2",
    )(z3, z3, z3, z3, s0, hg_lb, gn.reshape(1, HG_DV), *casts)


def _softmax_rows(s):
    e = jnp.exp(s - jnp.max(s, axis=-1, keepdims=True))
    return e / jnp.sum(e, axis=-1, keepdims=True)


def _xattn_kernel(q_ref, k_ref, v_ref, o_ref):
    hls = [slice(h * XA_HD, (h + 1) * XA_HD) for h in range(XA_HEADS)]
    s = [_dot_nt(q_ref[0, :, hl].astype(BF16), k_ref[0, :, hl].astype(BF16)) * (XA_HD ** -0.5)
         for hl in hls]
    p = [_softmax_rows(sh).astype(BF16) for sh in s]
    for hl, ph in zip(hls, p):
        o = jnp.dot(ph, v_ref[0, :, hl].astype(BF16), preferred_element_type=F32)
        o_ref[0, :, hl] = o.astype(BF16)


def _xattn(z3, mk, mv, T):
    B, L, _ = z3.shape
    C = XA_HEADS * XA_HD
    mem = lambda: pl.BlockSpec((1, MEM_LEN, C), lambda b, t: (b, 0, 0))
    return pl.pallas_call(
        _xattn_kernel,
        grid=(B, L // T),
        in_specs=[pl.BlockSpec((1, T, C), lambda b, t: (b, t, COL_XQ)), mem(), mem()],
        out_specs=pl.BlockSpec((1, T, C), lambda b, t: (b, t, 0)),
        out_shape=jax.ShapeDtypeStruct((B, L, C), BF16),
        compiler_params=_params(("arbitrary", "arbitrary")),
        name="xattn",
    )(z3, mk, mv)


def _xattn_cached_kernel(q_ref, k_ref, v_ref, o_ref):
    kt = jnp.swapaxes(k_ref[...], 0, 1)
    vt = jnp.swapaxes(v_ref[...], 0, 1)
    hls = [slice(h * XA_HD, (h + 1) * XA_HD) for h in range(XA_HEADS)]
    s = [_dot_nt(q_ref[0, :, hl].astype(BF16), kt[h].astype(BF16)) * (XA_HD ** -0.5)
         for h, hl in enumerate(hls)]
    p = [_softmax_rows(sh).astype(BF16) for sh in s]
    for h, hl in enumerate(hls):
        o = jnp.dot(p[h], vt[h].astype(BF16), preferred_element_type=F32)
        o_ref[0, :, hl] = o.astype(BF16)


def _xattn_cached(z3, cache_k, cache_v, layer):
    B, L, _ = z3.shape
    C = XA_HEADS * XA_HD
    mem = lambda: pl.BlockSpec((None, None, MEM_LEN, XA_HEADS, XA_HD), lambda b: (layer, b, 0, 0, 0))
    return pl.pallas_call(
        _xattn_cached_kernel,
        grid=(B,),
        in_specs=[pl.BlockSpec((1, L, C), lambda b: (b, 0, COL_XQ)), mem(), mem()],
        out_specs=pl.BlockSpec((1, L, C), lambda b: (b, 0, 0)),
        out_shape=jax.ShapeDtypeStruct((B, L, C), BF16),
        compiler_params=_params(("arbitrary",)),
        name="xattn_cached",
    )(z3, cache_k, cache_v)


def _merge_kernel(x_ref, ya_ref, yb_ref, yc_ref, g0_ref, g1_ref, g2_ref, bg_ref, wb_ref, wo_ref,
                  pg_ref, o_ref):
    m = jnp.zeros(o_ref.shape, F32)
    for nb, (y_ref, g_ref) in enumerate(((ya_ref, g0_ref), (yb_ref, g1_ref), (yc_ref, g2_ref))):
        gate = _sigmoid(g_ref[...] + bg_ref[nb:nb + 1, :])
        m = m + gate * jnp.dot(y_ref[...], wb_ref[nb], preferred_element_type=F32)
    y = jnp.dot(m.astype(BF16), wo_ref[...], preferred_element_type=F32)
    o_ref[...] = x_ref[...] + y * _rms_scale(y) * pg_ref[...]


def _merge(x, ya, yb, yc, z, b_gate, wb, wo, pg, tm):
    n, d = x.shape
    c = BRANCH_WIDTH
    once = pl.Buffered(1)
    branch = lambda: pl.BlockSpec((tm, c), lambda i: (i, 0))
    gate = lambda nb: pl.BlockSpec((tm, d), lambda i: (i, GATE_COL0 + nb))
    return pl.pallas_call(
        _merge_kernel,
        grid=(n // tm,),
        in_specs=[
            pl.BlockSpec((tm, d), lambda i: (i, 0)),
            branch(), branch(), branch(),
            gate(0), gate(1), gate(2),
            pl.BlockSpec((N_BRANCH, d), lambda i: (0, 0)),
            pl.BlockSpec((N_BRANCH, c, d), lambda i: (0, 0, 0), pipeline_mode=once),
            pl.BlockSpec((d, d), lambda i: (0, 0), pipeline_mode=once),
            pl.BlockSpec((1, d), lambda i: (0, 0)),
        ],
        out_specs=pl.BlockSpec((tm, d), lambda i: (i, 0)),
        out_shape=jax.ShapeDtypeStruct((n, d), F32),
        compiler_params=_params(("arbitrary",)),
        name="merge",
    )(x, ya, yb, yc, z, z, z, b_gate, wb, wo, pg.reshape(1, d))


def _ffn_kernel(x_ref, g_ref, wu_ref, wv_ref, cw_ref, cb_ref, fc0_ref, wd_ref, pg_ref,
                o_ref, fc1_ref, xn_ref, acc_ref, hal_ref, car_ref, *, nb, T, tf):
    it = pl.program_id(1)
    j = pl.program_id(2)
    nj = pl.num_programs(2)
    hist = FFN_CONV_W - 1

    @pl.when(j == 0)
    def _():
        x = x_ref[...].reshape(nb * T, D_MODEL)
        xn_ref[...] = (x * _rms_scale(x) * g_ref[...]).astype(BF16)
        acc_ref[...] = jnp.zeros_like(acc_ref)

    @pl.when(it == 0)
    def _():
        car_ref[j] = fc0_ref[...]

    xn = xn_ref[...]
    u = jnp.dot(xn, wu_ref[...], preferred_element_type=F32)
    v = jnp.dot(xn, wv_ref[...], preferred_element_type=F32)
    hal_ref[:, HALO - hist:HALO, :] = car_ref[j]
    hal_ref[:, HALO:HALO + T, :] = u.reshape(nb, T, tf)
    cw = cw_ref[...]
    uc = cb_ref[...][None]
    for jj in range(FFN_CONV_W):
        uc = uc + hal_ref[:, HALO - hist + jj:HALO - hist + jj + T, :] * cw[jj:jj + 1][None]
    tail = hal_ref[:, HALO + T - hist:HALO + T, :]
    car_ref[j] = tail
    act = (jax.nn.gelu(uc).reshape(nb * T, tf) * v).astype(BF16)
    acc_ref[...] += jnp.dot(act, wd_ref[...], preferred_element_type=F32)

    @pl.when(j == nj - 1)
    def _():
        pg = pg_ref[...]
        for b in range(nb):
            for r in range(0, T, FFN_OUT_ROWS):
                y = acc_ref[b * T + r:b * T + r + FFN_OUT_ROWS, :]
                o_ref[b, r:r + FFN_OUT_ROWS, :] = x_ref[b, r:r + FFN_OUT_ROWS, :] + y * _rms_scale(y) * pg

    @pl.when((j == nj - 1) & (it == pl.num_programs(1) - 1))
    def _():
        for jj in range(FFN_DIM // tf):
            fc1_ref[:, :, jj * tf:(jj + 1) * tf] = car_ref[jj]


def _ffn(x3, g, w_up, cw, cb, fc0, w_down, pg, nb, T, tf):
    B, L, d = x3.shape
    F = FFN_DIM
    nj = F // tf
    hist = FFN_CONV_W - 1
    rows = lambda: pl.BlockSpec((nb, T, d), lambda b, t, j: (b, t, 0))
    return pl.pallas_call(
        functools.partial(_ffn_kernel, nb=nb, T=T, tf=tf),
        grid=(B // nb, L // T, nj),
        in_specs=[
            rows(),
            pl.BlockSpec((1, d), lambda b, t, j: (0, 0)),
            pl.BlockSpec((d, tf), lambda b, t, j: (0, j)),
            pl.BlockSpec((d, tf), lambda b, t, j: (0, nj + j)),
            pl.BlockSpec((FFN_CONV_W, tf), lambda b, t, j: (0, j)),
            pl.BlockSpec((1, tf), lambda b, t, j: (0, j)),
            pl.BlockSpec((nb, hist, tf), lambda b, t, j: (b, 0, j)),
            pl.BlockSpec((tf, d), lambda b, t, j: (j, 0)),
            pl.BlockSpec((1, d), lambda b, t, j: (0, 0)),
        ],
        out_specs=[rows(), pl.BlockSpec((nb, hist, F), lambda b, t, j: (b, 0, 0))],
        out_shape=[jax.ShapeDtypeStruct((B, L, d), F32), jax.ShapeDtypeStruct((B, hist, F), F32)],
        scratch_shapes=[
            pltpu.VMEM((nb * T, d), BF16),
            pltpu.VMEM((nb * T, d), F32),
            pltpu.VMEM((nb, HALO + T, tf), F32),
            pltpu.VMEM((nj, nb, hist, tf), F32),
        ],
        compiler_params=_params(("arbitrary", "arbitrary", "arbitrary")),
        name="ffn",
    )(x3, g.reshape(1, d), w_up, w_up, cw, cb.reshape(1, F), fc0, w_down, pg.reshape(1, d))


def _block_diag(w):
    nblk, bi, bj = w.shape
    per = MXU_COLS // bi
    eye = jnp.eye(per, dtype=w.dtype)
    grouped = jnp.einsum("ghij,hk->ghikj", w.reshape(nblk // per, per, bi, bj), eye)
    return grouped.reshape(nblk // per, per * bi, per * bj)


def _tiles(B, L):
    T = min(L, 256)
    return dict(
        proj_tm=min(B * L, 1024), proj_tn=2048,
        rnn_nb=B if B * T <= 512 else 512 // T, rnn_T=T,
        hg_nb=min(B, 4), hg_T=T,
        xa_T=min(L, 1024),
        merge_tm=256,
        ffn_nb=B if B * T <= 512 else 512 // T, ffn_T=T, ffn_tf=512,
    )


def _trunk_layer(x, mk, mv, h0, rc0, s0, fc0, w, layer):
    B, L, d = x.shape
    n = B * L
    t = _tiles(B, L)
    x2 = x.reshape(n, d)
    made = {}
    tm, tn = t["proj_tm"], t["proj_tn"]
    if w["w_in"].dtype == F32:
        z, made["w_in"] = _norm_matmul(x2, w["pre_mix_norm"], w["w_in"], tm, tn // 2, "in_proj",
                                       tiles=(0, 1), emit_w=True)
        if n > tm:
            z = _norm_matmul(x2, w["pre_mix_norm"], made["w_in"], tm, tn, "in_proj",
                             tiles=(1, n // tm), into=z)
    else:
        z = _norm_matmul(x2, w["pre_mix_norm"], w["w_in"], tm, tn, "in_proj")
    z3 = z.reshape(B, L, IN_COLS)
    c = BRANCH_WIDTH
    fresh = w["w_out"].dtype == F32
    ya, h1, rc1, *cast = _rglru(
        z3, rc0, h0, w["rnn_conv_w"], w["rnn_conv_b"], w["lru_wa"], w["lru_ba"], w["lru_wx"],
        w["lru_bx"], w["lru_lambda"], t["rnn_nb"], t["rnn_T"],
        casts=(w["w_branch"].reshape(N_BRANCH * c, d), w["w_out"]) if fresh else ())
    if fresh:
        made["w_branch"], made["w_out"] = cast[0].reshape(N_BRANCH, c, d), cast[1]
    yb, s1, *cast = _hgrn2(z3, s0, w["hg_lb"], w["hg_norm"], layer, t["hg_nb"], t["hg_T"],
                           casts=(w["w_ffn_up"], w["w_ffn_down"]) if fresh else ())
    if fresh:
        made["w_ffn_up"], made["w_ffn_down"] = cast
    w = {**w, **made}
    yc = _xattn(z3, mk, mv, t["xa_T"]) if mv is not None else _xattn_cached(z3, *mk, layer)
    x1 = _merge(x2, ya.reshape(n, c), yb.reshape(n, c), yc.reshape(n, c), z, w["b_gate"],
                w["w_branch"], w["w_out"], w["post_mix_norm"], t["merge_tm"]).reshape(B, L, d)
    xo, fc1 = _ffn(x1, w["pre_ffn_norm"], w["w_ffn_up"], w["ffn_conv_w"], w["ffn_conv_b"], fc0,
                   w["w_ffn_down"], w["post_ffn_norm"], t["ffn_nb"], t["ffn_T"], t["ffn_tf"])
    return xo, h1, rc1, s1, fc1, made


def kernel(x_prompt, x_sample, cache_mem_k, cache_mem_v, state_rnn_h, state_rnn_conv, state_hg,
           state_ffn_conv, mem_prompt, pre_mix_norm, w_in, rnn_conv_w, rnn_conv_b, lru_wa, lru_ba,
           lru_wx, lru_bx, lru_lambda, hg_lb, hg_norm, mem_norm, w_mem_kv, w_branch, b_gate, w_out,
           post_mix_norm, pre_ffn_norm, w_ffn_up, ffn_conv_w, ffn_conv_b, w_ffn_down, post_ffn_norm):
    depth = w_in.shape[0]
    Bp = x_prompt.shape[0]
    Bs = x_sample.shape[0]
    xa_w = XA_HEADS * XA_HD
    yp, ys = x_prompt, x_sample
    outs = [[] for _ in range(10)]
    for l in range(depth):
        w = {
            "pre_mix_norm": pre_mix_norm[l], "w_in": w_in[l],
            "rnn_conv_w": rnn_conv_w[l], "rnn_conv_b": rnn_conv_b[l],
            "lru_wa": _block_diag(lru_wa[l]).astype(BF16), "lru_ba": lru_ba[l],
            "lru_wx": _block_diag(lru_wx[l]).astype(BF16), "lru_bx": lru_bx[l],
            "lru_lambda": lru_lambda[l], "hg_lb": hg_lb, "hg_norm": hg_norm[l],
            "w_branch": w_branch[l], "b_gate": b_gate[l],
            "w_out": w_out[l], "post_mix_norm": post_mix_norm[l],
            "pre_ffn_norm": pre_ffn_norm[l], "w_ffn_up": w_ffn_up[l],
            "ffn_conv_w": ffn_conv_w[l], "ffn_conv_b": ffn_conv_b[l],
            "w_ffn_down": w_ffn_down[l], "post_ffn_norm": post_ffn_norm[l],
        }
        mem2 = mem_prompt.reshape(Bp * MEM_LEN, D_MODEL)
        kv = _norm_matmul(mem2, mem_norm[l], w_mem_kv[l], Bp * MEM_LEN, 1024, "mem_kv")
        kv = kv.reshape(Bp, MEM_LEN, 2 * xa_w)
        mk_p, mv_p = kv[..., :xa_w], kv[..., xa_w:]
        yp, h_p, rc_p, s_p, fc_p, made = _trunk_layer(
            yp, mk_p, mv_p, jnp.zeros((Bp, RNN_WIDTH), F32),
            jnp.zeros((Bp, RNN_CONV_W - 1, RNN_WIDTH), F32),
            jnp.zeros((Bp, HG_HEADS, HG_DK, HG_DV), F32),
            jnp.zeros((Bp, FFN_CONV_W - 1, FFN_DIM), F32), w, l)
        w = {**w, **made}
        ys, h_s, rc_s, s_s, fc_s, _ = _trunk_layer(
            ys, (cache_mem_k, cache_mem_v), None,
            state_rnn_h[l], state_rnn_conv[l], state_hg[l], state_ffn_conv[l], w, l)
        layer_out = (mk_p.reshape(Bp, MEM_LEN, XA_HEADS, XA_HD), mv_p.reshape(Bp, MEM_LEN, XA_HEADS, XA_HD),
                     h_p, rc_p, s_p, fc_p, h_s, rc_s, s_s, fc_s)
        for acc, val in zip(outs, layer_out):
            acc.append(val)
    return (yp, ys) + tuple(jnp.stack(o) for o in outs)
```

```python
import functools

import jax
import jax.numpy as jnp
from jax import lax
from jax.experimental import pallas as pl
from jax.experimental.pallas import tpu as pltpu

F32 = jnp.float32
BF16 = jnp.bfloat16

D_MODEL = 2048
RNN_WIDTH = 1024
RNN_BLOCKS = 16
RNN_CONV_W = 4
LRU_C = 8.0
HG_HEADS = 8
HG_DK = 128
HG_DV = 128
HG_CHUNK = 128
HG_SUB_MAX = 16
SUBLANES = 8
MXU_COLS = 256
HG_SKEW = 8
MEM_LEN = 256
XA_HEADS = 4
XA_HD = 256
BRANCH_WIDTH = 1024
N_BRANCH = 3
FFN_DIM = 5632
FFN_CONV_W = 3
FFN_OUT_ROWS = 8
EPS = 1e-6
LOG2E = 1.4426950408889634
IN_COLS = 6 * BRANCH_WIDTH + N_BRANCH * D_MODEL

COL_RNN, COL_HQ, COL_HF, COL_HI, COL_HO, COL_XQ = range(6)
GATE_COL0 = 3

HALO = 8
VMEM_LIMIT = 56 * 1024 * 1024


def _params(sem):
    return pltpu.CompilerParams(dimension_semantics=sem, vmem_limit_bytes=VMEM_LIMIT)


def _rms_scale(x):
    return lax.rsqrt(jnp.mean(x * x, axis=-1, keepdims=True) + EPS)


_sigmoid = jax.nn.sigmoid


def _sigmoid_tanh(x):
    return 0.5 * jnp.tanh(0.5 * x) + 0.5


def _with_casts(kernel, n_in, n_out, n_cast):
    def wrapped(*refs, **kw):
        ins, refs = refs[:n_in], refs[n_in:]
        cast_in, refs = refs[:n_cast], refs[n_cast:]
        outs, refs = refs[:n_out], refs[n_out:]
        cast_out, scratch = refs[:n_cast], refs[n_cast:]
        for src, dst in zip(cast_in, cast_out):
            dst[...] = src[...].astype(BF16)
        kernel(*ins, *outs, *scratch, **kw)
    return wrapped


def _cast_specs(casts, nsteps, step):
    in_specs, out_specs, out_shape = [], [], []
    for w in casts:
        rows, cols = w.shape[0] // nsteps, w.shape[1]
        assert rows * nsteps == w.shape[0] and rows % (2 * SUBLANES) == 0
        for specs in (in_specs, out_specs):
            specs.append(pl.BlockSpec((rows, cols), lambda *g, step=step: (step(*g), 0)))
        out_shape.append(jax.ShapeDtypeStruct(w.shape, BF16))
    return in_specs, out_specs, out_shape


def _norm_matmul_kernel(x_ref, g_ref, w_ref, *rest, emit_w, aliased):
    rest = rest[1:] if aliased else rest
    o_ref, xn_ref = rest[0], rest[-1]

    @pl.when(pl.program_id(1) == 0)
    def _():
        x = x_ref[...]
        xn_ref[...] = (x * _rms_scale(x) * g_ref[...]).astype(BF16)

    w = w_ref[...].astype(BF16)
    if emit_w:
        rest[1][...] = w
    o_ref[...] = jnp.dot(xn_ref[...], w, preferred_element_type=F32)


def _norm_matmul(x, g, w, tm, tn, name, tiles=None, emit_w=False, into=None):
    n, d = x.shape
    c = w.shape[1]
    t0, t1 = tiles if tiles is not None else (0, n // tm)
    assert not emit_w or t1 - t0 == 1
    in_specs = [
        pl.BlockSpec((tm, d), lambda i, j: (i + t0, 0)),
        pl.BlockSpec((1, d), lambda i, j: (0, 0)),
        pl.BlockSpec((d, tn), lambda i, j: (0, j)),
    ]
    args = [x, g.reshape(1, d), w]
    out_specs = [pl.BlockSpec((tm, tn), lambda i, j: (i + t0, j))]
    out_shape = [jax.ShapeDtypeStruct((n, c), F32)]
    if into is not None:
        in_specs.append(pl.BlockSpec(memory_space=pl.ANY))
        args.append(into)
    if emit_w:
        out_specs.append(pl.BlockSpec((d, tn), lambda i, j: (0, j)))
        out_shape.append(jax.ShapeDtypeStruct((d, c), BF16))
    res = pl.pallas_call(
        functools.partial(_norm_matmul_kernel, emit_w=emit_w, aliased=into is not None),
        grid=(t1 - t0, c // tn),
        in_specs=in_specs,
        out_specs=out_specs,
        out_shape=out_shape,
        scratch_shapes=[pltpu.VMEM((tm, d), BF16)],
        input_output_aliases={3: 0} if into is not None else {},
        compiler_params=_params(("arbitrary", "arbitrary")),
        name=name,
    )(*args)
    return res if emit_w else res[0]


def _rglru_kernel(z_ref, rc0_ref, h0_ref, cw_ref, cb_ref, wa_ref, ba_ref, wx_ref, bx_ref, lam_ref,
                  ya_ref, h1_ref, rc1_ref, xp_ref, a_ref, b_ref, hs_ref, hc_ref, *, nb, T):
    it = pl.program_id(1)
    nt = pl.num_programs(1)
    hist = RNN_CONV_W - 1

    @pl.when(it == 0)
    def _():
        xp_ref[:, HALO - hist:HALO, :] = rc0_ref[...]
        hc_ref[...] = h0_ref[...]

    x = z_ref[...]
    cw = cw_ref[...]
    xr = cb_ref[...][None]
    for j in range(RNN_CONV_W):
        xs = x if j == hist else pltpu.roll(x, hist - j, 1)
        xr = xr + xs * cw[j:j + 1][None]
    xp_ref[:, HALO:2 * HALO, :] = x[:, :HALO]
    head = cb_ref[...][None]
    for j in range(RNN_CONV_W):
        head = head + xp_ref[:, HALO - hist + j:2 * HALO - hist + j, :] * cw[j:j + 1][None]
    xr = jnp.concatenate([head, xr[:, HALO:]], axis=1)
    xp_ref[:, HALO - hist:HALO, :] = x[:, T - hist:]

    xr2 = xr.reshape(nb * T, RNN_WIDTH)
    xb = xr2.astype(BF16)
    def gate(w_ref, bias_ref):
        cols = [jnp.dot(xb[:, g * MXU_COLS:(g + 1) * MXU_COLS], w_ref[g], preferred_element_type=F32)
                for g in range(RNN_WIDTH // MXU_COLS)]
        return _sigmoid_tanh(jnp.concatenate(cols, axis=1) + bias_ref[...])

    r = gate(wa_ref, ba_ref)
    ig = gate(wx_ref, bx_ref)
    nl = -lam_ref[...]
    softplus = jnp.maximum(nl, 0.0) + jnp.log1p(jnp.exp(-jnp.abs(nl)))
    log_a = -LRU_C * r * softplus
    a = jnp.exp(log_a)
    a_ref[...] = a.reshape(nb, T, RNN_WIDTH)
    one_minus_a2 = -jnp.tanh(log_a) * (a * a + 1.0)
    b_ref[...] = (jnp.sqrt(one_minus_a2) * (ig * xr2)).reshape(nb, T, RNN_WIDTH)

    hs = [hc_ref[bi:bi + 1, :] for bi in range(nb)]
    for t in range(T):
        for bi in range(nb):
            hs[bi] = a_ref[bi, t:t + 1, :] * hs[bi] + b_ref[bi, t:t + 1, :]
            hs_ref[bi, t:t + 1, :] = hs[bi]
    for bi in range(nb):
        hc_ref[bi:bi + 1, :] = hs[bi]
    ya_ref[...] = hs_ref[...].astype(BF16)

    @pl.when(it == nt - 1)
    def _():
        h1_ref[...] = hc_ref[...]
        rc1_ref[...] = xp_ref[:, HALO - hist:HALO, :]


def _rglru(z3, rc0, h0, cw, cb, wa, ba, wx, bx, lam, nb, T, casts=()):
    B, L, _ = z3.shape
    C = RNN_WIDTH
    hist = RNN_CONV_W - 1
    vec = lambda: pl.BlockSpec((1, C), lambda b, t: (0, 0))
    mat = lambda: pl.BlockSpec((C // MXU_COLS, MXU_COLS, MXU_COLS), lambda b, t: (0, 0, 0))
    nt = L // T
    c_in, c_out, c_shape = _cast_specs(casts, (B // nb) * nt, lambda b, t: b * nt + t)
    return pl.pallas_call(
        functools.partial(_with_casts(_rglru_kernel, 10, 3, len(casts)), nb=nb, T=T),
        grid=(B // nb, nt),
        in_specs=[
            pl.BlockSpec((nb, T, C), lambda b, t: (b, t, COL_RNN)),
            pl.BlockSpec((nb, hist, C), lambda b, t: (b, 0, 0)),
            pl.BlockSpec((nb, C), lambda b, t: (b, 0)),
            pl.BlockSpec((RNN_CONV_W, C), lambda b, t: (0, 0)),
            vec(), mat(), vec(), mat(), vec(), vec(),
        ] + c_in,
        out_specs=[
            pl.BlockSpec((nb, T, C), lambda b, t: (b, t, 0)),
            pl.BlockSpec((nb, C), lambda b, t: (b, 0)),
            pl.BlockSpec((nb, hist, C), lambda b, t: (b, 0, 0)),
        ] + c_out,
        out_shape=[
            jax.ShapeDtypeStruct((B, L, C), BF16),
            jax.ShapeDtypeStruct((B, C), F32),
            jax.ShapeDtypeStruct((B, hist, C), F32),
        ] + c_shape,
        scratch_shapes=[
            pltpu.VMEM((nb, HALO + T, C), F32),
            pltpu.VMEM((nb, T, C), F32),
            pltpu.VMEM((nb, T, C), F32),
            pltpu.VMEM((nb, T, C), F32),
            pltpu.VMEM((nb, C), F32),
        ],
        compiler_params=_params(("arbitrary", "arbitrary")),
        name="rglru",
    )(z3, rc0, h0, cw, cb.reshape(1, C), wa, ba.reshape(1, C), wx, bx.reshape(1, C), lam.reshape(1, C),
      *casts)


def _dot_nt(a, b):
    return lax.dot_general(a, b, (((1,), (1,)), ((), ())), preferred_element_type=F32)


def _split3(x):
    hi = x.astype(BF16)
    r = x - hi.astype(F32)
    mid = r.astype(BF16)
    lo = (r - mid.astype(F32)).astype(BF16)
    return jnp.concatenate([hi, mid, lo], axis=0)


def _hg_sub_rows(cs):
    return HG_SUB_MAX if cs >= 4 * HG_SUB_MAX else SUBLANES


def _hgrn2_scores(qh, bc, ck, ck_row, st, cs):
    o = _dot_nt((qh * jnp.exp2(bc)).astype(BF16), st.astype(BF16))

    sub_rows = _hg_sub_rows(cs)
    nblk = cs // sub_rows
    a_rows = [None] * nblk
    half = cs // 2
    while half >= sub_rows:
        npair = cs // (2 * half)
        q_parts, k_parts = [], []
        for p in range(npair):
            lo, mid, hi = 2 * half * p, 2 * half * p + half, 2 * half * (p + 1)
            br = bc[mid - 1:mid]
            q_parts.append(qh[mid:hi] * jnp.exp2(bc[mid:hi] - br))
            k_parts += [jnp.exp2(br - ck[lo:mid]), jnp.zeros((half, HG_DK), F32)]
        off = _dot_nt(jnp.concatenate(q_parts, axis=0).astype(BF16),
                      jnp.concatenate(k_parts, axis=0).astype(BF16))
        if npair > 1:
            lg = half.bit_length() - 1
            rowp = lax.shift_right_logical(lax.broadcasted_iota(jnp.int32, off.shape, 0), lg)
            colp = lax.shift_right_logical(lax.broadcasted_iota(jnp.int32, off.shape, 1), lg + 1)
            off = jnp.where(rowp == colp, off, 0.0)
        for p in range(npair):
            for sub in range(half // sub_rows):
                blk = (2 * half * p + half) // sub_rows + sub
                piece = off[p * half + sub * sub_rows:p * half + (sub + 1) * sub_rows]
                a_rows[blk] = piece if a_rows[blk] is None else a_rows[blk] + piece
        half //= 2

    ones = jnp.ones((HG_DK, HG_DK), BF16)
    zero = jnp.zeros((SUBLANES, HG_DK), F32)
    groups = sub_rows // SUBLANES
    lane_sums = []
    for blk in range(nblk):
        r0 = blk * sub_rows
        bg = [bc[r0 + v * SUBLANES:r0 + (v + 1) * SUBLANES] for v in range(groups)]
        qg = [qh[r0 + v * SUBLANES:r0 + (v + 1) * SUBLANES] for v in range(groups)]
        pair = []
        for s in range(sub_rows):
            cks = ck_row(r0 + s)
            pair += [qg[v] * jnp.exp2(bg[v] - cks) if v >= s // SUBLANES else zero
                     for v in range(groups)]
        lane_sums.append(jnp.dot(jnp.concatenate(pair, axis=0).astype(BF16), ones,
                                 preferred_element_type=F32))
    return o, a_rows, lane_sums


def _hgrn2_output(scores, vh, ogh, gnh, bl, ck, st, cs):
    o, a_rows, lane_sums = scores
    sub_rows = _hg_sub_rows(cs)
    groups = sub_rows // SUBLANES
    lane = lax.broadcasted_iota(jnp.int32, (SUBLANES, HG_DK), 1)
    tloc = lax.broadcasted_iota(jnp.int32, (SUBLANES, HG_DK), 0)
    mask = [[(lane == s) & (tloc >= s - v * SUBLANES) if s > v * SUBLANES else (lane == s)
             for v in range(groups)] for s in range(sub_rows)]
    a_rows = list(a_rows)
    for blk, lane_sum in enumerate(lane_sums):
        a_g = [jnp.zeros((SUBLANES, HG_DK), F32)] * groups
        for s in range(sub_rows):
            for v in range(s // SUBLANES, groups):
                r = s * sub_rows + v * SUBLANES
                a_g[v] = jnp.where(mask[s][v], lane_sum[r:r + SUBLANES], a_g[v])
        a_d = jnp.concatenate(a_g, axis=0)
        if blk:
            a_d = pltpu.roll(a_d, blk * sub_rows, 1)
        a_d = a_d[:, :cs]
        a_rows[blk] = a_d if a_rows[blk] is None else a_rows[blk] + a_d
    a = jnp.concatenate(a_rows, axis=0)
    o = o + jnp.dot(a.astype(BF16), vh.astype(BF16), preferred_element_type=F32)
    kd = jnp.exp2(bl - ck).astype(BF16)
    st_new = st * jnp.exp2(bl) + jnp.dot(vh.T.astype(BF16), kd, preferred_element_type=F32)
    y = o * _rms_scale(o) * gnh * _sigmoid(ogh)
    return y, st_new


def _hgrn2_kernel(q_ref, f_ref, v_ref, og_ref, s0_ref, lbp_ref, gn_ref, yb_ref, s1_ref,
                  lb_ref, bc_ref, ck_ref, *st_refs, nb, T, cs, layer):
    it = pl.program_id(1)
    nt = pl.num_programs(1)
    row = lax.broadcasted_iota(jnp.int32, (cs, 3 * cs), 0)
    col = lax.broadcasted_iota(jnp.int32, (cs, 3 * cs), 1)
    tri3 = (row >= (col & (cs - 1))).astype(BF16)

    @pl.when(it == 0)
    def _():
        p = lbp_ref[...]
        e = jnp.exp(p - jnp.max(p, axis=0, keepdims=True))
        sm = e / jnp.sum(e, axis=0, keepdims=True)
        lb_ref[...] = jnp.sum(sm[:layer + 1], axis=0, keepdims=True)

        def init(b, c):
            for h in range(HG_HEADS):
                st_refs[h][b] = s0_ref[b, h].T
            return c
        lax.fori_loop(0, nb, init, 0)

    nch = T // cs
    par = 2 if nb % 2 == 0 else 1

    def chunk(idx, c):
        rows = pl.ds(pl.multiple_of((idx % nch) * cs, cs), cs)
        lb = lb_ref[...]
        units = []
        for k in range(par):
            b = (idx // nch) * par + k
            f = lb + (1.0 - lb) * _sigmoid(f_ref[b, rows, :])
            bc = jnp.dot(tri3, _split3(jnp.log(f) * LOG2E), preferred_element_type=F32)
            bc_ref[k] = bc
            ck_ref[k] = bc - jnp.log(1.0 - f) * LOG2E
            units += [(k, b, h) for h in range(HG_HEADS)]

        def scores(k, b, h):
            hl = slice(h * HG_DK, (h + 1) * HG_DK)
            return _hgrn2_scores(q_ref[b, rows, hl], bc_ref[k, :, hl], ck_ref[k, :, hl],
                                 lambda r: ck_ref[k, r:r + 1, hl], st_refs[h][b], cs)

        def output(k, b, h, sc):
            hl = slice(h * HG_DK, (h + 1) * HG_DK)
            y, st_new = _hgrn2_output(sc, v_ref[b, rows, hl], og_ref[b, rows, hl], gn_ref[...],
                                      bc_ref[k, cs - 1:cs, hl], ck_ref[k, :, hl], st_refs[h][b], cs)
            yb_ref[b, rows, hl] = y.astype(BF16)
            st_refs[h][b] = st_new

        pending = {}
        for n in range(len(units) + HG_SKEW):
            if n < len(units):
                pending[n] = scores(*units[n])
            if n >= HG_SKEW:
                output(*units[n - HG_SKEW], pending.pop(n - HG_SKEW))
        return c
    lax.fori_loop(0, (nb // par) * nch, chunk, 0)

    @pl.when(it == nt - 1)
    def _():
        def fin(b, c):
            for h in range(HG_HEADS):
                s1_ref[b, h] = st_refs[h][b].T
            return c
        lax.fori_loop(0, nb, fin, 0)


def _hgrn2(z3, s0, hg_lb, gn, layer, nb, T, casts=()):
    B, L, _ = z3.shape
    C = HG_HEADS * HG_DK
    cs = min(HG_CHUNK, L)
    zcol = lambda col: pl.BlockSpec((nb, T, C), lambda b, t: (b, t, col))
    st_spec = lambda: pl.BlockSpec((nb, HG_HEADS, HG_DK, HG_DV), lambda b, t: (b, 0, 0, 0))
    nl = hg_lb.shape[0]
    nt = L // T
    c_in, c_out, c_shape = _cast_specs(casts, (B // nb) * nt, lambda b, t: b * nt + t)
    return pl.pallas_call(
        functools.partial(_with_casts(_hgrn2_kernel, 7, 2, len(casts)), nb=nb, T=T, cs=cs, layer=layer),
        grid=(B // nb, nt),
        in_specs=[
            zcol(COL_HQ), zcol(COL_HF), zcol(COL_HI), zcol(COL_HO),
            st_spec(),
            pl.BlockSpec((nl, C), lambda b, t: (0, 0)),
            pl.BlockSpec((1, HG_DV), lambda b, t: (0, 0)),
        ] + c_in,
        out_specs=[pl.BlockSpec((nb, T, C), lambda b, t: (b, t, 0)), st_spec()] + c_out,
        out_shape=[
            jax.ShapeDtypeStruct((B, L, C), BF16),
            jax.ShapeDtypeStruct((B, HG_HEADS, HG_DK, HG_DV), F32),
        ] + c_shape,
        scratch_shapes=[
            pltpu.VMEM((1, C), F32),
            pltpu.VMEM((2, cs, C), F32),
            pltpu.VMEM((2, cs, C), F32),
        ] + [pltpu.VMEM((nb, HG_DV, HG_DK), F32) for _ in range(HG_HEADS)],
        compiler_params=_params(("arbitrary", "arbitrary")),
        name="hgrn2",
    )(z3, z3, z3, z3, s0, hg_lb, gn.reshape(1, HG_DV), *casts)


def _softmax_rows(s):
    e = jnp.exp(s - jnp.max(s, axis=-1, keepdims=True))
    return e / jnp.sum(e, axis=-1, keepdims=True)


def _xattn_kernel(q_ref, k_ref, v_ref, o_ref):
    hls = [slice(h * XA_HD, (h + 1) * XA_HD) for h in range(XA_HEADS)]
    s = [_dot_nt(q_ref[0, :, hl].astype(BF16), k_ref[0, :, hl].astype(BF16)) * (XA_HD ** -0.5)
         for hl in hls]
    p = [_softmax_rows(sh).astype(BF16) for sh in s]
    for hl, ph in zip(hls, p):
        o = jnp.dot(ph, v_ref[0, :, hl].astype(BF16), preferred_element_type=F32)
        o_ref[0, :, hl] = o.astype(BF16)


def _xattn(z3, mk, mv, T):
    B, L, _ = z3.shape
    C = XA_HEADS * XA_HD
    mem = lambda: pl.BlockSpec((1, MEM_LEN, C), lambda b, t: (b, 0, 0))
    return pl.pallas_call(
        _xattn_kernel,
        grid=(B, L // T),
        in_specs=[pl.BlockSpec((1, T, C), lambda b, t: (b, t, COL_XQ)), mem(), mem()],
        out_specs=pl.BlockSpec((1, T, C), lambda b, t: (b, t, 0)),
        out_shape=jax.ShapeDtypeStruct((B, L, C), BF16),
        compiler_params=_params(("arbitrary", "arbitrary")),
        name="xattn",
    )(z3, mk, mv)


def _xattn_cached_kernel(q_ref, k_ref, v_ref, o_ref):
    kt = jnp.swapaxes(k_ref[...], 0, 1)
    vt = jnp.swapaxes(v_ref[...], 0, 1)
    hls = [slice(h * XA_HD, (h + 1) * XA_HD) for h in range(XA_HEADS)]
    s = [_dot_nt(q_ref[0, :, hl].astype(BF16), kt[h].astype(BF16)) * (XA_HD ** -0.5)
         for h, hl in enumerate(hls)]
    p = [_softmax_rows(sh).astype(BF16) for sh in s]
    for h, hl in enumerate(hls):
        o = jnp.dot(p[h], vt[h].astype(BF16), preferred_element_type=F32)
        o_ref[0, :, hl] = o.astype(BF16)


def _xattn_cached(z3, cache_k, cache_v, layer):
    B, L, _ = z3.shape
    C = XA_HEADS * XA_HD
    mem = lambda: pl.BlockSpec((None, None, MEM_LEN, XA_HEADS, XA_HD), lambda b: (layer, b, 0, 0, 0))
    return pl.pallas_call(
        _xattn_cached_kernel,
        grid=(B,),
        in_specs=[pl.BlockSpec((1, L, C), lambda b: (b, 0, COL_XQ)), mem(), mem()],
        out_specs=pl.BlockSpec((1, L, C), lambda b: (b, 0, 0)),
        out_shape=jax.ShapeDtypeStruct((B, L, C), BF16),
        compiler_params=_params(("arbitrary",)),
        name="xattn_cached",
    )(z3, cache_k, cache_v)


def _merge_kernel(x_ref, ya_ref, yb_ref, yc_ref, g0_ref, g1_ref, g2_ref, bg_ref, wb_ref, wo_ref,
                  pg_ref, o_ref):
    m = jnp.zeros(o_ref.shape, F32)
    for nb, (y_ref, g_ref) in enumerate(((ya_ref, g0_ref), (yb_ref, g1_ref), (yc_ref, g2_ref))):
        gate = _sigmoid(g_ref[...] + bg_ref[nb:nb + 1, :])
        m = m + gate * jnp.dot(y_ref[...], wb_ref[nb], preferred_element_type=F32)
    y = jnp.dot(m.astype(BF16), wo_ref[...], preferred_element_type=F32)
    o_ref[...] = x_ref[...] + y * _rms_scale(y) * pg_ref[...]


def _merge(x, ya, yb, yc, z, b_gate, wb, wo, pg, tm):
    n, d = x.shape
    c = BRANCH_WIDTH
    once = pl.Buffered(1)
    branch = lambda: pl.BlockSpec((tm, c), lambda i: (i, 0))
    gate = lambda nb: pl.BlockSpec((tm, d), lambda i: (i, GATE_COL0 + nb))
    return pl.pallas_call(
        _merge_kernel,
        grid=(n // tm,),
        in_specs=[
            pl.BlockSpec((tm, d), lambda i: (i, 0)),
            branch(), branch(), branch(),
            gate(0), gate(1), gate(2),
            pl.BlockSpec((N_BRANCH, d), lambda i: (0, 0)),
            pl.BlockSpec((N_BRANCH, c, d), lambda i: (0, 0, 0), pipeline_mode=once),
            pl.BlockSpec((d, d), lambda i: (0, 0), pipeline_mode=once),
            pl.BlockSpec((1, d), lambda i: (0, 0)),
        ],
        out_specs=pl.BlockSpec((tm, d), lambda i: (i, 0)),
        out_shape=jax.ShapeDtypeStruct((n, d), F32),
        compiler_params=_params(("arbitrary",)),
        name="merge",
    )(x, ya, yb, yc, z, z, z, b_gate, wb, wo, pg.reshape(1, d))


def _ffn_kernel(x_ref, g_ref, wu_ref, wv_ref, cw_ref, cb_ref, fc0_ref, wd_ref, pg_ref,
                o_ref, fc1_ref, xn_ref, acc_ref, hal_ref, car_ref, *, nb, T, tf):
    it = pl.program_id(1)
    j = pl.program_id(2)
    nj = pl.num_programs(2)
    hist = FFN_CONV_W - 1

    @pl.when(j == 0)
    def _():
        x = x_ref[...].reshape(nb * T, D_MODEL)
        xn_ref[...] = (x * _rms_scale(x) * g_ref[...]).astype(BF16)
        acc_ref[...] = jnp.zeros_like(acc_ref)

    @pl.when(it == 0)
    def _():
        car_ref[j] = fc0_ref[...]

    xn = xn_ref[...]
    u = jnp.dot(xn, wu_ref[...], preferred_element_type=F32)
    v = jnp.dot(xn, wv_ref[...], preferred_element_type=F32)
    hal_ref[:, HALO - hist:HALO, :] = car_ref[j]
    hal_ref[:, HALO:HALO + T, :] = u.reshape(nb, T, tf)
    cw = cw_ref[...]
    uc = cb_ref[...][None]
    for jj in range(FFN_CONV_W):
        uc = uc + hal_ref[:, HALO - hist + jj:HALO - hist + jj + T, :] * cw[jj:jj + 1][None]
    tail = hal_ref[:, HALO + T - hist:HALO + T, :]
    car_ref[j] = tail
    act = (jax.nn.gelu(uc).reshape(nb * T, tf) * v).astype(BF16)
    acc_ref[...] += jnp.dot(act, wd_ref[...], preferred_element_type=F32)

    @pl.when(j == nj - 1)
    def _():
        pg = pg_ref[...]
        for b in range(nb):
            for r in range(0, T, FFN_OUT_ROWS):
                y = acc_ref[b * T + r:b * T + r + FFN_OUT_ROWS, :]
                o_ref[b, r:r + FFN_OUT_ROWS, :] = x_ref[b, r:r + FFN_OUT_ROWS, :] + y * _rms_scale(y) * pg

    @pl.when((j == nj - 1) & (it == pl.num_programs(1) - 1))
    def _():
        for jj in range(FFN_DIM // tf):
            fc1_ref[:, :, jj * tf:(jj + 1) * tf] = car_ref[jj]


def _ffn(x3, g, w_up, cw, cb, fc0, w_down, pg, nb, T, tf):
    B, L, d = x3.shape
    F = FFN_DIM
    nj = F // tf
    hist = FFN_CONV_W - 1
    rows = lambda: pl.BlockSpec((nb, T, d), lambda b, t, j: (b, t, 0))
    return pl.pallas_call(
        functools.partial(_ffn_kernel, nb=nb, T=T, tf=tf),
        grid=(B // nb, L // T, nj),
        in_specs=[
            rows(),
            pl.BlockSpec((1, d), lambda b, t, j: (0, 0)),
            pl.BlockSpec((d, tf), lambda b, t, j: (0, j)),
            pl.BlockSpec((d, tf), lambda b, t, j: (0, nj + j)),
            pl.BlockSpec((FFN_CONV_W, tf), lambda b, t, j: (0, j)),
            pl.BlockSpec((1, tf), lambda b, t, j: (0, j)),
            pl.BlockSpec((nb, hist, tf), lambda b, t, j: (b, 0, j)),
            pl.BlockSpec((tf, d), lambda b, t, j: (j, 0)),
            pl.BlockSpec((1, d), lambda b, t, j: (0, 0)),
        ],
        out_specs=[rows(), pl.BlockSpec((nb, hist, F), lambda b, t, j: (b, 0, 0))],
        out_shape=[jax.ShapeDtypeStruct((B, L, d), F32), jax.ShapeDtypeStruct((B, hist, F), F32)],
        scratch_shapes=[
            pltpu.VMEM((nb * T, d), BF16),
            pltpu.VMEM((nb * T, d), F32),
            pltpu.VMEM((nb, HALO + T, tf), F32),
            pltpu.VMEM((nj, nb, hist, tf), F32),
        ],
        compiler_params=_params(("arbitrary", "arbitrary", "arbitrary")),
        name="ffn",
    )(x3, g.reshape(1, d), w_up, w_up, cw, cb.reshape(1, F), fc0, w_down, pg.reshape(1, d))


def _block_diag(w):
    nblk, bi, bj = w.shape
    per = MXU_COLS // bi
    eye = jnp.eye(per, dtype=w.dtype)
    grouped = jnp.einsum("ghij,hk->ghikj", w.reshape(nblk // per, per, bi, bj), eye)
    return grouped.reshape(nblk // per, per * bi, per * bj)


def _tiles(B, L):
    T = min(L, 256)
    return dict(
        proj_tm=min(B * L, 1024), proj_tn=2048,
        rnn_nb=B if B * T <= 512 else 512 // T, rnn_T=T,
        hg_nb=min(B, 4), hg_T=T,
        xa_T=min(L, 2048),
        merge_tm=256,
        ffn_nb=B if B * T <= 512 else 512 // T, ffn_T=T, ffn_tf=512,
    )


def _trunk_layer(x, mk, mv, h0, rc0, s0, fc0, w, layer):
    B, L, d = x.shape
    n = B * L
    t = _tiles(B, L)
    x2 = x.reshape(n, d)
    made = {}
    tm, tn = t["proj_tm"], t["proj_tn"]
    if w["w_in"].dtype == F32:
        z, made["w_in"] = _norm_matmul(x2, w["pre_mix_norm"], w["w_in"], tm, tn // 2, "in_proj",
                                       tiles=(0, 1), emit_w=True)
        if n > tm:
            z = _norm_matmul(x2, w["pre_mix_norm"], made["w_in"], tm, tn, "in_proj",
                             tiles=(1, n // tm), into=z)
    else:
        z = _norm_matmul(x2, w["pre_mix_norm"], w["w_in"], tm, tn, "in_proj")
    z3 = z.reshape(B, L, IN_COLS)
    c = BRANCH_WIDTH
    fresh = w["w_out"].dtype == F32
    ya, h1, rc1, *cast = _rglru(
        z3, rc0, h0, w["rnn_conv_w"], w["rnn_conv_b"], w["lru_wa"], w["lru_ba"], w["lru_wx"],
        w["lru_bx"], w["lru_lambda"], t["rnn_nb"], t["rnn_T"],
        casts=(w["w_branch"].reshape(N_BRANCH * c, d), w["w_out"]) if fresh else ())
    if fresh:
        made["w_branch"], made["w_out"] = cast[0].reshape(N_BRANCH, c, d), cast[1]
    yb, s1, *cast = _hgrn2(z3, s0, w["hg_lb"], w["hg_norm"], layer, t["hg_nb"], t["hg_T"],
                           casts=(w["w_ffn_up"], w["w_ffn_down"]) if fresh else ())
    if fresh:
        made["w_ffn_up"], made["w_ffn_down"] = cast
    w = {**w, **made}
    yc = _xattn(z3, mk, mv, t["xa_T"]) if mv is not None else _xattn_cached(z3, *mk, layer)
    x1 = _merge(x2, ya.reshape(n, c), yb.reshape(n, c), yc.reshape(n, c), z, w["b_gate"],
                w["w_branch"], w["w_out"], w["post_mix_norm"], t["merge_tm"]).reshape(B, L, d)
    xo, fc1 = _ffn(x1, w["pre_ffn_norm"], w["w_ffn_up"], w["ffn_conv_w"], w["ffn_conv_b"], fc0,
                   w["w_ffn_down"], w["post_ffn_norm"], t["ffn_nb"], t["ffn_T"], t["ffn_tf"])
    return xo, h1, rc1, s1, fc1, made


def kernel(x_prompt, x_sample, cache_mem_k, cache_mem_v, state_rnn_h, state_rnn_conv, state_hg,
           state_ffn_conv, mem_prompt, pre_mix_norm, w_in, rnn_conv_w, rnn_conv_b, lru_wa, lru_ba,
           lru_wx, lru_bx, lru_lambda, hg_lb, hg_norm, mem_norm, w_mem_kv, w_branch, b_gate, w_out,
           post_mix_norm, pre_ffn_norm, w_ffn_up, ffn_conv_w, ffn_conv_b, w_ffn_down, post_ffn_norm):
    depth = w_in.shape[0]
    Bp = x_prompt.shape[0]
    Bs = x_sample.shape[0]
    xa_w = XA_HEADS * XA_HD
    yp, ys = x_prompt, x_sample
    outs = [[] for _ in range(10)]
    for l in range(depth):
        w = {
            "pre_mix_norm": pre_mix_norm[l], "w_in": w_in[l],
            "rnn_conv_w": rnn_conv_w[l], "rnn_conv_b": rnn_conv_b[l],
            "lru_wa": _block_diag(lru_wa[l]).astype(BF16), "lru_ba": lru_ba[l],
            "lru_wx": _block_diag(lru_wx[l]).astype(BF16), "lru_bx": lru_bx[l],
            "lru_lambda": lru_lambda[l], "hg_lb": hg_lb, "hg_norm": hg_norm[l],
            "w_branch": w_branch[l], "b_gate": b_gate[l],
            "w_out": w_out[l], "post_mix_norm": post_mix_norm[l],
            "pre_ffn_norm": pre_ffn_norm[l], "w_ffn_up": w_ffn_up[l],
            "ffn_conv_w": ffn_conv_w[l], "ffn_conv_b": ffn_conv_b[l],
            "w_ffn_down": w_ffn_down[l], "post_ffn_norm": post_ffn_norm[l],
        }
        mem2 = mem_prompt.reshape(Bp * MEM_LEN, D_MODEL)
        kv = _norm_matmul(mem2, mem_norm[l], w_mem_kv[l], Bp * MEM_LEN, 1024, "mem_kv")
        kv = kv.reshape(Bp, MEM_LEN, 2 * xa_w)
        mk_p, mv_p = kv[..., :xa_w], kv[..., xa_w:]
        yp, h_p, rc_p, s_p, fc_p, made = _trunk_layer(
            yp, mk_p, mv_p, jnp.zeros((Bp, RNN_WIDTH), F32),
            jnp.zeros((Bp, RNN_CONV_W - 1, RNN_WIDTH), F32),
            jnp.zeros((Bp, HG_HEADS, HG_DK, HG_DV), F32),
            jnp.zeros((Bp, FFN_CONV_W - 1, FFN_DIM), F32), w, l)
        w = {**w, **made}
        ys, h_s, rc_s, s_s, fc_s, _ = _trunk_layer(
            ys, (cache_mem_k, cache_mem_v), None,
            state_rnn_h[l], state_rnn_conv[l], state_hg[l], state_ffn_conv[l], w, l)
        layer_out = (mk_p.reshape(Bp, MEM_LEN, XA_HEADS, XA_HD), mv_p.reshape(Bp, MEM_LEN, XA_HEADS, XA_HD),
                     h_p, rc_p, s_p, fc_p, h_s, rc_s, s_s, fc_s)
        for acc, val in zip(outs, layer_out):
            acc.append(val)
    return (yp, ys) + tuple(jnp.stack(o) for o in outs)
```

```python
import functools

import jax
import jax.numpy as jnp
from jax import lax
from jax.experimental import pallas as pl
from jax.experimental.pallas import tpu as pltpu

F32 = jnp.float32
BF16 = jnp.bfloat16

D_MODEL = 2048
RNN_WIDTH = 1024
RNN_BLOCKS = 16
RNN_CONV_W = 4
LRU_C = 8.0
HG_HEADS = 8
HG_DK = 128
HG_DV = 128
HG_CHUNK = 128
HG_SUB_MAX = 16
SUBLANES = 8
MXU_COLS = 256
HG_SKEW = 8
MEM_LEN = 256
XA_HEADS = 4
XA_HD = 256
BRANCH_WIDTH = 1024
N_BRANCH = 3
FFN_DIM = 5632
FFN_CONV_W = 3
FFN_OUT_ROWS = 8
EPS = 1e-6
LOG2E = 1.4426950408889634
IN_COLS = 6 * BRANCH_WIDTH + N_BRANCH * D_MODEL

COL_RNN, COL_HQ, COL_HF, COL_HI, COL_HO, COL_XQ = range(6)
GATE_COL0 = 3

HALO = 8
VMEM_LIMIT = 56 * 1024 * 1024


def _params(sem):
    return pltpu.CompilerParams(dimension_semantics=sem, vmem_limit_bytes=VMEM_LIMIT)


def _rms_scale(x):
    return lax.rsqrt(jnp.mean(x * x, axis=-1, keepdims=True) + EPS)


_sigmoid = jax.nn.sigmoid


def _sigmoid_tanh(x):
    return 0.5 * jnp.tanh(0.5 * x) + 0.5


def _with_casts(kernel, n_in, n_out, n_cast):
    def wrapped(*refs, **kw):
        ins, refs = refs[:n_in], refs[n_in:]
        cast_in, refs = refs[:n_cast], refs[n_cast:]
        outs, refs = refs[:n_out], refs[n_out:]
        cast_out, scratch = refs[:n_cast], refs[n_cast:]
        for src, dst in zip(cast_in, cast_out):
            dst[...] = src[...].astype(BF16)
        kernel(*ins, *outs, *scratch, **kw)
    return wrapped


def _cast_specs(casts, nsteps, step):
    in_specs, out_specs, out_shape = [], [], []
    for w in casts:
        rows, cols = w.shape[0] // nsteps, w.shape[1]
        assert rows * nsteps == w.shape[0] and rows % (2 * SUBLANES) == 0
        for specs in (in_specs, out_specs):
            specs.append(pl.BlockSpec((rows, cols), lambda *g, step=step: (step(*g), 0)))
        out_shape.append(jax.ShapeDtypeStruct(w.shape, BF16))
    return in_specs, out_specs, out_shape


def _norm_matmul_kernel(x_ref, g_ref, w_ref, *rest, emit_w, aliased):
    rest = rest[1:] if aliased else rest
    o_ref, xn_ref = rest[0], rest[-1]

    @pl.when(pl.program_id(1) == 0)
    def _():
        x = x_ref[...]
        xn_ref[...] = (x * _rms_scale(x) * g_ref[...]).astype(BF16)

    w = w_ref[...].astype(BF16)
    if emit_w:
        rest[1][...] = w
    o_ref[...] = jnp.dot(xn_ref[...], w, preferred_element_type=F32)


def _norm_matmul(x, g, w, tm, tn, name, tiles=None, emit_w=False, into=None):
    n, d = x.shape
    c = w.shape[1]
    t0, t1 = tiles if tiles is not None else (0, n // tm)
    assert not emit_w or t1 - t0 == 1
    in_specs = [
        pl.BlockSpec((tm, d), lambda i, j: (i + t0, 0)),
        pl.BlockSpec((1, d), lambda i, j: (0, 0)),
        pl.BlockSpec((d, tn), lambda i, j: (0, j)),
    ]
    args = [x, g.reshape(1, d), w]
    out_specs = [pl.BlockSpec((tm, tn), lambda i, j: (i + t0, j))]
    out_shape = [jax.ShapeDtypeStruct((n, c), F32)]
    if into is not None:
        in_specs.append(pl.BlockSpec(memory_space=pl.ANY))
        args.append(into)
    if emit_w:
        out_specs.append(pl.BlockSpec((d, tn), lambda i, j: (0, j)))
        out_shape.append(jax.ShapeDtypeStruct((d, c), BF16))
    res = pl.pallas_call(
        functools.partial(_norm_matmul_kernel, emit_w=emit_w, aliased=into is not None),
        grid=(t1 - t0, c // tn),
        in_specs=in_specs,
        out_specs=out_specs,
        out_shape=out_shape,
        scratch_shapes=[pltpu.VMEM((tm, d), BF16)],
        input_output_aliases={3: 0} if into is not None else {},
        compiler_params=_params(("arbitrary", "arbitrary")),
        name=name,
    )(*args)
    return res if emit_w else res[0]


def _rglru_kernel(z_ref, rc0_ref, h0_ref, cw_ref, cb_ref, wa_ref, ba_ref, wx_ref, bx_ref, lam_ref,
                  ya_ref, h1_ref, rc1_ref, xp_ref, a_ref, b_ref, hs_ref, hc_ref, *, nb, T):
    it = pl.program_id(1)
    nt = pl.num_programs(1)
    hist = RNN_CONV_W - 1

    @pl.when(it == 0)
    def _():
        xp_ref[:, HALO - hist:HALO, :] = rc0_ref[...]
        hc_ref[...] = h0_ref[...]

    x = z_ref[...]
    cw = cw_ref[...]
    xr = cb_ref[...][None]
    for j in range(RNN_CONV_W):
        xs = x if j == hist else pltpu.roll(x, hist - j, 1)
        xr = xr + xs * cw[j:j + 1][None]
    xp_ref[:, HALO:2 * HALO, :] = x[:, :HALO]
    head = cb_ref[...][None]
    for j in range(RNN_CONV_W):
        head = head + xp_ref[:, HALO - hist + j:2 * HALO - hist + j, :] * cw[j:j + 1][None]
    xr = jnp.concatenate([head, xr[:, HALO:]], axis=1)
    xp_ref[:, HALO - hist:HALO, :] = x[:, T - hist:]

    xr2 = xr.reshape(nb * T, RNN_WIDTH)
    xb = xr2.astype(BF16)
    def gate(w_ref, bias_ref):
        cols = [jnp.dot(xb[:, g * MXU_COLS:(g + 1) * MXU_COLS], w_ref[g], preferred_element_type=F32)
                for g in range(RNN_WIDTH // MXU_COLS)]
        return _sigmoid_tanh(jnp.concatenate(cols, axis=1) + bias_ref[...])

    r = gate(wa_ref, ba_ref)
    ig = gate(wx_ref, bx_ref)
    nl = -lam_ref[...]
    softplus = jnp.maximum(nl, 0.0) + jnp.log1p(jnp.exp(-jnp.abs(nl)))
    log_a = -LRU_C * r * softplus
    a = jnp.exp(log_a)
    a_ref[...] = a.reshape(nb, T, RNN_WIDTH)
    one_minus_a2 = -jnp.tanh(log_a) * (a * a + 1.0)
    b_ref[...] = (jnp.sqrt(one_minus_a2) * (ig * xr2)).reshape(nb, T, RNN_WIDTH)

    hs = [hc_ref[bi:bi + 1, :] for bi in range(nb)]
    for t in range(T):
        for bi in range(nb):
            hs[bi] = a_ref[bi, t:t + 1, :] * hs[bi] + b_ref[bi, t:t + 1, :]
            hs_ref[bi, t:t + 1, :] = hs[bi]
    for bi in range(nb):
        hc_ref[bi:bi + 1, :] = hs[bi]
    ya_ref[...] = hs_ref[...].astype(BF16)

    @pl.when(it == nt - 1)
    def _():
        h1_ref[...] = hc_ref[...]
        rc1_ref[...] = xp_ref[:, HALO - hist:HALO, :]


def _rglru(z3, rc0, h0, cw, cb, wa, ba, wx, bx, lam, nb, T, casts=()):
    B, L, _ = z3.shape
    C = RNN_WIDTH
    hist = RNN_CONV_W - 1
    vec = lambda: pl.BlockSpec((1, C), lambda b, t: (0, 0))
    mat = lambda: pl.BlockSpec((C // MXU_COLS, MXU_COLS, MXU_COLS), lambda b, t: (0, 0, 0))
    nt = L // T
    c_in, c_out, c_shape = _cast_specs(casts, (B // nb) * nt, lambda b, t: b * nt + t)
    return pl.pallas_call(
        functools.partial(_with_casts(_rglru_kernel, 10, 3, len(casts)), nb=nb, T=T),
        grid=(B // nb, nt),
        in_specs=[
            pl.BlockSpec((nb, T, C), lambda b, t: (b, t, COL_RNN)),
            pl.BlockSpec((nb, hist, C), lambda b, t: (b, 0, 0)),
            pl.BlockSpec((nb, C), lambda b, t: (b, 0)),
            pl.BlockSpec((RNN_CONV_W, C), lambda b, t: (0, 0)),
            vec(), mat(), vec(), mat(), vec(), vec(),
        ] + c_in,
        out_specs=[
            pl.BlockSpec((nb, T, C), lambda b, t: (b, t, 0)),
            pl.BlockSpec((nb, C), lambda b, t: (b, 0)),
            pl.BlockSpec((nb, hist, C), lambda b, t: (b, 0, 0)),
        ] + c_out,
        out_shape=[
            jax.ShapeDtypeStruct((B, L, C), BF16),
            jax.ShapeDtypeStruct((B, C), F32),
            jax.ShapeDtypeStruct((B, hist, C), F32),
        ] + c_shape,
        scratch_shapes=[
            pltpu.VMEM((nb, HALO + T, C), F32),
            pltpu.VMEM((nb, T, C), F32),
            pltpu.VMEM((nb, T, C), F32),
            pltpu.VMEM((nb, T, C), F32),
            pltpu.VMEM((nb, C), F32),
        ],
        compiler_params=_params(("arbitrary", "arbitrary")),
        name="rglru",
    )(z3, rc0, h0, cw, cb.reshape(1, C), wa, ba.reshape(1, C), wx, bx.reshape(1, C), lam.reshape(1, C),
      *casts)


def _dot_nt(a, b):
    return lax.dot_general(a, b, (((1,), (1,)), ((), ())), preferred_element_type=F32)


def _split3(x):
    hi = x.astype(BF16)
    r = x - hi.astype(F32)
    mid = r.astype(BF16)
    lo = (r - mid.astype(F32)).astype(BF16)
    return jnp.concatenate([hi, mid, lo], axis=0)


def _hg_sub_rows(cs):
    return HG_SUB_MAX if cs >= 4 * HG_SUB_MAX else SUBLANES


def _hgrn2_scores(qh, bc, ck, ck_row, st, cs):
    o = _dot_nt((qh * jnp.exp2(bc)).astype(BF16), st.astype(BF16))

    sub_rows = _hg_sub_rows(cs)
    nblk = cs // sub_rows
    a_rows = [None] * nblk
    half = cs // 2
    while half >= sub_rows:
        npair = cs // (2 * half)
        q_parts, k_parts = [], []
        for p in range(npair):
            lo, mid, hi = 2 * half * p, 2 * half * p + half, 2 * half * (p + 1)
            br = bc[mid - 1:mid]
            q_parts.append(qh[mid:hi] * jnp.exp2(bc[mid:hi] - br))
            k_parts += [jnp.exp2(br - ck[lo:mid]), jnp.zeros((half, HG_DK), F32)]
        off = _dot_nt(jnp.concatenate(q_parts, axis=0).astype(BF16),
                      jnp.concatenate(k_parts, axis=0).astype(BF16))
        if npair > 1:
            lg = half.bit_length() - 1
            rowp = lax.shift_right_logical(lax.broadcasted_iota(jnp.int32, off.shape, 0), lg)
            colp = lax.shift_right_logical(lax.broadcasted_iota(jnp.int32, off.shape, 1), lg + 1)
            off = jnp.where(rowp == colp, off, 0.0)
        for p in range(npair):
            for sub in range(half // sub_rows):
                blk = (2 * half * p + half) // sub_rows + sub
                piece = off[p * half + sub * sub_rows:p * half + (sub + 1) * sub_rows]
                a_rows[blk] = piece if a_rows[blk] is None else a_rows[blk] + piece
        half //= 2

    ones = jnp.ones((HG_DK, HG_DK), BF16)
    zero = jnp.zeros((SUBLANES, HG_DK), F32)
    groups = sub_rows // SUBLANES
    lane_sums = []
    for blk in range(nblk):
        r0 = blk * sub_rows
        bg = [bc[r0 + v * SUBLANES:r0 + (v + 1) * SUBLANES] for v in range(groups)]
        qg = [qh[r0 + v * SUBLANES:r0 + (v + 1) * SUBLANES] for v in range(groups)]
        pair = []
        for s in range(sub_rows):
            cks = ck_row(r0 + s)
            pair += [qg[v] * jnp.exp2(bg[v] - cks) if v >= s // SUBLANES else zero
                     for v in range(groups)]
        lane_sums.append(jnp.dot(jnp.concatenate(pair, axis=0).astype(BF16), ones,
                                 preferred_element_type=F32))
    return o, a_rows, lane_sums


def _hgrn2_output(scores, vh, ogh, gnh, bl, ck, st, cs):
    o, a_rows, lane_sums = scores
    sub_rows = _hg_sub_rows(cs)
    groups = sub_rows // SUBLANES
    lane = lax.broadcasted_iota(jnp.int32, (SUBLANES, HG_DK), 1)
    tloc = lax.broadcasted_iota(jnp.int32, (SUBLANES, HG_DK), 0)
    mask = [[(lane == s) & (tloc >= s - v * SUBLANES) if s > v * SUBLANES else (lane == s)
             for v in range(groups)] for s in range(sub_rows)]
    a_rows = list(a_rows)
    for blk, lane_sum in enumerate(lane_sums):
        a_g = [jnp.zeros((SUBLANES, HG_DK), F32)] * groups
        for s in range(sub_rows):
            for v in range(s // SUBLANES, groups):
                r = s * sub_rows + v * SUBLANES
                a_g[v] = jnp.where(mask[s][v], lane_sum[r:r + SUBLANES], a_g[v])
        a_d = jnp.concatenate(a_g, axis=0)
        if blk:
            a_d = pltpu.roll(a_d, blk * sub_rows, 1)
        a_d = a_d[:, :cs]
        a_rows[blk] = a_d if a_rows[blk] is None else a_rows[blk] + a_d
    a = jnp.concatenate(a_rows, axis=0)
    o = o + jnp.dot(a.astype(BF16), vh.astype(BF16), preferred_element_type=F32)
    kd = jnp.exp2(bl - ck).astype(BF16)
    st_new = st * jnp.exp2(bl) + jnp.dot(vh.T.astype(BF16), kd, preferred_element_type=F32)
    y = o * _rms_scale(o) * gnh * _sigmoid(ogh)
    return y, st_new


def _hgrn2_kernel(q_ref, f_ref, v_ref, og_ref, s0_ref, lbp_ref, gn_ref, yb_ref, s1_ref,
                  lb_ref, bc_ref, ck_ref, *st_refs, nb, T, cs, layer):
    it = pl.program_id(1)
    nt = pl.num_programs(1)
    row = lax.broadcasted_iota(jnp.int32, (cs, 3 * cs), 0)
    col = lax.broadcasted_iota(jnp.int32, (cs, 3 * cs), 1)
    tri3 = (row >= (col & (cs - 1))).astype(BF16)

    @pl.when(it == 0)
    def _():
        p = lbp_ref[...]
        e = jnp.exp(p - jnp.max(p, axis=0, keepdims=True))
        sm = e / jnp.sum(e, axis=0, keepdims=True)
        lb_ref[...] = jnp.sum(sm[:layer + 1], axis=0, keepdims=True)

        def init(b, c):
            for h in range(HG_HEADS):
                st_refs[h][b] = s0_ref[b, h].T
            return c
        lax.fori_loop(0, nb, init, 0)

    nch = T // cs
    par = 2 if nb % 2 == 0 else 1

    def chunk(idx, c):
        rows = pl.ds(pl.multiple_of((idx % nch) * cs, cs), cs)
        lb = lb_ref[...]
        units = []
        for k in range(par):
            b = (idx // nch) * par + k
            f = lb + (1.0 - lb) * _sigmoid(f_ref[b, rows, :])
            bc = jnp.dot(tri3, _split3(jnp.log(f) * LOG2E), preferred_element_type=F32)
            bc_ref[k] = bc
            ck_ref[k] = bc - jnp.log(1.0 - f) * LOG2E
            units += [(k, b, h) for h in range(HG_HEADS)]

        def scores(k, b, h):
            hl = slice(h * HG_DK, (h + 1) * HG_DK)
            return _hgrn2_scores(q_ref[b, rows, hl], bc_ref[k, :, hl], ck_ref[k, :, hl],
                                 lambda r: ck_ref[k, r:r + 1, hl], st_refs[h][b], cs)

        def output(k, b, h, sc):
            hl = slice(h * HG_DK, (h + 1) * HG_DK)
            y, st_new = _hgrn2_output(sc, v_ref[b, rows, hl], og_ref[b, rows, hl], gn_ref[...],
                                      bc_ref[k, cs - 1:cs, hl], ck_ref[k, :, hl], st_refs[h][b], cs)
            yb_ref[b, rows, hl] = y.astype(BF16)
            st_refs[h][b] = st_new

        pending = {}
        for n in range(len(units) + HG_SKEW):
            if n < len(units):
                pending[n] = scores(*units[n])
            if n >= HG_SKEW:
                output(*units[n - HG_SKEW], pending.pop(n - HG_SKEW))
        return c
    lax.fori_loop(0, (nb // par) * nch, chunk, 0)

    @pl.when(it == nt - 1)
    def _():
        def fin(b, c):
            for h in range(HG_HEADS):
                s1_ref[b, h] = st_refs[h][b].T
            return c
        lax.fori_loop(0, nb, fin, 0)


def _hgrn2(z3, s0, hg_lb, gn, layer, nb, T, casts=()):
    B, L, _ = z3.shape
    C = HG_HEADS * HG_DK
    cs = min(HG_CHUNK, L)
    zcol = lambda col: pl.BlockSpec((nb, T, C), lambda b, t: (b, t, col))
    st_spec = lambda: pl.BlockSpec((nb, HG_HEADS, HG_DK, HG_DV), lambda b, t: (b, 0, 0, 0))
    nl = hg_lb.shape[0]
    nt = L // T
    c_in, c_out, c_shape = _cast_specs(casts, (B // nb) * nt, lambda b, t: b * nt + t)
    return pl.pallas_call(
        functools.partial(_with_casts(_hgrn2_kernel, 7, 2, len(casts)), nb=nb, T=T, cs=cs, layer=layer),
        grid=(B // nb, nt),
        in_specs=[
            zcol(COL_HQ), zcol(COL_HF), zcol(COL_HI), zcol(COL_HO),
            st_spec(),
            pl.BlockSpec((nl, C), lambda b, t: (0, 0)),
            pl.BlockSpec((1, HG_DV), lambda b, t: (0, 0)),
        ] + c_in,
        out_specs=[pl.BlockSpec((nb, T, C), lambda b, t: (b, t, 0)), st_spec()] + c_out,
        out_shape=[
            jax.ShapeDtypeStruct((B, L, C), BF16),
            jax.ShapeDtypeStruct((B, HG_HEADS, HG_DK, HG_DV), F32),
        ] + c_shape,
        scratch_shapes=[
            pltpu.VMEM((1, C), F32),
            pltpu.VMEM((2, cs, C), F32),
            pltpu.VMEM((2, cs, C), F32),
        ] + [pltpu.VMEM((nb, HG_DV, HG_DK), F32) for _ in range(HG_HEADS)],
        compiler_params=_params(("arbitrary", "arbitrary")),
        name="hgrn2",
    )(z3, z3, z3, z3, s0, hg_lb, gn.reshape(1, HG_DV), *casts)


def _softmax_rows(s):
    e = jnp.exp(s - jnp.max(s, axis=-1, keepdims=True))
    return e / jnp.sum(e, axis=-1, keepdims=True)


def _xattn_kernel(q_ref, k_ref, v_ref, o_ref):
    hls = [slice(h * XA_HD, (h + 1) * XA_HD) for h in range(XA_HEADS)]
    s = [_dot_nt(q_ref[0, :, hl].astype(BF16), k_ref[0, :, hl].astype(BF16)) * (XA_HD ** -0.5)
         for hl in hls]
    p = [_softmax_rows(sh).astype(BF16) for sh in s]
    for hl, ph in zip(hls, p):
        o = jnp.dot(ph, v_ref[0, :, hl].astype(BF16), preferred_element_type=F32)
        o_ref[0, :, hl] = o.astype(BF16)


def _xattn(z3, mk, mv, T):
    B, L, _ = z3.shape
    C = XA_HEADS * XA_HD
    mem = lambda: pl.BlockSpec((1, MEM_LEN, C), lambda b, t: (b, 0, 0))
    return pl.pallas_call(
        _xattn_kernel,
        grid=(B, L // T),
        in_specs=[pl.BlockSpec((1, T, C), lambda b, t: (b, t, COL_XQ)), mem(), mem()],
        out_specs=pl.BlockSpec((1, T, C), lambda b, t: (b, t, 0)),
        out_shape=jax.ShapeDtypeStruct((B, L, C), BF16),
        compiler_params=_params(("arbitrary", "arbitrary")),
        name="xattn",
    )(z3, mk, mv)


def _xattn_cached_kernel(q_ref, k_ref, v_ref, o_ref):
    kt = jnp.swapaxes(k_ref[...], 0, 1)
    vt = jnp.swapaxes(v_ref[...], 0, 1)
    hls = [slice(h * XA_HD, (h + 1) * XA_HD) for h in range(XA_HEADS)]
    s = [_dot_nt(q_ref[0, :, hl].astype(BF16), kt[h].astype(BF16)) * (XA_HD ** -0.5)
         for h, hl in enumerate(hls)]
    p = [_softmax_rows(sh).astype(BF16) for sh in s]
    for h, hl in enumerate(hls):
        o = jnp.dot(p[h], vt[h].astype(BF16), preferred_element_type=F32)
        o_ref[0, :, hl] = o.astype(BF16)


def _xattn_cached(z3, cache_k, cache_v, layer):
    B, L, _ = z3.shape
    C = XA_HEADS * XA_HD
    mem = lambda: pl.BlockSpec((None, None, MEM_LEN, XA_HEADS, XA_HD), lambda b: (layer, b, 0, 0, 0))
    return pl.pallas_call(
        _xattn_cached_kernel,
        grid=(B,),
        in_specs=[pl.BlockSpec((1, L, C), lambda b: (b, 0, COL_XQ)), mem(), mem()],
        out_specs=pl.BlockSpec((1, L, C), lambda b: (b, 0, 0)),
        out_shape=jax.ShapeDtypeStruct((B, L, C), BF16),
        compiler_params=_params(("arbitrary",)),
        name="xattn_cached",
    )(z3, cache_k, cache_v)


def _merge_kernel(x_ref, ya_ref, yb_ref, yc_ref, g0_ref, g1_ref, g2_ref, bg_ref, wb_ref, wo_ref,
                  pg_ref, o_ref):
    tm, d = o_ref.shape
    halves = []
    for c in range(2):
        cols = slice(c * d // 2, (c + 1) * d // 2)
        m = jnp.zeros((tm, d // 2), F32)
        for nb, (y_ref, g_ref) in enumerate(((ya_ref, g0_ref), (yb_ref, g1_ref), (yc_ref, g2_ref))):
            gate = _sigmoid(g_ref[:, cols] + bg_ref[nb:nb + 1, cols])
            m = m + gate * jnp.dot(y_ref[...], wb_ref[nb, :, cols], preferred_element_type=F32)
        halves.append(m.astype(BF16))
    y = jnp.dot(jnp.concatenate(halves, axis=1), wo_ref[...], preferred_element_type=F32)
    o_ref[...] = x_ref[...] + y * _rms_scale(y) * pg_ref[...]


def _merge(x, ya, yb, yc, z, b_gate, wb, wo, pg, tm):
    n, d = x.shape
    c = BRANCH_WIDTH
    once = pl.Buffered(1)
    branch = lambda: pl.BlockSpec((tm, c), lambda i: (i, 0))
    gate = lambda nb: pl.BlockSpec((tm, d), lambda i: (i, GATE_COL0 + nb))
    return pl.pallas_call(
        _merge_kernel,
        grid=(n // tm,),
        in_specs=[
            pl.BlockSpec((tm, d), lambda i: (i, 0)),
            branch(), branch(), branch(),
            gate(0), gate(1), gate(2),
            pl.BlockSpec((N_BRANCH, d), lambda i: (0, 0)),
            pl.BlockSpec((N_BRANCH, c, d), lambda i: (0, 0, 0), pipeline_mode=once),
            pl.BlockSpec((d, d), lambda i: (0, 0), pipeline_mode=once),
            pl.BlockSpec((1, d), lambda i: (0, 0)),
        ],
        out_specs=pl.BlockSpec((tm, d), lambda i: (i, 0)),
        out_shape=jax.ShapeDtypeStruct((n, d), F32),
        compiler_params=_params(("arbitrary",)),
        name="merge",
    )(x, ya, yb, yc, z, z, z, b_gate, wb, wo, pg.reshape(1, d))


def _ffn_kernel(x_ref, g_ref, wu_ref, wv_ref, cw_ref, cb_ref, fc0_ref, wd_ref, pg_ref,
                o_ref, fc1_ref, xn_ref, acc_ref, hal_ref, car_ref, *, nb, T, tf):
    it = pl.program_id(1)
    j = pl.program_id(2)
    nj = pl.num_programs(2)
    hist = FFN_CONV_W - 1

    @pl.when(j == 0)
    def _():
        x = x_ref[...].reshape(nb * T, D_MODEL)
        xn_ref[...] = (x * _rms_scale(x) * g_ref[...]).astype(BF16)
        acc_ref[...] = jnp.zeros_like(acc_ref)

    @pl.when(it == 0)
    def _():
        car_ref[j] = fc0_ref[...]

    xn = xn_ref[...]
    u = jnp.dot(xn, wu_ref[...], preferred_element_type=F32)
    v = jnp.dot(xn, wv_ref[...], preferred_element_type=F32)
    hal_ref[:, HALO - hist:HALO, :] = car_ref[j]
    hal_ref[:, HALO:HALO + T, :] = u.reshape(nb, T, tf)
    cw = cw_ref[...]
    uc = cb_ref[...][None]
    for jj in range(FFN_CONV_W):
        uc = uc + hal_ref[:, HALO - hist + jj:HALO - hist + jj + T, :] * cw[jj:jj + 1][None]
    tail = hal_ref[:, HALO + T - hist:HALO + T, :]
    car_ref[j] = tail
    act = (jax.nn.gelu(uc).reshape(nb * T, tf) * v).astype(BF16)
    acc_ref[...] += jnp.dot(act, wd_ref[...], preferred_element_type=F32)

    @pl.when(j == nj - 1)
    def _():
        pg = pg_ref[...]
        for b in range(nb):
            for r in range(0, T, FFN_OUT_ROWS):
                y = acc_ref[b * T + r:b * T + r + FFN_OUT_ROWS, :]
                o_ref[b, r:r + FFN_OUT_ROWS, :] = x_ref[b, r:r + FFN_OUT_ROWS, :] + y * _rms_scale(y) * pg

    @pl.when((j == nj - 1) & (it == pl.num_programs(1) - 1))
    def _():
        for jj in range(FFN_DIM // tf):
            fc1_ref[:, :, jj * tf:(jj + 1) * tf] = car_ref[jj]


def _ffn(x3, g, w_up, cw, cb, fc0, w_down, pg, nb, T, tf):
    B, L, d = x3.shape
    F = FFN_DIM
    nj = F // tf
    hist = FFN_CONV_W - 1
    rows = lambda: pl.BlockSpec((nb, T, d), lambda b, t, j: (b, t, 0))
    return pl.pallas_call(
        functools.partial(_ffn_kernel, nb=nb, T=T, tf=tf),
        grid=(B // nb, L // T, nj),
        in_specs=[
            rows(),
            pl.BlockSpec((1, d), lambda b, t, j: (0, 0)),
            pl.BlockSpec((d, tf), lambda b, t, j: (0, j)),
            pl.BlockSpec((d, tf), lambda b, t, j: (0, nj + j)),
            pl.BlockSpec((FFN_CONV_W, tf), lambda b, t, j: (0, j)),
            pl.BlockSpec((1, tf), lambda b, t, j: (0, j)),
            pl.BlockSpec((nb, hist, tf), lambda b, t, j: (b, 0, j)),
            pl.BlockSpec((tf, d), lambda b, t, j: (j, 0)),
            pl.BlockSpec((1, d), lambda b, t, j: (0, 0)),
        ],
        out_specs=[rows(), pl.BlockSpec((nb, hist, F), lambda b, t, j: (b, 0, 0))],
        out_shape=[jax.ShapeDtypeStruct((B, L, d), F32), jax.ShapeDtypeStruct((B, hist, F), F32)],
        scratch_shapes=[
            pltpu.VMEM((nb * T, d), BF16),
            pltpu.VMEM((nb * T, d), F32),
            pltpu.VMEM((nb, HALO + T, tf), F32),
            pltpu.VMEM((nj, nb, hist, tf), F32),
        ],
        compiler_params=_params(("arbitrary", "arbitrary", "arbitrary")),
        name="ffn",
    )(x3, g.reshape(1, d), w_up, w_up, cw, cb.reshape(1, F), fc0, w_down, pg.reshape(1, d))


def _block_diag(w):
    nblk, bi, bj = w.shape
    per = MXU_COLS // bi
    eye = jnp.eye(per, dtype=w.dtype)
    grouped = jnp.einsum("ghij,hk->ghikj", w.reshape(nblk // per, per, bi, bj), eye)
    return grouped.reshape(nblk // per, per * bi, per * bj)


def _tiles(B, L):
    T = min(L, 256)
    return dict(
        proj_tm=min(B * L, 1024), proj_tn=2048,
        rnn_nb=B if B * T <= 512 else 512 // T, rnn_T=T,
        hg_nb=min(B, 4), hg_T=T,
        xa_T=min(L, 2048),
        merge_tm=256,
        ffn_nb=B if B * T <= 512 else 512 // T, ffn_T=T, ffn_tf=512,
    )


def _trunk_layer(x, mk, mv, h0, rc0, s0, fc0, w, layer):
    B, L, d = x.shape
    n = B * L
    t = _tiles(B, L)
    x2 = x.reshape(n, d)
    made = {}
    tm, tn = t["proj_tm"], t["proj_tn"]
    if w["w_in"].dtype == F32:
        z, made["w_in"] = _norm_matmul(x2, w["pre_mix_norm"], w["w_in"], tm, tn // 2, "in_proj",
                                       tiles=(0, 1), emit_w=True)
        if n > tm:
            z = _norm_matmul(x2, w["pre_mix_norm"], made["w_in"], tm, tn, "in_proj",
                             tiles=(1, n // tm), into=z)
    else:
        z = _norm_matmul(x2, w["pre_mix_norm"], w["w_in"], tm, tn, "in_proj")
    z3 = z.reshape(B, L, IN_COLS)
    c = BRANCH_WIDTH
    fresh = w["w_out"].dtype == F32
    ya, h1, rc1, *cast = _rglru(
        z3, rc0, h0, w["rnn_conv_w"], w["rnn_conv_b"], w["lru_wa"], w["lru_ba"], w["lru_wx"],
        w["lru_bx"], w["lru_lambda"], t["rnn_nb"], t["rnn_T"],
        casts=(w["w_branch"].reshape(N_BRANCH * c, d), w["w_out"]) if fresh else ())
    if fresh:
        made["w_branch"], made["w_out"] = cast[0].reshape(N_BRANCH, c, d), cast[1]
    yb, s1, *cast = _hgrn2(z3, s0, w["hg_lb"], w["hg_norm"], layer, t["hg_nb"], t["hg_T"],
                           casts=(w["w_ffn_up"], w["w_ffn_down"]) if fresh else ())
    if fresh:
        made["w_ffn_up"], made["w_ffn_down"] = cast
    w = {**w, **made}
    yc = _xattn(z3, mk, mv, t["xa_T"]) if mv is not None else _xattn_cached(z3, *mk, layer)
    x1 = _merge(x2, ya.reshape(n, c), yb.reshape(n, c), yc.reshape(n, c), z, w["b_gate"],
                w["w_branch"], w["w_out"], w["post_mix_norm"], t["merge_tm"]).reshape(B, L, d)
    xo, fc1 = _ffn(x1, w["pre_ffn_norm"], w["w_ffn_up"], w["ffn_conv_w"], w["ffn_conv_b"], fc0,
                   w["w_ffn_down"], w["post_ffn_norm"], t["ffn_nb"], t["ffn_T"], t["ffn_tf"])
    return xo, h1, rc1, s1, fc1, made


def kernel(x_prompt, x_sample, cache_mem_k, cache_mem_v, state_rnn_h, state_rnn_conv, state_hg,
           state_ffn_conv, mem_prompt, pre_mix_norm, w_in, rnn_conv_w, rnn_conv_b, lru_wa, lru_ba,
           lru_wx, lru_bx, lru_lambda, hg_lb, hg_norm, mem_norm, w_mem_kv, w_branch, b_gate, w_out,
           post_mix_norm, pre_ffn_norm, w_ffn_up, ffn_conv_w, ffn_conv_b, w_ffn_down, post_ffn_norm):
    depth = w_in.shape[0]
    Bp = x_prompt.shape[0]
    Bs = x_sample.shape[0]
    xa_w = XA_HEADS * XA_HD
    yp, ys = x_prompt, x_sample
    outs = [[] for _ in range(10)]
    for l in range(depth):
        w = {
            "pre_mix_norm": pre_mix_norm[l], "w_in": w_in[l],
            "rnn_conv_w": rnn_conv_w[l], "rnn_conv_b": rnn_conv_b[l],
            "lru_wa": _block_diag(lru_wa[l]).astype(BF16), "lru_ba": lru_ba[l],
            "lru_wx": _block_diag(lru_wx[l]).astype(BF16), "lru_bx": lru_bx[l],
            "lru_lambda": lru_lambda[l], "hg_lb": hg_lb, "hg_norm": hg_norm[l],
            "w_branch": w_branch[l], "b_gate": b_gate[l],
            "w_out": w_out[l], "post_mix_norm": post_mix_norm[l],
            "pre_ffn_norm": pre_ffn_norm[l], "w_ffn_up": w_ffn_up[l],
            "ffn_conv_w": ffn_conv_w[l], "ffn_conv_b": ffn_conv_b[l],
            "w_ffn_down": w_ffn_down[l], "post_ffn_norm": post_ffn_norm[l],
        }
        mem2 = mem_prompt.reshape(Bp * MEM_LEN, D_MODEL)
        kv = _norm_matmul(mem2, mem_norm[l], w_mem_kv[l], Bp * MEM_LEN, 1024, "mem_kv")
        kv = kv.reshape(Bp, MEM_LEN, 2 * xa_w)
        mk_p, mv_p = kv[..., :xa_w], kv[..., xa_w:]
        yp, h_p, rc_p, s_p, fc_p, made = _trunk_layer(
            yp, mk_p, mv_p, jnp.zeros((Bp, RNN_WIDTH), F32),
            jnp.zeros((Bp, RNN_CONV_W - 1, RNN_WIDTH), F32),
            jnp.zeros((Bp, HG_HEADS, HG_DK, HG_DV), F32),
            jnp.zeros((Bp, FFN_CONV_W - 1, FFN_DIM), F32), w, l)
        w = {**w, **made}
        ys, h_s, rc_s, s_s, fc_s, _ = _trunk_layer(
            ys, (cache_mem_k, cache_mem_v), None,
            state_rnn_h[l], state_rnn_conv[l], state_hg[l], state_ffn_conv[l], w, l)
        layer_out = (mk_p.reshape(Bp, MEM_LEN, XA_HEADS, XA_HD), mv_p.reshape(Bp, MEM_LEN, XA_HEADS, XA_HD),
                     h_p, rc_p, s_p, fc_p, h_s, rc_s, s_s, fc_s)
        for acc, val in zip(outs, layer_out):
            acc.append(val)
    return (yp, ys) + tuple(jnp.stack(o) for o in outs)
```
